```python
import math
import jax
import jax.numpy as jnp
from jax import lax
import numpy as np

D_MODEL = 2048
BATCH = 8
SEQ = 4096
DEPTH = 2
DEC_BATCH = 16
DEC_SEQ = 16
PAST_LEN = 4096

CHUNK = 64
N_META = 16
Q_BLOCK = 128
N_EVEN = (DEPTH + 1) // 2
N_ODD = DEPTH // 2
H_A = 16
DH_A = 128
D_A = H_A * DH_A
H_B = 32
P_B = 64
D_SSM = H_B * P_B
G_B = 4
HG_B = H_B // G_B
N_B = 128
CONV_W = 4
CONV_DIM = D_SSM + 2 * G_B * N_B
H_C = 8
DK_C = 128
DV_C = 256
D_QK_C = H_C * DK_C
D_V_C = H_C * DV_C
D_FF = -(-8 * D_MODEL // (3 * 256)) * 256
E_A = 3 * D_A + H_A + D_SSM + CONV_DIM + H_B
E_C = 2 * D_QK_C + 2 * D_V_C + 2 * H_C
ALPHA = (2 * DEPTH) ** 0.25
BETA = (8 * DEPTH) ** -0.25
NEG = -1e30

kernel_name = 'fox_ssd_mlstm_streaming_step'


def _split(p, sizes):
    idx = [int(s) for s in np.cumsum(sizes)[:-1]]
    return jnp.split(p, idx, axis=-1)


def _front_pad(a, pad, value=0.0):
    widths = [(0, 0), (pad, 0)] + [(0, 0)] * (a.ndim - 2)
    return jnp.pad(a, widths, constant_values=value)


def layer_norm(x, g, b, eps=1e-5):
    xf = x.astype(jnp.float32)
    xc = xf - jnp.mean(xf, -1, keepdims=True)
    var = jnp.mean(xc * xc, -1, keepdims=True)
    return (xc * lax.rsqrt(var + eps) * g + b).astype(x.dtype)


def group_rms_norm(y, g, groups, eps=1e-5):
    shp = y.shape
    yg = y.astype(jnp.float32).reshape(shp[:-1] + (groups, shp[-1] // groups))
    yg = yg * lax.rsqrt(jnp.mean(yg * yg, -1, keepdims=True) + eps)
    return yg.reshape(shp) * g


def swiglu(x, wg, wu, wd):
    h = jax.nn.silu(jnp.einsum('btd,df->btf', x, wg)) * jnp.einsum('btd,df->btf', x, wu)
    return jnp.einsum('btf,fd->btd', h, wd)


def fox_attention(q, k_all, v_all, logf_all):
    bsz, t = q.shape[:2]
    s_len = k_all.shape[1]
    past = s_len - t
    f_cum = jnp.cumsum(logf_all.astype(jnp.float32), axis=1)
    f_k = jnp.transpose(f_cum, (0, 2, 1))
    f_q = f_cum[:, past:]
    k_pos = jnp.arange(s_len)
    scale = DH_A ** -0.5

    def attend(q_blk, fq_blk, qpos_blk):
        s = jnp.einsum('bqhd,bkhd->bhqk', q_blk, k_all, preferred_element_type=jnp.float32) * scale
        s = s + jnp.transpose(fq_blk, (0, 2, 1))[..., None] - f_k[:, :, None, :]
        s = jnp.where(k_pos[None, :] <= qpos_blk[:, None], s, NEG)
        p = jax.nn.softmax(s, axis=-1)
        return jnp.einsum('bhqk,bkhd->bqhd', p.astype(v_all.dtype), v_all)

    if t <= Q_BLOCK:
        return attend(q, f_q, past + jnp.arange(t))
    nb = -(-t // Q_BLOCK)
    pad = nb * Q_BLOCK - t
    q_p = jnp.pad(q, ((0, 0), (0, pad), (0, 0), (0, 0)))
    fq_p = jnp.pad(f_q, ((0, 0), (0, pad), (0, 0)))
    q_b = jnp.moveaxis(q_p.reshape(bsz, nb, Q_BLOCK, H_A, DH_A), 1, 0)
    fq_b = jnp.moveaxis(fq_p.reshape(bsz, nb, Q_BLOCK, H_A), 1, 0)
    pos_b = (past + jnp.arange(nb * Q_BLOCK)).reshape(nb, Q_BLOCK)
    out = lax.map(lambda a: attend(a[0], a[1], a[2]), (q_b, fq_b, pos_b))
    out = jnp.moveaxis(out, 0, 1).reshape(bsz, nb * Q_BLOCK, H_A, DH_A)
    return out[:, :t]


def causal_depthwise_conv(xp, w, b, t):
    out = b
    for i in range(CONV_W):
        out = out + w[i] * xp[:, i:i + t]
    return out


def ssd_chunked(x, dt, a, bm, cm, h0):
    bsz, t = x.shape[:2]
    nc = -(-t // CHUNK)
    pad = nc * CHUNK - t
    f32 = jnp.float32
    x = _front_pad(x.astype(f32), pad).reshape(bsz, nc, CHUNK, G_B, HG_B, P_B)
    dt = _front_pad(dt.astype(f32), pad).reshape(bsz, nc, CHUNK, G_B, HG_B)
    bm = _front_pad(bm.astype(f32), pad).reshape(bsz, nc, CHUNK, G_B, N_B)
    cm = _front_pad(cm.astype(f32), pad).reshape(bsz, nc, CHUNK, G_B, N_B)
    dt_t = jnp.moveaxis(dt, 2, -1)
    a_cum = jnp.cumsum(dt_t * a[:, :, None], axis=-1)
    causal = jnp.tril(jnp.ones((CHUNK, CHUNK), bool))
    decay = jnp.exp(jnp.where(causal, a_cum[..., :, None] - a_cum[..., None, :], NEG))
    cb = jnp.einsum('bclgn,bcsgn->bcgls', cm, bm)
    w = cb[:, :, :, None] * decay * dt_t[..., None, :]
    y_diag = jnp.einsum('bcghls,bcsghp->bclghp', w, x)
    to_end = jnp.exp(a_cum[..., -1:] - a_cum) * dt_t
    states = jnp.einsum('bclgn,bcghl,bclghp->bcghpn', bm, to_end, x)
    chunk_decay = jnp.exp(a_cum[..., -1])

    def step(h, inp):
        dec, st = inp
        return dec[..., None, None] * h + st, h

    h_final, h_in = lax.scan(step, h0.astype(f32), (jnp.moveaxis(chunk_decay, 1, 0), jnp.moveaxis(states, 1, 0)))
    h_in = jnp.moveaxis(h_in, 0, 1)
    y_off = jnp.einsum('bclgn,bcghpn,bcghl->bclghp', cm, h_in, jnp.exp(a_cum))
    y = (y_diag + y_off).reshape(bsz, nc * CHUNK, G_B, HG_B, P_B)[:, pad:]
    return y, h_final


def mlstm_chunked(q, k, v, logi, logf, c0, n0, m0):
    bsz, t = q.shape[:2]
    nc = -(-t // CHUNK)
    pad = nc * CHUNK - t

    def chunks(a, value=0.0):
        a = _front_pad(a, pad, value)
        a = a.reshape((bsz, nc, CHUNK) + a.shape[2:])
        return jnp.moveaxis(jnp.moveaxis(a, 1, 0), 2, 3)

    qc, kc, vc = chunks(q), chunks(k), chunks(v)
    li, lf = chunks(logi, NEG), chunks(logf)
    causal = jnp.tril(jnp.ones((CHUNK, CHUNK), bool))

    def step(carry, inp):
        c_st, n_st, m_st = carry
        qb, kb, vb, lib, lfb = inp
        b_cum = jnp.cumsum(lfb, -1)
        d_mat = jnp.where(causal, b_cum[..., :, None] - b_cum[..., None, :] + lib[..., None, :], NEG)
        inter = b_cum + m_st[..., None]
        m_row = jnp.maximum(inter, jnp.max(d_mat, -1))
        w = jnp.exp(d_mat - m_row[..., None]) * jnp.einsum('bhld,bhsd->bhls', qb, kb)
        g_inter = jnp.exp(inter - m_row)
        num = jnp.einsum('bhls,bhsv->bhlv', w, vb) + g_inter[..., None] * jnp.einsum('bhvd,bhld->bhlv', c_st, qb)
        den = jnp.sum(w, -1) + g_inter * jnp.einsum('bhd,bhld->bhl', n_st, qb)
        h = num / jnp.maximum(jnp.abs(den), jnp.exp(-m_row))[..., None]
        b_tot = b_cum[..., -1]
        d_end = b_tot[..., None] - b_cum + lib
        m_new = jnp.maximum(b_tot + m_st, jnp.max(d_end, -1))
        w_end = jnp.exp(d_end - m_new[..., None])
        g_old = jnp.exp(b_tot + m_st - m_new)
        c_new = g_old[..., None, None] * c_st + jnp.einsum('bhs,bhsv,bhsd->bhvd', w_end, vb, kb)
        n_new = g_old[..., None] * n_st + jnp.einsum('bhs,bhsd->bhd', w_end, kb)
        return (c_new, n_new, m_new), h

    (c_f, n_f, m_f), h = lax.scan(step, (c0, n0, m0), (qc, kc, vc, li, lf))
    h = jnp.transpose(h, (1, 0, 3, 2, 4)).reshape(bsz, nc * CHUNK, H_C, DV_C)[:, pad:]
    return h, c_f, n_f, m_f


def even_mixer(x, k_past, v_past, logf_past, conv_past, h0, w_in, b_f, conv_w, conv_b, dt_bias, a_log, d_skip, norm_g, w_out):
    bsz, t, _ = x.shape
    f32 = jnp.float32
    proj = jnp.einsum('btd,de->bte', x, w_in)
    q, k, v, fg, z, xbc, dt_raw = _split(proj, [D_A, D_A, D_A, H_A, D_SSM, CONV_DIM, H_B])
    q = q.reshape(bsz, t, H_A, DH_A)
    k = k.reshape(bsz, t, H_A, DH_A)
    v = v.reshape(bsz, t, H_A, DH_A)
    logf = jax.nn.log_sigmoid((fg + b_f).astype(f32))
    k_all = jnp.concatenate([k_past.astype(k.dtype), k], axis=1)
    v_all = jnp.concatenate([v_past.astype(v.dtype), v], axis=1)
    logf_all = jnp.concatenate([logf_past.astype(f32), logf], axis=1)
    attn = fox_attention(q, k_all, v_all, logf_all).reshape(bsz, t, D_A)
    xbc_ext = jnp.concatenate([conv_past.astype(xbc.dtype), xbc], axis=1)
    xbc_c = jax.nn.silu(causal_depthwise_conv(xbc_ext, conv_w, conv_b, t))
    xs, bm, cm = _split(xbc_c, [D_SSM, G_B * N_B, G_B * N_B])
    dt = jax.nn.softplus((dt_raw + dt_bias).astype(f32))
    a = -jnp.exp(a_log.astype(f32))
    xs_h = xs.reshape(bsz, t, G_B, HG_B, P_B)
    y, h_new = ssd_chunked(xs_h, dt.reshape(bsz, t, G_B, HG_B), a.reshape(G_B, HG_B),
                           bm.reshape(bsz, t, G_B, N_B), cm.reshape(bsz, t, G_B, N_B),
                           h0.reshape(bsz, G_B, HG_B, P_B, N_B))
    y = y + d_skip.reshape(G_B, HG_B)[:, :, None].astype(f32) * xs_h.astype(f32)
    y = y.reshape(bsz, t, D_SSM) * jax.nn.silu(z.astype(f32))
    y = group_rms_norm(y, norm_g, G_B).astype(x.dtype)
    out = jnp.einsum('bte,ed->btd', jnp.concatenate([attn, y], axis=-1), w_out)
    return out, k, v, logf, xbc_ext[:, -(CONV_W - 1):], h_new.reshape(bsz, H_B, P_B, N_B)


def odd_mixer(x, c0, n0, m0, w_in, b_i, b_f, norm_g, w_out):
    bsz, t, _ = x.shape
    f32 = jnp.float32
    proj = jnp.einsum('btd,de->bte', x, w_in)
    q, k, v, o, ig, fg = _split(proj, [D_QK_C, D_QK_C, D_V_C, D_V_C, H_C, H_C])
    q = q.reshape(bsz, t, H_C, DK_C).astype(f32)
    k = k.reshape(bsz, t, H_C, DK_C).astype(f32) * (DK_C ** -0.5)
    v = v.reshape(bsz, t, H_C, DV_C).astype(f32)
    logi = (ig + b_i).astype(f32)
    logf = jax.nn.log_sigmoid((fg + b_f).astype(f32))
    h, c_new, n_new, m_new = mlstm_chunked(q, k, v, logi, logf, c0.astype(f32), n0.astype(f32), m0.astype(f32))
    h = group_rms_norm(h.reshape(bsz, t, D_V_C), norm_g, H_C) * jax.nn.sigmoid(o.astype(f32))
    out = jnp.einsum('bte,ed->btd', h.astype(x.dtype), w_out)
    return out, c_new, n_new, m_new


def post_norm_layer(x, mix, g1, b1, wg, wu, wd, g2, b2):
    x = layer_norm(ALPHA * x + mix, g1, b1)
    return layer_norm(ALPHA * x + swiglu(x, wg, wu, wd), g2, b2)


def setup_inputs(seed: int = 0) -> dict:
    key = jax.random.key(seed)
    ks = jax.random.split(key, 32)
    f32 = jnp.float32

    def nrm(k, shape, scale):
        return jax.random.normal(k, shape, f32) * scale

    dt_init = jnp.exp(jax.random.uniform(ks[15], (N_EVEN, H_B), f32, math.log(1e-3), math.log(1e-1)))
    return {
        'x_prompt': nrm(ks[0], (BATCH, SEQ, D_MODEL), 1.0),
        'x_sample': nrm(ks[1], (DEC_BATCH, DEC_SEQ, D_MODEL), 1.0),
        'cache_fox_k': nrm(ks[2], (N_EVEN, DEC_BATCH, PAST_LEN, H_A, DH_A), 1.0),
        'cache_fox_v': nrm(ks[3], (N_EVEN, DEC_BATCH, PAST_LEN, H_A, DH_A), 1.0),
        'cache_fox_logf': jax.nn.log_sigmoid(3.0 + nrm(ks[4], (N_EVEN, DEC_BATCH, PAST_LEN, H_A), 1.0)),
        'state_ssd_conv': nrm(ks[5], (N_EVEN, DEC_BATCH, CONV_W - 1, CONV_DIM), 1.0),
        'state_ssd': nrm(ks[6], (N_EVEN, DEC_BATCH, H_B, P_B, N_B), 0.5),
        'state_mlstm_c': nrm(ks[7], (N_ODD, DEC_BATCH, H_C, DV_C, DK_C), 0.5),
        'state_mlstm_n': nrm(ks[8], (N_ODD, DEC_BATCH, H_C, DK_C), 0.5),
        'state_mlstm_m': nrm(ks[9], (N_ODD, DEC_BATCH, H_C), 1.0),
        'meta_tokens': nrm(ks[10], (N_META, D_MODEL), 1.0),
        'w_in_a': nrm(ks[11], (N_EVEN, D_MODEL, E_A), D_MODEL ** -0.5),
        'b_fgate_a': 3.0 + nrm(ks[12], (N_EVEN, H_A), 0.1),
        'conv_w': nrm(ks[13], (N_EVEN, CONV_W, CONV_DIM), CONV_W ** -0.5),
        'conv_b': nrm(ks[14], (N_EVEN, CONV_DIM), 0.02),
        'dt_bias': dt_init + jnp.log(-jnp.expm1(-dt_init)),
        'a_log': jnp.log(jax.random.uniform(ks[16], (N_EVEN, H_B), f32, 1.0, 16.0)),
        'd_skip': 1.0 + nrm(ks[17], (N_EVEN, H_B), 0.1),
        'ssd_norm_g': 1.0 + nrm(ks[18], (N_EVEN, D_SSM), 0.02),
        'w_out_a': nrm(ks[19], (N_EVEN, D_A + D_SSM, D_MODEL), BETA * (D_A + D_SSM) ** -0.5),
        'w_in_c': nrm(ks[20], (N_ODD, D_MODEL, E_C), D_MODEL ** -0.5),
        'b_igate_c': nrm(ks[21], (N_ODD, H_C), 0.1),
        'b_fgate_c': jnp.linspace(3.0, 6.0, H_C, dtype=f32)[None] + nrm(ks[22], (N_ODD, H_C), 0.1),
        'mlstm_norm_g': 1.0 + nrm(ks[23], (N_ODD, D_V_C), 0.02),
        'w_out_c': nrm(ks[24], (N_ODD, D_V_C, D_MODEL), BETA * D_V_C ** -0.5),
        'ln_mix_g': 1.0 + nrm(ks[25], (DEPTH, D_MODEL), 0.02),
        'ln_mix_b': nrm(ks[26], (DEPTH, D_MODEL), 0.02),
        'ln_ffn_g': 1.0 + nrm(ks[27], (DEPTH, D_MODEL), 0.02),
        'ln_ffn_b': nrm(ks[28], (DEPTH, D_MODEL), 0.02),
        'w_ffn_gate': nrm(ks[29], (DEPTH, D_MODEL, D_FF), D_MODEL ** -0.5),
        'w_ffn_up': nrm(ks[30], (DEPTH, D_MODEL, D_FF), D_MODEL ** -0.5),
        'w_ffn_down': nrm(ks[31], (DEPTH, D_FF, D_MODEL), BETA * D_FF ** -0.5),
    }


def reference(x_prompt, x_sample, cache_fox_k, cache_fox_v, cache_fox_logf, state_ssd_conv, state_ssd,
              state_mlstm_c, state_mlstm_n, state_mlstm_m, meta_tokens, w_in_a, b_fgate_a, conv_w, conv_b,
              dt_bias, a_log, d_skip, ssd_norm_g, w_out_a, w_in_c, b_igate_c, b_fgate_c, mlstm_norm_g, w_out_c,
              ln_mix_g, ln_mix_b, ln_ffn_g, ln_ffn_b, w_ffn_gate, w_ffn_up, w_ffn_down):
    f32 = jnp.float32
    bsz = x_prompt.shape[0]
    dtype = x_prompt.dtype
    meta = jnp.broadcast_to(meta_tokens.astype(dtype)[None], (bsz, N_META, D_MODEL))
    xp = jnp.concatenate([meta, x_prompt], axis=1)
    xs = x_sample
    zk = jnp.zeros((bsz, 0, H_A, DH_A), dtype)
    zf = jnp.zeros((bsz, 0, H_A), f32)
    zconv = jnp.zeros((bsz, CONV_W - 1, CONV_DIM), dtype)
    zh = jnp.zeros((bsz, H_B, P_B, N_B), f32)
    zc = jnp.zeros((bsz, H_C, DV_C, DK_C), f32)
    zn = jnp.zeros((bsz, H_C, DK_C), f32)
    zm = jnp.zeros((bsz, H_C), f32)
    kp_l, vp_l, fp_l, cvp_l, hp_l, cp_l, np_l, mp_l = [], [], [], [], [], [], [], []
    ks_l, vs_l, fs_l, cvs_l, hs_l, cs_l, ns_l, ms_l = [], [], [], [], [], [], [], []
    for layer in range(DEPTH):
        if layer % 2 == 0:
            e = layer // 2
            wts = (w_in_a[e], b_fgate_a[e], conv_w[e], conv_b[e], dt_bias[e], a_log[e], d_skip[e], ssd_norm_g[e], w_out_a[e])
            mix_p, kp, vp, fp, cvp, hp = even_mixer(xp, zk, zk, zf, zconv, zh, *wts)
            mix_s, kss, vss, fss, cvs, hss = even_mixer(xs, cache_fox_k[e], cache_fox_v[e], cache_fox_logf[e],
                                                        state_ssd_conv[e], state_ssd[e], *wts)
            kp_l.append(kp); vp_l.append(vp); fp_l.append(fp); cvp_l.append(cvp); hp_l.append(hp)
            ks_l.append(kss); vs_l.append(vss); fs_l.append(fss); cvs_l.append(cvs); hs_l.append(hss)
        else:
            o = layer // 2
            wts = (w_in_c[o], b_igate_c[o], b_fgate_c[o], mlstm_norm_g[o], w_out_c[o])
            mix_p, cp, np_, mp = odd_mixer(xp, zc, zn, zm, *wts)
            mix_s, css, nss, mss = odd_mixer(xs, state_mlstm_c[o], state_mlstm_n[o], state_mlstm_m[o], *wts)
            cp_l.append(cp); np_l.append(np_); mp_l.append(mp)
            cs_l.append(css); ns_l.append(nss); ms_l.append(mss)
        xp = post_norm_layer(xp, mix_p, ln_mix_g[layer], ln_mix_b[layer], w_ffn_gate[layer], w_ffn_up[layer],
                             w_ffn_down[layer], ln_ffn_g[layer], ln_ffn_b[layer])
        xs = post_norm_layer(xs, mix_s, ln_mix_g[layer], ln_mix_b[layer], w_ffn_gate[layer], w_ffn_up[layer],
                             w_ffn_down[layer], ln_ffn_g[layer], ln_ffn_b[layer])
    y_prompt = xp[:, N_META:]
    y_sample = xs
    return (y_prompt, y_sample,
            jnp.stack(kp_l), jnp.stack(vp_l), jnp.stack(fp_l), jnp.stack(cvp_l), jnp.stack(hp_l),
            jnp.stack(cp_l), jnp.stack(np_l), jnp.stack(mp_l),
            jnp.stack(ks_l), jnp.stack(vs_l), jnp.stack(fs_l), jnp.stack(cvs_l), jnp.stack(hs_l),
            jnp.stack(cs_l), jnp.stack(ns_l), jnp.stack(ms_l))
```

```python
import functools

import jax
import jax.numpy as jnp
import numpy as np
from jax import lax
from jax.experimental import pallas as pl
from jax.experimental.pallas import tpu as pltpu

f32 = jnp.float32
bf16 = jnp.bfloat16

D_MODEL = 2048
DEPTH = 2
N_META = 16
H_A, DH_A = 16, 128
D_A = H_A * DH_A
H_B, P_B, G_B, N_B = 32, 64, 4, 128
HG_B = H_B // G_B
D_SSM = H_B * P_B
CONV_W = 4
CONV_DIM = D_SSM + 2 * G_B * N_B
H_C, DK_C, DV_C = 8, 128, 256
D_QK_C = H_C * DK_C
D_V_C = H_C * DV_C
ALPHA = (2 * DEPTH) ** 0.25
NEG = -1e30
LN_EPS = 1e-5

LANE = 128
SUBLANE = 8
SEQ_PAD = 128
VMEM_LIMIT = 56 * 1024 * 1024


def _cparams(sem):
    return pltpu.CompilerParams(dimension_semantics=sem, vmem_limit_bytes=VMEM_LIMIT)


def _tile(n, pref, mult=16):
    if n <= pref:
        return n
    t = (pref // mult) * mult
    while t >= mult:
        if n % t == 0:
            return t
        t -= mult
    return n


def _softplus(x):
    return jnp.maximum(x, 0.0) + jnp.log1p(jnp.exp(-jnp.abs(x)))


def _log_sigmoid(x):
    return -_softplus(-x)


def _sigmoid(x):
    return 1.0 / (1.0 + jnp.exp(-x))


def _split3(a):
    a1 = a.astype(bf16)
    r1 = a - a1.astype(f32)
    a2 = r1.astype(bf16)
    a3 = (r1 - a2.astype(f32)).astype(bf16)
    return a1, a2, a3


_NN = (((1,), (0,)), ((), ()))
_NT = (((1,), (1,)), ((), ()))


def _dot(a, b, dims=_NN):
    return lax.dot_general(a, b, dims, preferred_element_type=f32)


def _dot_exact_rhs(a_bf16_exact, b_f32, dims=_NN):
    b1, b2, b3 = _split3(b_f32)
    return _dot(a_bf16_exact, b1, dims) + _dot(a_bf16_exact, b2, dims) + _dot(a_bf16_exact, b3, dims)


def _dot_exact_lhs(a_f32, b_bf16_exact, dims=_NN):
    a1, a2, a3 = _split3(a_f32)
    return _dot(a1, b_bf16_exact, dims) + _dot(a2, b_bf16_exact, dims) + _dot(a3, b_bf16_exact, dims)


def _iota2(shape, axis):
    return lax.broadcasted_iota(jnp.int32, shape, axis)


def _tri_lower(n):
    return (_iota2((n, n), 0) >= _iota2((n, n), 1)).astype(bf16)


def _tri_upper(n):
    return (_iota2((n, n), 0) <= _iota2((n, n), 1)).astype(bf16)


def _eye(rows, cols):
    return (_iota2((rows, cols), 0) == _iota2((rows, cols), 1)).astype(bf16)


def _layer_norm(x, g, b):
    mu = jnp.mean(x, axis=-1, keepdims=True)
    xc = x - mu
    var = jnp.mean(xc * xc, axis=-1, keepdims=True)
    return xc * lax.rsqrt(var + LN_EPS) * g + b


def _mm_kernel(x_ref, w_ref, s_ref, *out_refs):
    acc = _dot(x_ref[...], w_ref[...]) * s_ref[...]
    for o in out_refs:
        o[...] = acc.astype(o.dtype)


def _matmul(x, w, out_dtypes, col_scale=None, tm_pref=1024, tn_pref=512):
    m, k = x.shape
    n = w.shape[1]
    tm = _tile(m, tm_pref)
    tn = _tile(n, tn_pref, LANE)
    if col_scale is None:
        col_scale = jnp.ones((1, n), f32)
    return pl.pallas_call(
        _mm_kernel,
        grid=(m // tm, n // tn),
        in_specs=[pl.BlockSpec((tm, k), lambda i, j: (i, 0)),
                  pl.BlockSpec((k, tn), lambda i, j: (0, j)),
                  pl.BlockSpec((1, tn), lambda i, j: (0, j))],
        out_specs=[pl.BlockSpec((tm, tn), lambda i, j: (i, j)) for _ in out_dtypes],
        out_shape=[jax.ShapeDtypeStruct((m, n), dt) for dt in out_dtypes],
        compiler_params=_cparams(("parallel", "arbitrary")),
        name="proj_matmul",
    )(x, w, col_scale)


def _outproj_ln_kernel(*refs, n_act, kc):
    acts = refs[:n_act]
    w_ref, res_ref, g_ref, b_ref, of_ref, ob_ref, acc_ref = refs[n_act:]
    k = pl.program_id(1)

    @pl.when(k == 0)
    def _():
        acc_ref[...] = jnp.zeros_like(acc_ref)

    for a in range(n_act):
        @pl.when((k >= a * kc) & (k < (a + 1) * kc))
        def _(a=a):
            acc_ref[...] += _dot(acts[a][...], w_ref[...])

    @pl.when(k == pl.num_programs(1) - 1)
    def _():
        y = _layer_norm(ALPHA * res_ref[...] + acc_ref[...], g_ref[...], b_ref[...])
        of_ref[...] = y
        ob_ref[...] = y.astype(bf16)


def _outproj_ln(acts, w, res, g, b, tm_pref=512, tk=1024):
    m, d = res.shape
    tm = _tile(m, tm_pref)
    kc = acts[0].shape[1] // tk
    n_act = len(acts)
    in_specs = [pl.BlockSpec((tm, tk), (lambda i, k, a=a: (i, jnp.clip(k - a * kc, 0, kc - 1))))
                for a in range(n_act)]
    in_specs += [pl.BlockSpec((tk, d), lambda i, k: (k, 0)),
                 pl.BlockSpec((tm, d), lambda i, k: (i, 0)),
                 pl.BlockSpec((1, d), lambda i, k: (0, 0)),
                 pl.BlockSpec((1, d), lambda i, k: (0, 0))]
    return pl.pallas_call(
        functools.partial(_outproj_ln_kernel, n_act=n_act, kc=kc),
        grid=(m // tm, n_act * kc),
        in_specs=in_specs,
        out_specs=[pl.BlockSpec((tm, d), lambda i, k: (i, 0)),
                   pl.BlockSpec((tm, d), lambda i, k: (i, 0))],
        out_shape=[jax.ShapeDtypeStruct((m, d), f32), jax.ShapeDtypeStruct((m, d), bf16)],
        scratch_shapes=[pltpu.VMEM((tm, d), f32)],
        compiler_params=_cparams(("parallel", "arbitrary")),
        name="outproj_ln",
    )(*acts, w, res, g.reshape(1, d), b.reshape(1, d))


def _ffn_ln_kernel(xb_ref, xf_ref, wg_ref, wu_ref, wd_ref, g_ref, b_ref, of_ref, ob_ref, acc_ref):
    f = pl.program_id(1)

    @pl.when(f == 0)
    def _():
        acc_ref[...] = jnp.zeros_like(acc_ref)

    x = xb_ref[...]
    gate = _dot(x, wg_ref[...])
    up = _dot(x, wu_ref[...])
    h = (gate * _sigmoid(gate) * up).astype(bf16)
    acc_ref[...] += _dot(h, wd_ref[...])

    @pl.when(f == pl.num_programs(1) - 1)
    def _():
        y = _layer_norm(ALPHA * xf_ref[...] + acc_ref[...], g_ref[...], b_ref[...])
        of_ref[...] = y
        ob_ref[...] = y.astype(bf16)


def _ffn_ln(xb, xf, wg, wu, wd, g, b, tm_pref=512, tf_pref=512):
    m, d = xf.shape
    dff = wg.shape[1]
    tm = _tile(m, tm_pref)
    tf = _tile(dff, tf_pref, LANE)
    return pl.pallas_call(
        _ffn_ln_kernel,
        grid=(m // tm, dff // tf),
        in_specs=[pl.BlockSpec((tm, d), lambda i, f: (i, 0)),
                  pl.BlockSpec((tm, d), lambda i, f: (i, 0)),
                  pl.BlockSpec((d, tf), lambda i, f: (0, f)),
                  pl.BlockSpec((d, tf), lambda i, f: (0, f)),
                  pl.BlockSpec((tf, d), lambda i, f: (f, 0)),
                  pl.BlockSpec((1, d), lambda i, f: (0, 0)),
                  pl.BlockSpec((1, d), lambda i, f: (0, 0))],
        out_specs=[pl.BlockSpec((tm, d), lambda i, f: (i, 0)),
                   pl.BlockSpec((tm, d), lambda i, f: (i, 0))],
        out_shape=[jax.ShapeDtypeStruct((m, d), f32), jax.ShapeDtypeStruct((m, d), bf16)],
        scratch_shapes=[pltpu.VMEM((tm, d), f32)],
        compiler_params=_cparams(("parallel", "arbitrary")),
        name="ffn_ln",
    )(xb, xf, wg, wu, wd, g.reshape(1, d), b.reshape(1, d))


def _cumsum_kernel(x_ref, bias_ref, carry_ref, logf_ref, cum_ref, acc_ref, *, gate):
    c = pl.program_id(1)

    @pl.when(c == 0)
    def _():
        acc_ref[...] = carry_ref[0]

    x = x_ref[0]
    lf = _log_sigmoid(x + bias_ref[...]) if gate else x
    tc = x.shape[0]
    cum = _dot_exact_rhs(_tri_lower(tc), lf) + acc_ref[...]
    logf_ref[0] = lf
    cum_ref[0] = cum
    acc_ref[...] = cum[tc - 1:tc, :]


def _logf_cumsum(x, bias, carry, gate, tc_pref=512):
    bsz, t, w = x.shape
    tc = _tile(t, tc_pref)
    spec = pl.BlockSpec((1, tc, w), lambda b, c: (b, c, 0))
    return pl.pallas_call(
        functools.partial(_cumsum_kernel, gate=gate),
        grid=(bsz, t // tc),
        in_specs=[spec,
                  pl.BlockSpec((1, w), lambda b, c: (0, 0)),
                  pl.BlockSpec((1, 1, w), lambda b, c: (b, 0, 0))],
        out_specs=[spec, spec],
        out_shape=[jax.ShapeDtypeStruct((bsz, t, w), f32)] * 2,
        scratch_shapes=[pltpu.VMEM((1, w), f32)],
        compiler_params=_cparams(("parallel", "arbitrary")),
        name="logf_cumsum",
    )(x, bias, carry)


def _attn_kernel(*refs, tq, tk1, s1, s1_valid, scale):
    if s1:
        q_ref, k1_ref, v1_ref, f1_ref, k2_ref, v2_ref, f2_ref, o_ref = refs
    else:
        q_ref, k2_ref, v2_ref, f2_ref, o_ref = refs
    qi = pl.program_id(2)
    q = q_ref[0]
    d = q.shape[-1]

    def block(carry, k, v, fk, mask):
        m, l, acc = carry
        s = _dot(q, k.astype(bf16), _NT) * scale - fk
        if mask is not None:
            s = jnp.where(mask, s, NEG)
        m_new = jnp.maximum(m, jnp.max(s, axis=-1, keepdims=True))
        alpha = jnp.exp(m - m_new)
        p = jnp.exp(s - m_new)
        l = alpha * l + jnp.sum(p, axis=-1, keepdims=True)
        acc = alpha * acc + _dot(p.astype(bf16), v.astype(bf16))
        return m_new, l, acc

    carry = (jnp.full((tq, 1), NEG, f32), jnp.zeros((tq, 1), f32), jnp.zeros((tq, d), f32))

    if s1:
        n1 = s1 // tk1
        if n1 == 1:
            mask = None if s1_valid == s1 else (_iota2((tq, tk1), 1) < s1_valid)
            carry = block(carry, k1_ref[0], v1_ref[0], f1_ref[0, 0], mask)
        else:
            def body1(j, c):
                off = pl.multiple_of(j * tk1, tk1)
                return block(c, k1_ref[0, pl.ds(off, tk1), :], v1_ref[0, pl.ds(off, tk1), :],
                             f1_ref[0, 0, :, pl.ds(off, tk1)], None)
            carry = lax.fori_loop(0, n1, body1, carry)

    def body2(j, c):
        off = pl.multiple_of(j * tq, tq)
        return block(c, k2_ref[0, pl.ds(off, tq), :], v2_ref[0, pl.ds(off, tq), :],
                     f2_ref[0, 0, :, pl.ds(off, tq)], None)

    carry = lax.fori_loop(0, qi, body2, carry)
    off = pl.multiple_of(qi * tq, tq)
    causal = _iota2((tq, tq), 0) >= _iota2((tq, tq), 1)
    m, l, acc = block(carry, k2_ref[0, pl.ds(off, tq), :], v2_ref[0, pl.ds(off, tq), :],
                      f2_ref[0, 0, :, pl.ds(off, tq)], causal)
    o_ref[0] = (acc / l).astype(o_ref.dtype)


def _fox_attention(q, k2, v2, f2, prefix=None, s1_valid=0, tq_pref=512, tk1_pref=512):
    bsz, t, hd = q.shape
    nh = hd // DH_A
    tq = _tile(t, tq_pref, LANE)
    qspec = pl.BlockSpec((1, tq, DH_A), lambda b, h, i: (b, i, h))
    kvspec = pl.BlockSpec((1, t, DH_A), lambda b, h, i: (b, 0, h))
    fspec = pl.BlockSpec((1, 1, 1, t), lambda b, h, i: (b, h, 0, 0))
    args, in_specs = [q], [qspec]
    s1 = tk1 = 0
    if prefix is not None:
        k1, v1, f1 = prefix
        s1 = k1.shape[1]
        tk1 = _tile(s1, tk1_pref, LANE)
        p_spec = pl.BlockSpec((1, s1, DH_A), lambda b, h, i: (b, 0, h))
        args += [k1, v1, f1]
        in_specs += [p_spec, p_spec, pl.BlockSpec((1, 1, 1, s1), lambda b, h, i: (b, h, 0, 0))]
    args += [k2, v2, f2]
    in_specs += [kvspec, kvspec, fspec]
    return pl.pallas_call(
        functools.partial(_attn_kernel, tq=tq, tk1=tk1, s1=s1, s1_valid=s1_valid, scale=DH_A ** -0.5),
        grid=(bsz, nh, t // tq),
        in_specs=in_specs,
        out_specs=qspec,
        out_shape=jax.ShapeDtypeStruct((bsz, t, hd), bf16),
        compiler_params=_cparams(("parallel", "parallel", "arbitrary")),
        name="fox_attention",
    )(*args)


def _ssd_kernel(zx_ref, dt_ref, hist_ref, h0_ref, cw_ref, cb_ref, dtb_ref, alog_ref, alogc_ref,
                dskip_ref, ng_ref, e_ref, y_ref, tail_ref, hout_ref, ext_ref, st_ref, *, lc, t_valid):
    c = pl.program_id(1)

    @pl.when(c == 0)
    def _():
        ext_ref[0:SUBLANE, :] = hist_ref[0]
        st_ref[...] = h0_ref[0]

    ext_ref[SUBLANE:SUBLANE + lc, :] = zx_ref[0, :, D_SSM:]
    conv = cb_ref[...]
    for i in range(CONV_W):
        lo = SUBLANE - (CONV_W - 1) + i
        conv = conv + cw_ref[i:i + 1, :] * ext_ref[lo:lo + lc, :]
    xc = conv * _sigmoid(conv)
    tail = ext_ref[t_valid:t_valid + SUBLANE, :]
    ext_ref[0:SUBLANE, :] = tail

    dt = _softplus(dt_ref[0] + dtb_ref[...])
    if t_valid < lc:
        dt = jnp.where(_iota2((lc, LANE), 0) < t_valid, dt, 0.0)
    a_row = -jnp.exp(alog_ref[...])
    a_col = -jnp.exp(alogc_ref[...])
    a_cum = _dot_exact_rhs(_tri_lower(lc), dt * a_row)
    dt_t = _dot_exact_rhs(_eye(H_B, LANE), dt, _NT)
    a_cum_t = _dot_exact_lhs(dt_t * a_col, _tri_upper(lc))
    a_last = a_cum[lc - 1:lc, :]
    e = e_ref[...]
    to_end_x = _dot_exact_lhs(jnp.exp(a_last - a_cum) * dt, e)
    ea_x = _dot_exact_lhs(jnp.exp(a_cum), e)
    cdec_x = _dot_exact_lhs(jnp.broadcast_to(jnp.exp(a_last), (SUBLANE, LANE)), e)[0:1, :]

    causal = _iota2((lc, lc), 0) >= _iota2((lc, lc), 1)
    left = _iota2((lc, LANE), 1) < P_B
    eye_n = _eye(N_B, N_B)
    gw = D_SSM // G_B
    for g in range(G_B):
        bg = xc[:, D_SSM + g * N_B:D_SSM + (g + 1) * N_B].astype(bf16)
        cg = xc[:, D_SSM + G_B * N_B + g * N_B:D_SSM + G_B * N_B + (g + 1) * N_B].astype(bf16)
        cb = _dot(cg, bg, _NT)
        bg_t = _dot(eye_n, bg, _NT).astype(bf16)
        xg = xc[:, g * gw:(g + 1) * gw]
        st_g = st_ref[:, g * gw:(g + 1) * gw]
        y_off = _dot(cg, st_g.astype(bf16)) * ea_x[:, g * gw:(g + 1) * gw]
        upd = _dot(bg_t, (xg * to_end_x[:, g * gw:(g + 1) * gw]).astype(bf16))
        st_ref[:, g * gw:(g + 1) * gw] = cdec_x[:, g * gw:(g + 1) * gw] * st_g + upd
        pairs = []
        for j in range(gw // LANE):
            xp = xg[:, j * LANE:(j + 1) * LANE].astype(bf16)
            res = []
            for h in (g * HG_B + 2 * j, g * HG_B + 2 * j + 1):
                seg = a_cum[:, h:h + 1] - a_cum_t[h:h + 1, :]
                dec = jnp.exp(jnp.where(causal, seg, NEG))
                w = (cb * dec * dt_t[h:h + 1, :]).astype(bf16)
                res.append(_dot(w, xp))
            pairs.append(jnp.where(left, res[0], res[1]))
        y = jnp.concatenate(pairs, axis=-1) + y_off + dskip_ref[:, g * gw:(g + 1) * gw] * xg
        z = zx_ref[0, :, g * gw:(g + 1) * gw]
        y = y * (z * _sigmoid(z))
        y = y * lax.rsqrt(jnp.mean(y * y, axis=-1, keepdims=True) + LN_EPS) * ng_ref[:, g * gw:(g + 1) * gw]
        y_ref[0, :, g * gw:(g + 1) * gw] = y.astype(y_ref.dtype)

    @pl.when(c == pl.num_programs(1) - 1)
    def _():
        tail_ref[0] = tail
        hout_ref[0] = st_ref[...]


def _ssd_mixer(zx, dt_raw, hist, h0_t, p, lc, t_valid):
    bsz, t, wz = zx.shape
    const = lambda shape: pl.BlockSpec(shape, lambda b, c: (0,) * len(shape))
    return pl.pallas_call(
        functools.partial(_ssd_kernel, lc=lc, t_valid=t_valid),
        grid=(bsz, t // lc),
        in_specs=[pl.BlockSpec((1, lc, wz), lambda b, c: (b, c, 0)),
                  pl.BlockSpec((1, lc, LANE), lambda b, c: (b, c, 0)),
                  pl.BlockSpec((1, SUBLANE, CONV_DIM), lambda b, c: (b, 0, 0)),
                  pl.BlockSpec((1, N_B, D_SSM), lambda b, c: (b, 0, 0)),
                  const((SUBLANE, CONV_DIM)), const((1, CONV_DIM)), const((1, LANE)), const((1, LANE)),
                  const((H_B, 1)), const((1, D_SSM)), const((1, D_SSM)), const((LANE, D_SSM))],
        out_specs=[pl.BlockSpec((1, lc, D_SSM), lambda b, c: (b, c, 0)),
                   pl.BlockSpec((1, SUBLANE, CONV_DIM), lambda b, c: (b, 0, 0)),
                   pl.BlockSpec((1, N_B, D_SSM), lambda b, c: (b, 0, 0))],
        out_shape=[jax.ShapeDtypeStruct((bsz, t, D_SSM), bf16),
                   jax.ShapeDtypeStruct((bsz, SUBLANE, CONV_DIM), f32),
                   jax.ShapeDtypeStruct((bsz, N_B, D_SSM), f32)],
        scratch_shapes=[pltpu.VMEM((SUBLANE + lc, CONV_DIM), f32), pltpu.VMEM((N_B, D_SSM), f32)],
        compiler_params=_cparams(("parallel", "arbitrary")),
        name="ssd_mixer",
    )(zx, dt_raw, hist, h0_t, p["conv_w"], p["conv_b"], p["dt_bias"], p["a_log_row"], p["a_log_col"],
      p["d_skip"], p["norm_g"], p["expand"])


def _mlstm_kernel(qkv_ref, o_ref, gates_ref, gb_ref, ng_ref, c0_ref, n0_ref, m0_ref,
                  y_ref, cout_ref, nout_ref, mout_ref, ct_ref, n_ref, m_ref, *, lc, t_valid):
    c = pl.program_id(1)

    @pl.when(c == 0)
    def _():
        ct_ref[...] = c0_ref[0]
        n_ref[...] = n0_ref[0]
        m_ref[...] = m0_ref[0]

    lane = _iota2((lc, LANE), 1)
    g = gates_ref[0] + gb_ref[...]
    is_f = (lane >= H_C) & (lane < 2 * H_C)
    lf = jnp.where(is_f, _log_sigmoid(g), 0.0)
    li = g
    if t_valid < lc:
        valid = _iota2((lc, LANE), 0) < t_valid
        lf = jnp.where(valid, lf, 0.0)
        li = jnp.where(valid, li, NEG)
    b_cum = _dot_exact_rhs(_tri_lower(lc), lf)
    rows = 2 * H_C
    li_t = _dot_exact_rhs(_eye(rows, LANE), li, _NT)
    lf_t = _dot_exact_rhs(_eye(rows, LANE), lf, _NT)
    b_cum_t = _dot_exact_lhs(lf_t, _tri_upper(lc))

    causal = _iota2((lc, lc), 0) >= _iota2((lc, lc), 1)
    eye_k = _eye(DK_C, DK_C)
    m_all = m_ref[...]
    m_out = m_all
    for h in range(H_C):
        qh = qkv_ref[0, :, h * DK_C:(h + 1) * DK_C]
        kh = qkv_ref[0, :, D_QK_C + h * DK_C:D_QK_C + (h + 1) * DK_C]
        vh = qkv_ref[0, :, 2 * D_QK_C + h * DV_C:2 * D_QK_C + (h + 1) * DV_C]
        bcol = b_cum[:, H_C + h:H_C + h + 1]
        brow = b_cum_t[H_C + h:H_C + h + 1, :]
        lirow = li_t[h:h + 1, :]
        m_st = m_all[:, h:h + 1]
        d_mat = jnp.where(causal, bcol - brow + lirow, NEG)
        inter = bcol + m_st
        m_row = jnp.maximum(inter, jnp.max(d_mat, axis=-1, keepdims=True))
        w = jnp.exp(d_mat - m_row) * _dot(qh, kh, _NT)
        g_inter = jnp.exp(inter - m_row)
        ct_h = ct_ref[h]
        n_h = n_ref[h:h + 1, :]
        num = _dot(w.astype(bf16), vh) + g_inter * _dot(qh, ct_h.astype(bf16))
        den = jnp.sum(w, axis=-1, keepdims=True) + g_inter * jnp.sum(qh.astype(f32) * n_h, axis=-1, keepdims=True)
        hh = num / jnp.maximum(jnp.abs(den), jnp.exp(-m_row))
        hh = hh * lax.rsqrt(jnp.mean(hh * hh, axis=-1, keepdims=True) + LN_EPS) * ng_ref[:, h * DV_C:(h + 1) * DV_C]
        og = o_ref[0, :, h * DV_C:(h + 1) * DV_C]
        y_ref[0, :, h * DV_C:(h + 1) * DV_C] = (hh * _sigmoid(og)).astype(y_ref.dtype)
        b_tot = bcol[lc - 1:lc, :]
        d_end = b_tot - brow + lirow
        m_new = jnp.maximum(b_tot + m_st, jnp.max(d_end, axis=-1, keepdims=True))
        w_end = jnp.exp(d_end - m_new)
        g_old = jnp.exp(b_tot + m_st - m_new)
        kh_t = _dot(eye_k, kh, _NT)
        ct_ref[h] = g_old * ct_h + _dot((kh_t * w_end).astype(bf16), vh)
        n_upd = _dot(jnp.broadcast_to(w_end, (SUBLANE, lc)).astype(bf16), kh)[0:1, :]
        n_ref[h:h + 1, :] = g_old * n_h + n_upd
        m_out = jnp.where(lane[0:1, :] == h, m_new, m_out)
    m_ref[...] = m_out

    @pl.when(c == pl.num_programs(1) - 1)
    def _():
        cout_ref[0] = ct_ref[...]
        nout_ref[0] = n_ref[...]
        mout_ref[0] = m_ref[...]


def _mlstm_mixer(qkv, o, gates, c0_t, n0, m0, p, lc, t_valid):
    bsz, t, wq = qkv.shape
    const = lambda shape: pl.BlockSpec(shape, lambda b, c: (0,) * len(shape))
    return pl.pallas_call(
        functools.partial(_mlstm_kernel, lc=lc, t_valid=t_valid),
        grid=(bsz, t // lc),
        in_specs=[pl.BlockSpec((1, lc, wq), lambda b, c: (b, c, 0)),
                  pl.BlockSpec((1, lc, D_V_C), lambda b, c: (b, c, 0)),
                  pl.BlockSpec((1, lc, LANE), lambda b, c: (b, c, 0)),
                  const((1, LANE)), const((1, D_V_C)),
                  pl.BlockSpec((1, H_C, DK_C, DV_C), lambda b, c: (b, 0, 0, 0)),
                  pl.BlockSpec((1, H_C, DK_C), lambda b, c: (b, 0, 0)),
                  pl.BlockSpec((1, 1, LANE), lambda b, c: (b, 0, 0))],
        out_specs=[pl.BlockSpec((1, lc, D_V_C), lambda b, c: (b, c, 0)),
                   pl.BlockSpec((1, H_C, DK_C, DV_C), lambda b, c: (b, 0, 0, 0)),
                   pl.BlockSpec((1, H_C, DK_C), lambda b, c: (b, 0, 0)),
                   pl.BlockSpec((1, 1, LANE), lambda b, c: (b, 0, 0))],
        out_shape=[jax.ShapeDtypeStruct((bsz, t, D_V_C), bf16),
                   jax.ShapeDtypeStruct((bsz, H_C, DK_C, DV_C), f32),
                   jax.ShapeDtypeStruct((bsz, H_C, DK_C), f32),
                   jax.ShapeDtypeStruct((bsz, 1, LANE), f32)],
        scratch_shapes=[pltpu.VMEM((H_C, DK_C, DV_C), f32), pltpu.VMEM((H_C, DK_C), f32),
                        pltpu.VMEM((1, LANE), f32)],
        compiler_params=_cparams(("parallel", "arbitrary")),
        name="mlstm_mixer",
    )(qkv, o, gates, p["gate_bias"], p["norm_g"], c0_t, n0, m0)


def _pad_lanes(a, width=LANE):
    return jnp.pad(a, [(0, 0)] * (a.ndim - 1) + [(0, width - a.shape[-1])])


def _pad_seq(a, t_pad=SEQ_PAD):
    return jnp.pad(a, [(0, 0), (0, t_pad - a.shape[1])] + [(0, 0)] * (a.ndim - 2))


def _even_layer(xm_b, xm_f, xa_b, xa_f, n_p, t_main, cache, w):
    k_past, v_past, logf_past, conv_past, h_past = cache
    n_s = k_past.shape[0]
    n_aux = n_p + n_s
    t_aux = xa_f.shape[0] // n_aux
    past = k_past.shape[1]

    def project(xb):
        q, = _matmul(xb, w["w_q"], [bf16])
        kv_f, kv_b = _matmul(xb, w["w_kv"], [f32, bf16])
        zx, = _matmul(xb, w["w_zx"], [f32])
        sm, = _matmul(xb, w["w_small"], [f32], tn_pref=2 * LANE)
        return q, kv_f, kv_b, zx, sm

    qm, kvm_f, kvm_b, zxm, smm = project(xm_b)
    qa, kva_f, kva_b, zxa, sma = project(xa_b)

    seq = lambda a, n: a.reshape(n, a.shape[0] // n, a.shape[1])
    qm, kvm_f, kvm_b, zxm, smm = [seq(a, n_p) for a in (qm, kvm_f, kvm_b, zxm, smm)]
    qa, kva_f, kva_b, zxa, sma = [_pad_seq(seq(a, n_aux)) for a in (qa, kva_f, kva_b, zxa, sma)]

    zero_c = jnp.zeros((n_s, 1, LANE), f32)
    _, cum_past = _logf_cumsum(_pad_lanes(logf_past.astype(f32)), w["b_f"], zero_c, gate=False)
    carry_a = jnp.concatenate([jnp.zeros((n_p, 1, LANE), f32), cum_past[:, past - 1:past]], axis=0)
    logf_a, cum_a = _logf_cumsum(sma[:, :, LANE:], w["b_f"], carry_a, gate=True)
    logf_m, cum_m = _logf_cumsum(smm[:, :, LANE:], w["b_f"], cum_a[:n_p, t_aux - 1:t_aux], gate=True)
    rows = lambda cum: jnp.swapaxes(cum[:, :, :H_A], 1, 2)[:, :, None, :]

    ka_b, va_b = kva_b[:, :, :D_A], kva_b[:, :, D_A:]
    fa = rows(cum_a)
    att_meta = _fox_attention(qa[:n_p], ka_b[:n_p], va_b[:n_p], fa[:n_p])
    att_s = _fox_attention(qa[n_p:], ka_b[n_p:], va_b[n_p:], fa[n_p:],
                           prefix=(k_past.reshape(n_s, past, D_A), v_past.reshape(n_s, past, D_A), rows(cum_past)),
                           s1_valid=past)
    att_m = _fox_attention(qm, kvm_b[:, :, :D_A], kvm_b[:, :, D_A:], rows(cum_m),
                           prefix=(ka_b[:n_p], va_b[:n_p], fa[:n_p]), s1_valid=t_aux)
    att_a = jnp.concatenate([att_meta, att_s], axis=0)

    hist_a = jnp.concatenate([jnp.zeros((n_p, SUBLANE, CONV_DIM), f32),
                              jnp.pad(conv_past.astype(f32), ((0, 0), (SUBLANE - (CONV_W - 1), 0), (0, 0)))], axis=0)
    to_t = lambda h: jnp.transpose(h.reshape(-1, D_SSM, N_B), (0, 2, 1))
    from_t = lambda h: jnp.transpose(h, (0, 2, 1)).reshape(-1, H_B, P_B, N_B)
    h0_a = jnp.concatenate([jnp.zeros((n_p, N_B, D_SSM), f32), to_t(h_past.astype(f32))], axis=0)
    ya, tail_a, h_a = _ssd_mixer(zxa, sma[:, :, :LANE], hist_a, h0_a, w, SEQ_PAD, t_aux)
    ym, tail_m, h_m = _ssd_mixer(zxm, smm[:, :, :LANE], tail_a[:n_p], h_a[:n_p], w, min(t_main, 128), min(t_main, 128))

    unseq = lambda a: a.reshape(-1, a.shape[-1])
    x1m_f, x1m_b = _outproj_ln([unseq(att_m), unseq(ym)], w["w_out"], xm_f, w["ln_g"], w["ln_b"])
    x1a_f, x1a_b = _outproj_ln([unseq(att_a[:, :t_aux]), unseq(ya[:, :t_aux])], w["w_out"], xa_f, w["ln_g"], w["ln_b"])

    heads = lambda a: a.reshape(a.shape[0], a.shape[1], H_A, DH_A)
    ka_f, va_f = kva_f[:, :t_aux, :D_A], kva_f[:, :t_aux, D_A:]
    new = dict(
        k_p=heads(jnp.concatenate([ka_f[:n_p], kvm_f[:, :, :D_A]], axis=1)),
        v_p=heads(jnp.concatenate([va_f[:n_p], kvm_f[:, :, D_A:]], axis=1)),
        f_p=jnp.concatenate([logf_a[:n_p, :t_aux, :H_A], logf_m[:, :, :H_A]], axis=1),
        conv_p=tail_m[:, SUBLANE - (CONV_W - 1):], h_p=from_t(h_m),
        k_s=heads(ka_f[n_p:]), v_s=heads(va_f[n_p:]), f_s=logf_a[n_p:, :t_aux, :H_A],
        conv_s=tail_a[n_p:, SUBLANE - (CONV_W - 1):], h_s=from_t(h_a[n_p:]))
    return (x1m_f, x1m_b, x1a_f, x1a_b), new


def _odd_layer(xm_b, xm_f, xa_b, xa_f, n_p, t_main, cache, w):
    c_past, n_past, m_past = cache
    n_s = c_past.shape[0]
    n_aux = n_p + n_s
    t_aux = xa_f.shape[0] // n_aux

    def project(xb):
        qkv, = _matmul(xb, w["w_qkv"], [bf16], col_scale=w["qkv_scale"])
        o, = _matmul(xb, w["w_o"], [f32])
        gt, = _matmul(xb, w["w_gates"], [f32])
        return qkv, o, gt

    qkvm, om, gm = project(xm_b)
    qkva, oa, ga = project(xa_b)
    seq = lambda a, n: a.reshape(n, a.shape[0] // n, a.shape[1])
    qkvm, om, gm = [seq(a, n_p) for a in (qkvm, om, gm)]
    qkva, oa, ga = [_pad_seq(seq(a, n_aux)) for a in (qkva, oa, ga)]

    c0 = jnp.concatenate([jnp.zeros((n_p, H_C, DK_C, DV_C), f32), jnp.swapaxes(c_past.astype(f32), 2, 3)], axis=0)
    n0 = jnp.concatenate([jnp.zeros((n_p, H_C, DK_C), f32), n_past.astype(f32)], axis=0)
    m0 = jnp.concatenate([jnp.zeros((n_p, 1, LANE), f32), _pad_lanes(m_past.astype(f32))[:, None, :]], axis=0)
    ha, c_a, n_a, m_a = _mlstm_mixer(qkva, oa, ga, c0, n0, m0, w, SEQ_PAD, t_aux)
    lc = min(t_main, 256)
    hm, c_m, n_m, m_m = _mlstm_mixer(qkvm, om, gm, c_a[:n_p], n_a[:n_p], m_a[:n_p], w, lc, lc)

    unseq = lambda a: a.reshape(-1, a.shape[-1])
    x1m_f, x1m_b = _outproj_ln([unseq(hm)], w["w_out"], xm_f, w["ln_g"], w["ln_b"])
    x1a_f, x1a_b = _outproj_ln([unseq(ha[:, :t_aux])], w["w_out"], xa_f, w["ln_g"], w["ln_b"])
    new = dict(c_p=jnp.swapaxes(c_m, 2, 3), n_p=n_m, m_p=m_m[:, 0, :H_C],
               c_s=jnp.swapaxes(c_a[n_p:], 2, 3), n_s=n_a[n_p:], m_s=m_a[n_p:, 0, :H_C])
    return (x1m_f, x1m_b, x1a_f, x1a_b), new


def _even_weights(e, w_in_a, b_fgate_a, conv_w, conv_b, dt_bias, a_log, d_skip, ssd_norm_g, w_out_a):
    wi = w_in_a[e]
    o_f = 3 * D_A
    o_z = o_f + H_A
    o_dt = o_z + D_SSM + CONV_DIM
    w_small = jnp.concatenate([_pad_lanes(wi[:, o_dt:o_dt + H_B]), _pad_lanes(wi[:, o_f:o_f + H_A])], axis=1)
    expand = (np.arange(LANE)[:, None] == (np.arange(D_SSM) // P_B)[None, :])
    return dict(
        w_q=wi[:, :D_A].astype(bf16), w_kv=wi[:, D_A:3 * D_A].astype(bf16),
        w_zx=wi[:, o_z:o_dt].astype(bf16), w_small=w_small.astype(bf16),
        b_f=_pad_lanes(b_fgate_a[e][None, :]),
        conv_w=jnp.pad(conv_w[e], ((0, SUBLANE - CONV_W), (0, 0))), conv_b=conv_b[e][None, :],
        dt_bias=_pad_lanes(dt_bias[e][None, :]), a_log_row=_pad_lanes(a_log[e][None, :]),
        a_log_col=a_log[e][:, None], d_skip=jnp.repeat(d_skip[e], P_B)[None, :],
        norm_g=ssd_norm_g[e][None, :], expand=jnp.asarray(expand, bf16),
        w_out=w_out_a[e].astype(bf16))


def _odd_weights(o, w_in_c, b_igate_c, b_fgate_c, mlstm_norm_g, w_out_c):
    wi = w_in_c[o]
    o_o = 2 * D_QK_C + D_V_C
    o_g = o_o + D_V_C
    scale = jnp.concatenate([jnp.ones((D_QK_C,), f32), jnp.full((D_QK_C,), DK_C ** -0.5, f32),
                             jnp.ones((D_V_C,), f32)])[None, :]
    return dict(
        w_qkv=wi[:, :o_o].astype(bf16), qkv_scale=scale, w_o=wi[:, o_o:o_g].astype(bf16),
        w_gates=_pad_lanes(wi[:, o_g:]).astype(bf16),
        gate_bias=_pad_lanes(jnp.concatenate([b_igate_c[o], b_fgate_c[o]])[None, :]),
        norm_g=mlstm_norm_g[o][None, :], w_out=w_out_c[o].astype(bf16))


def kernel(x_prompt, x_sample, cache_fox_k, cache_fox_v, cache_fox_logf, state_ssd_conv, state_ssd, state_mlstm_c, state_mlstm_n, state_mlstm_m, meta_tokens, w_in_a, b_fgate_a, conv_w, conv_b, dt_bias, a_log, d_skip, ssd_norm_g, w_out_a, w_in_c, b_igate_c, b_fgate_c, mlstm_norm_g, w_out_c, ln_mix_g, ln_mix_b, ln_ffn_g, ln_ffn_b, w_ffn_gate, w_ffn_up, w_ffn_down):
    n_p, t_main, d = x_prompt.shape
    n_s, t_s, _ = x_sample.shape
    assert t_s == N_META, "aux rows hold equal-length meta and running-stream sequences"
    xm_f = x_prompt.reshape(n_p * t_main, d)
    xa_f = jnp.concatenate([jnp.broadcast_to(meta_tokens.astype(x_prompt.dtype)[None], (n_p, N_META, d)),
                            x_sample], axis=0).reshape((n_p + n_s) * t_s, d)
    xm_b, xa_b = xm_f.astype(bf16), xa_f.astype(bf16)

    even_out, odd_out = [], []
    for layer in range(DEPTH):
        if layer % 2 == 0:
            e = layer // 2
            w = _even_weights(e, w_in_a, b_fgate_a, conv_w, conv_b, dt_bias, a_log, d_skip, ssd_norm_g, w_out_a)
            cache = (cache_fox_k[e], cache_fox_v[e], cache_fox_logf[e], state_ssd_conv[e], state_ssd[e])
            w["ln_g"], w["ln_b"] = ln_mix_g[layer], ln_mix_b[layer]
            (x1m_f, x1m_b, x1a_f, x1a_b), new = _even_layer(xm_b, xm_f, xa_b, xa_f, n_p, t_main, cache, w)
            even_out.append(new)
        else:
            o = layer // 2
            w = _odd_weights(o, w_in_c, b_igate_c, b_fgate_c, mlstm_norm_g, w_out_c)
            cache = (state_mlstm_c[o], state_mlstm_n[o], state_mlstm_m[o])
            w["ln_g"], w["ln_b"] = ln_mix_g[layer], ln_mix_b[layer]
            (x1m_f, x1m_b, x1a_f, x1a_b), new = _odd_layer(xm_b, xm_f, xa_b, xa_f, n_p, t_main, cache, w)
            odd_out.append(new)
        wg, wu, wd = w_ffn_gate[layer].astype(bf16), w_ffn_up[layer].astype(bf16), w_ffn_down[layer].astype(bf16)
        xm_f, xm_b = _ffn_ln(x1m_b, x1m_f, wg, wu, wd, ln_ffn_g[layer], ln_ffn_b[layer])
        xa_f, xa_b = _ffn_ln(x1a_b, x1a_f, wg, wu, wd, ln_ffn_g[layer], ln_ffn_b[layer])

    y_prompt = xm_f.reshape(n_p, t_main, d)
    y_sample = xa_f.reshape(n_p + n_s, t_s, d)[n_p:]
    st = lambda outs, key: jnp.stack([o[key] for o in outs])
    return (y_prompt, y_sample,
            st(even_out, "k_p"), st(even_out, "v_p"), st(even_out, "f_p"), st(even_out, "conv_p"), st(even_out, "h_p"),
            st(odd_out, "c_p"), st(odd_out, "n_p"), st(odd_out, "m_p"),
            st(even_out, "k_s"), st(even_out, "v_s"), st(even_out, "f_s"), st(even_out, "conv_s"), st(even_out, "h_s"),
            st(odd_out, "c_s"), st(odd_out, "n_s"), st(odd_out, "m_s"))
```

```python
import functools

import jax
import jax.numpy as jnp
import numpy as np
from jax import lax
from jax.experimental import pallas as pl
from jax.experimental.pallas import tpu as pltpu

f32 = jnp.float32
bf16 = jnp.bfloat16

D_MODEL = 2048
DEPTH = 2
N_META = 16
H_A, DH_A = 16, 128
D_A = H_A * DH_A
H_B, P_B, G_B, N_B = 32, 64, 4, 128
HG_B = H_B // G_B
D_SSM = H_B * P_B
CONV_W = 4
CONV_DIM = D_SSM + 2 * G_B * N_B
H_C, DK_C, DV_C = 8, 128, 256
D_QK_C = H_C * DK_C
D_V_C = H_C * DV_C
ALPHA = (2 * DEPTH) ** 0.25
NEG = -1e30
LN_EPS = 1e-5
LOG2E = 1.4426950408889634

LANE = 128
SUBLANE = 8
SEQ_PAD = 128
VMEM_LIMIT = 56 * 1024 * 1024


def _cparams(sem):
    return pltpu.CompilerParams(dimension_semantics=sem, vmem_limit_bytes=VMEM_LIMIT)


def _tile(n, pref, mult=16):
    if n <= pref:
        return n
    t = (pref // mult) * mult
    while t >= mult:
        if n % t == 0:
            return t
        t -= mult
    return n


def _softplus(x):
    return jnp.maximum(x, 0.0) + jnp.log1p(jnp.exp(-jnp.abs(x)))


def _log_sigmoid(x):
    return -_softplus(-x)


def _sigmoid(x):
    return 1.0 / (1.0 + jnp.exp(-x))


def _split3(a):
    a1 = a.astype(bf16)
    r1 = a - a1.astype(f32)
    a2 = r1.astype(bf16)
    a3 = (r1 - a2.astype(f32)).astype(bf16)
    return a1, a2, a3


_NN = (((1,), (0,)), ((), ()))
_NT = (((1,), (1,)), ((), ()))


def _dot(a, b, dims=_NN):
    return lax.dot_general(a, b, dims, preferred_element_type=f32)


def _dot_exact_rhs(a_bf16_exact, b_f32, dims=_NN):
    b1, b2, b3 = _split3(b_f32)
    return _dot(a_bf16_exact, b1, dims) + _dot(a_bf16_exact, b2, dims) + _dot(a_bf16_exact, b3, dims)


def _dot_exact_lhs(a_f32, b_bf16_exact, dims=_NN):
    a1, a2, a3 = _split3(a_f32)
    return _dot(a1, b_bf16_exact, dims) + _dot(a2, b_bf16_exact, dims) + _dot(a3, b_bf16_exact, dims)


def _iota2(shape, axis):
    return lax.broadcasted_iota(jnp.int32, shape, axis)


def _tri_lower(n):
    return (_iota2((n, n), 0) >= _iota2((n, n), 1)).astype(bf16)


def _tri_upper(n):
    return (_iota2((n, n), 0) <= _iota2((n, n), 1)).astype(bf16)


def _eye(rows, cols):
    return (_iota2((rows, cols), 0) == _iota2((rows, cols), 1)).astype(bf16)


def _layer_norm(x, g, b):
    mu = jnp.mean(x, axis=-1, keepdims=True)
    xc = x - mu
    var = jnp.mean(xc * xc, axis=-1, keepdims=True)
    return xc * lax.rsqrt(var + LN_EPS) * g + b


def _mm_kernel(x_ref, w_ref, s_ref, *out_refs):
    acc = _dot(x_ref[...], w_ref[...]) * s_ref[...]
    for o in out_refs:
        o[...] = acc.astype(o.dtype).reshape(o.shape)


def _matmul(x, w, out_dtypes, col_scale=None, tm_pref=1024, tn_pref=512, seq_out=None):
    m, k = x.shape
    n = w.shape[1]
    tm = _tile(m if seq_out is None else m // seq_out[0], tm_pref)
    tn = _tile(n, tn_pref, LANE)
    if col_scale is None:
        col_scale = jnp.ones((1, n), f32)
    out_specs = [pl.BlockSpec((tm, tn), lambda i, j: (i, j)) for _ in out_dtypes]
    out_shape = [jax.ShapeDtypeStruct((m, n), dt) for dt in out_dtypes]
    if seq_out is not None:
        n_seq, t_out, row_off = seq_out
        per_seq = m // n_seq // tm
        out_specs[0] = pl.BlockSpec((pl.Element(1), pl.Element(tm), pl.Element(tn)),
                                    lambda i, j: (i // per_seq,
                                                  pl.multiple_of(row_off + (i % per_seq) * tm, SUBLANE),
                                                  pl.multiple_of(j * tn, LANE)))
        out_shape[0] = jax.ShapeDtypeStruct((n_seq, t_out, n), out_dtypes[0])
    return pl.pallas_call(
        _mm_kernel,
        grid=(m // tm, n // tn),
        in_specs=[pl.BlockSpec((tm, k), lambda i, j: (i, 0)),
                  pl.BlockSpec((k, tn), lambda i, j: (0, j)),
                  pl.BlockSpec((1, tn), lambda i, j: (0, j))],
        out_specs=out_specs,
        out_shape=out_shape,
        compiler_params=_cparams(("parallel", "arbitrary")),
        name="proj_matmul",
    )(x, w, col_scale)


def _outproj_ln_kernel(*refs, n_act, kc):
    acts = refs[:n_act]
    w_ref, res_ref, g_ref, b_ref, of_ref, ob_ref, acc_ref = refs[n_act:]
    k = pl.program_id(1)

    @pl.when(k == 0)
    def _():
        acc_ref[...] = jnp.zeros_like(acc_ref)

    for a in range(n_act):
        @pl.when((k >= a * kc) & (k < (a + 1) * kc))
        def _(a=a):
            acc_ref[...] += _dot(acts[a][...], w_ref[...])

    @pl.when(k == pl.num_programs(1) - 1)
    def _():
        y = _layer_norm(ALPHA * res_ref[...] + acc_ref[...], g_ref[...], b_ref[...])
        of_ref[...] = y
        ob_ref[...] = y.astype(bf16)


def _outproj_ln(acts, w, res, g, b, tm_pref=512, tk=1024):
    m, d = res.shape
    tm = _tile(m, tm_pref)
    kc = acts[0].shape[1] // tk
    n_act = len(acts)
    in_specs = [pl.BlockSpec((tm, tk), (lambda i, k, a=a: (i, jnp.clip(k - a * kc, 0, kc - 1))))
                for a in range(n_act)]
    in_specs += [pl.BlockSpec((tk, d), lambda i, k: (k, 0)),
                 pl.BlockSpec((tm, d), lambda i, k: (i, 0)),
                 pl.BlockSpec((1, d), lambda i, k: (0, 0)),
                 pl.BlockSpec((1, d), lambda i, k: (0, 0))]
    return pl.pallas_call(
        functools.partial(_outproj_ln_kernel, n_act=n_act, kc=kc),
        grid=(m // tm, n_act * kc),
        in_specs=in_specs,
        out_specs=[pl.BlockSpec((tm, d), lambda i, k: (i, 0)),
                   pl.BlockSpec((tm, d), lambda i, k: (i, 0))],
        out_shape=[jax.ShapeDtypeStruct((m, d), f32), jax.ShapeDtypeStruct((m, d), bf16)],
        scratch_shapes=[pltpu.VMEM((tm, d), f32)],
        compiler_params=_cparams(("parallel", "arbitrary")),
        name="outproj_ln",
    )(*acts, w, res, g.reshape(1, d), b.reshape(1, d))


def _ffn_ln_kernel(xb_ref, xf_ref, wg_ref, wu_ref, wd_ref, g_ref, b_ref, of_ref, ob_ref, acc_ref):
    f = pl.program_id(1)

    @pl.when(f == 0)
    def _():
        acc_ref[...] = jnp.zeros_like(acc_ref)

    x = xb_ref[...]
    gate = _dot(x, wg_ref[...])
    up = _dot(x, wu_ref[...])
    h = (gate * _sigmoid(gate) * up).astype(bf16)
    acc_ref[...] += _dot(h, wd_ref[...])

    @pl.when(f == pl.num_programs(1) - 1)
    def _():
        y = _layer_norm(ALPHA * xf_ref[...] + acc_ref[...], g_ref[...], b_ref[...])
        of_ref[...] = y
        ob_ref[...] = y.astype(bf16)


def _ffn_ln(xb, xf, wg, wu, wd, g, b, tm_pref=512, tf_pref=512):
    m, d = xf.shape
    dff = wg.shape[1]
    tm = _tile(m, tm_pref)
    tf = _tile(dff, tf_pref, LANE)
    return pl.pallas_call(
        _ffn_ln_kernel,
        grid=(m // tm, dff // tf),
        in_specs=[pl.BlockSpec((tm, d), lambda i, f: (i, 0)),
                  pl.BlockSpec((tm, d), lambda i, f: (i, 0)),
                  pl.BlockSpec((d, tf), lambda i, f: (0, f)),
                  pl.BlockSpec((d, tf), lambda i, f: (0, f)),
                  pl.BlockSpec((tf, d), lambda i, f: (f, 0)),
                  pl.BlockSpec((1, d), lambda i, f: (0, 0)),
                  pl.BlockSpec((1, d), lambda i, f: (0, 0))],
        out_specs=[pl.BlockSpec((tm, d), lambda i, f: (i, 0)),
                   pl.BlockSpec((tm, d), lambda i, f: (i, 0))],
        out_shape=[jax.ShapeDtypeStruct((m, d), f32), jax.ShapeDtypeStruct((m, d), bf16)],
        scratch_shapes=[pltpu.VMEM((tm, d), f32)],
        compiler_params=_cparams(("parallel", "arbitrary")),
        name="ffn_ln",
    )(xb, xf, wg, wu, wd, g.reshape(1, d), b.reshape(1, d))


def _cumsum_kernel(x_ref, bias_ref, carry_ref, logf_ref, cum_ref, acc_ref, *, gate):
    c = pl.program_id(1)

    @pl.when(c == 0)
    def _():
        acc_ref[...] = carry_ref[0]

    x = x_ref[0]
    lf = _log_sigmoid(x + bias_ref[...]) if gate else x
    tc = x.shape[0]
    cum = _dot_exact_rhs(_tri_lower(tc), lf) + acc_ref[...]
    logf_ref[0] = lf
    cum_ref[0] = cum
    acc_ref[...] = cum[tc - 1:tc, :]


def _logf_cumsum(x, bias, carry, gate, tc_pref=512):
    bsz, t, w = x.shape
    tc = _tile(t, tc_pref)
    spec = pl.BlockSpec((1, tc, w), lambda b, c: (b, c, 0))
    return pl.pallas_call(
        functools.partial(_cumsum_kernel, gate=gate),
        grid=(bsz, t // tc),
        in_specs=[spec,
                  pl.BlockSpec((1, w), lambda b, c: (0, 0)),
                  pl.BlockSpec((1, 1, w), lambda b, c: (b, 0, 0))],
        out_specs=[spec, spec],
        out_shape=[jax.ShapeDtypeStruct((bsz, t, w), f32)] * 2,
        scratch_shapes=[pltpu.VMEM((1, w), f32)],
        compiler_params=_cparams(("parallel", "arbitrary")),
        name="logf_cumsum",
    )(x, bias, carry)


def _attn_kernel(*refs, tq, tk, tk2, s1, s1_valid, hps):
    if s1:
        q_ref, k1_ref, v1_ref, f1_ref, k2_ref, v2_ref, f2_ref, o_ref = refs
    else:
        q_ref, k2_ref, v2_ref, f2_ref, o_ref = refs
    qi = pl.program_id(2)
    d = DH_A

    for hh in range(hps):
        cols = slice(hh * d, (hh + 1) * d)
        q = q_ref[0, :, cols]

        def scores(k, q=q):
            return _dot(q, k.astype(bf16), _NT)

        def update(carry, s, v, fk, mask):
            m, l, acc = carry
            s = s - fk * LOG2E
            if mask is not None:
                s = jnp.where(mask, s, NEG)
            m_new = jnp.maximum(m, jnp.max(s, axis=-1, keepdims=True))
            alpha = jnp.exp2(m - m_new)
            p = jnp.exp2(s - m_new)
            l = alpha * l + jnp.sum(p, axis=-1, keepdims=True)
            acc = alpha * acc + _dot(p.astype(bf16), v.astype(bf16))
            return m_new, l, acc

        def step(j, c, mask, row0=0, cols=cols, hh=hh, q=q, update=update):
            off = pl.multiple_of(j * tk, tk)
            s = _dot(q[row0:], k2_ref[0, pl.ds(off, tk), cols].astype(bf16), _NT)
            new = update(tuple(a[row0:] for a in c), s, v2_ref[0, pl.ds(off, tk), cols],
                         f2_ref[0, hh, :, pl.ds(off, tk)], mask)
            if row0 == 0:
                return new
            return tuple(jnp.concatenate([a[:row0], b], axis=0) for a, b in zip(c, new))

        carry = (jnp.full((tq, 1), NEG, f32), jnp.zeros((tq, 1), f32), jnp.zeros((tq, d), f32))
        if s1:
            pmask = None if s1_valid == s1 else (_iota2((tq, s1), 1) < s1_valid)
            carry = update(carry, scores(k1_ref[0, :, cols]), v1_ref[0, :, cols], f1_ref[0, hh], pmask)

        nd = tk2 // tk
        carry = lax.fori_loop(0, qi * nd, lambda j, c, step=step: step(j, c, None), carry)
        for r in range(nd):
            row0 = r * tk if tq == tk2 else 0
            causal = _iota2((tq - row0, tk), 0) + row0 >= _iota2((tq - row0, tk), 1) + r * tk
            carry = step(qi * nd + r, carry, causal, row0)
        m, l, acc = carry
        o_ref[0, :, cols] = (acc / l).astype(o_ref.dtype)


def _fox_attention(q, k2, v2, f2, prefix=None, s1_valid=0, tq=512, tk=512, hps=1):
    bsz, t, hd = q.shape
    t2 = k2.shape[1]
    nq = t // tq
    tk2 = t2 // nq
    w = hps * DH_A
    qspec = pl.BlockSpec((1, tq, w), lambda b, h, i: (b, i, h))
    kvspec = pl.BlockSpec((1, t2, w), lambda b, h, i: (b, 0, h))
    fspec = pl.BlockSpec((1, hps, 1, t2), lambda b, h, i: (b, h, 0, 0))
    args, in_specs = [q], [qspec]
    s1 = 0
    if prefix is not None:
        k1, v1, f1 = prefix
        s1 = k1.shape[1]
        p_spec = pl.BlockSpec((1, s1, w), lambda b, h, i: (b, 0, h))
        args += [k1, v1, f1]
        in_specs += [p_spec, p_spec, pl.BlockSpec((1, hps, 1, s1), lambda b, h, i: (b, h, 0, 0))]
    args += [k2, v2, f2]
    in_specs += [kvspec, kvspec, fspec]
    return pl.pallas_call(
        functools.partial(_attn_kernel, tq=tq, tk=min(tk, tk2), tk2=tk2, s1=s1, s1_valid=s1_valid, hps=hps),
        grid=(bsz, hd // w, nq),
        in_specs=in_specs,
        out_specs=qspec,
        out_shape=jax.ShapeDtypeStruct((bsz, t, hd), bf16),
        compiler_params=_cparams(("parallel", "parallel", "arbitrary")),
        name="fox_attention",
    )(*args)


def _ssd_kernel(zx_ref, dt_ref, hist_ref, h0_ref, cw_ref, cb_ref, dtb_ref, alog_ref, alogc_ref,
                dskip_ref, ng_ref, e_ref, y_ref, tail_ref, hout_ref, ext_ref, st_ref, *, lc, t_valid):
    c = pl.program_id(1)

    @pl.when(c == 0)
    def _():
        ext_ref[0:SUBLANE, :] = hist_ref[0]
        st_ref[...] = h0_ref[0]

    ext_ref[SUBLANE:SUBLANE + lc, :] = zx_ref[0, :, D_SSM:]
    conv = cb_ref[...]
    for i in range(CONV_W):
        lo = SUBLANE - (CONV_W - 1) + i
        conv = conv + cw_ref[i:i + 1, :] * ext_ref[lo:lo + lc, :]
    xc = conv * _sigmoid(conv)
    tail = ext_ref[t_valid:t_valid + SUBLANE, :]
    ext_ref[0:SUBLANE, :] = tail

    dt = _softplus(dt_ref[0] + dtb_ref[...])
    if t_valid < lc:
        dt = jnp.where(_iota2((lc, LANE), 0) < t_valid, dt, 0.0)
    a_row = -jnp.exp(alog_ref[...])
    a_col = -jnp.exp(alogc_ref[...])
    a_cum = _dot_exact_rhs(_tri_lower(lc), dt * a_row)
    dt_t = _dot_exact_rhs(_eye(H_B, LANE), dt, _NT)
    a_cum_t = _dot_exact_lhs(dt_t * a_col, _tri_upper(lc))
    a_last = a_cum[lc - 1:lc, :]
    e = e_ref[...]
    to_end_x = _dot_exact_lhs(jnp.exp(a_last - a_cum) * dt, e)
    ea_x = _dot_exact_lhs(jnp.exp(a_cum), e)
    cdec_x = _dot_exact_lhs(jnp.broadcast_to(jnp.exp(a_last), (SUBLANE, LANE)), e)[0:1, :]

    causal = _iota2((lc, lc), 0) >= _iota2((lc, lc), 1)
    left = _iota2((lc, LANE), 1) < P_B
    eye_n = _eye(N_B, N_B)
    gw = D_SSM // G_B
    for g in range(G_B):
        bg = xc[:, D_SSM + g * N_B:D_SSM + (g + 1) * N_B].astype(bf16)
        cg = xc[:, D_SSM + G_B * N_B + g * N_B:D_SSM + G_B * N_B + (g + 1) * N_B].astype(bf16)
        cb = _dot(cg, bg, _NT)
        bg_t = _dot(eye_n, bg, _NT).astype(bf16)
        xg = xc[:, g * gw:(g + 1) * gw]
        st_g = st_ref[:, g * gw:(g + 1) * gw]
        y_off = _dot(cg, st_g.astype(bf16)) * ea_x[:, g * gw:(g + 1) * gw]
        upd = _dot(bg_t, (xg * to_end_x[:, g * gw:(g + 1) * gw]).astype(bf16))
        st_ref[:, g * gw:(g + 1) * gw] = cdec_x[:, g * gw:(g + 1) * gw] * st_g + upd
        pairs = []
        for j in range(gw // LANE):
            xp = xg[:, j * LANE:(j + 1) * LANE].astype(bf16)
            res = []
            for h in (g * HG_B + 2 * j, g * HG_B + 2 * j + 1):
                seg = a_cum[:, h:h + 1] - a_cum_t[h:h + 1, :]
                dec = jnp.exp(jnp.where(causal, seg, NEG))
                w = (cb * dec * dt_t[h:h + 1, :]).astype(bf16)
                res.append(_dot(w, xp))
            pairs.append(jnp.where(left, res[0], res[1]))
        y = jnp.concatenate(pairs, axis=-1) + y_off + dskip_ref[:, g * gw:(g + 1) * gw] * xg
        z = zx_ref[0, :, g * gw:(g + 1) * gw]
        y = y * (z * _sigmoid(z))
        y = y * lax.rsqrt(jnp.mean(y * y, axis=-1, keepdims=True) + LN_EPS) * ng_ref[:, g * gw:(g + 1) * gw]
        y_ref[0, :, g * gw:(g + 1) * gw] = y.astype(y_ref.dtype)

    @pl.when(c == pl.num_programs(1) - 1)
    def _():
        tail_ref[0] = tail
        hout_ref[0] = st_ref[...]


def _ssd_mixer(zx, dt_raw, hist, h0_t, p, lc, t_valid):
    bsz, t, wz = zx.shape
    const = lambda shape: pl.BlockSpec(shape, lambda b, c: (0,) * len(shape))
    return pl.pallas_call(
        functools.partial(_ssd_kernel, lc=lc, t_valid=t_valid),
        grid=(bsz, t // lc),
        in_specs=[pl.BlockSpec((1, lc, wz), lambda b, c: (b, c, 0)),
                  pl.BlockSpec((1, lc, LANE), lambda b, c: (b, c, 0)),
                  pl.BlockSpec((1, SUBLANE, CONV_DIM), lambda b, c: (b, 0, 0)),
                  pl.BlockSpec((1, N_B, D_SSM), lambda b, c: (b, 0, 0)),
                  const((SUBLANE, CONV_DIM)), const((1, CONV_DIM)), const((1, LANE)), const((1, LANE)),
                  const((H_B, 1)), const((1, D_SSM)), const((1, D_SSM)), const((LANE, D_SSM))],
        out_specs=[pl.BlockSpec((1, lc, D_SSM), lambda b, c: (b, c, 0)),
                   pl.BlockSpec((1, SUBLANE, CONV_DIM), lambda b, c: (b, 0, 0)),
                   pl.BlockSpec((1, N_B, D_SSM), lambda b, c: (b, 0, 0))],
        out_shape=[jax.ShapeDtypeStruct((bsz, t, D_SSM), bf16),
                   jax.ShapeDtypeStruct((bsz, SUBLANE, CONV_DIM), f32),
                   jax.ShapeDtypeStruct((bsz, N_B, D_SSM), f32)],
        scratch_shapes=[pltpu.VMEM((SUBLANE + lc, CONV_DIM), f32), pltpu.VMEM((N_B, D_SSM), f32)],
        compiler_params=_cparams(("parallel", "arbitrary")),
        name="ssd_mixer",
    )(zx, dt_raw, hist, h0_t, p["conv_w"], p["conv_b"], p["dt_bias"], p["a_log_row"], p["a_log_col"],
      p["d_skip"], p["norm_g"], p["expand"])


def _mlstm_kernel(qkv_ref, o_ref, gates_ref, gb_ref, ng_ref, c0_ref, n0_ref, m0_ref,
                  y_ref, cout_ref, nout_ref, mout_ref, ct_ref, n_ref, m_ref, *, lc, t_valid):
    c = pl.program_id(1)

    @pl.when(c == 0)
    def _():
        ct_ref[...] = c0_ref[0]
        n_ref[...] = n0_ref[0]
        m_ref[...] = m0_ref[0]

    lane = _iota2((lc, LANE), 1)
    g = gates_ref[0] + gb_ref[...]
    is_f = (lane >= H_C) & (lane < 2 * H_C)
    lf = jnp.where(is_f, _log_sigmoid(g), 0.0)
    li = g
    if t_valid < lc:
        valid = _iota2((lc, LANE), 0) < t_valid
        lf = jnp.where(valid, lf, 0.0)
        li = jnp.where(valid, li, NEG)
    b_cum = _dot_exact_rhs(_tri_lower(lc), lf)
    rows = 2 * H_C
    li_t = _dot_exact_rhs(_eye(rows, LANE), li, _NT)
    lf_t = _dot_exact_rhs(_eye(rows, LANE), lf, _NT)
    b_cum_t = _dot_exact_lhs(lf_t, _tri_upper(lc))

    causal = _iota2((lc, lc), 0) >= _iota2((lc, lc), 1)
    eye_k = _eye(DK_C, DK_C)
    m_all = m_ref[...]
    m_out = m_all
    for h in range(H_C):
        qh = qkv_ref[0, :, h * DK_C:(h + 1) * DK_C]
        kh = qkv_ref[0, :, D_QK_C + h * DK_C:D_QK_C + (h + 1) * DK_C]
        vh = qkv_ref[0, :, 2 * D_QK_C + h * DV_C:2 * D_QK_C + (h + 1) * DV_C]
        bcol = b_cum[:, H_C + h:H_C + h + 1]
        brow = b_cum_t[H_C + h:H_C + h + 1, :]
        lirow = li_t[h:h + 1, :]
        m_st = m_all[:, h:h + 1]
        d_mat = jnp.where(causal, bcol - brow + lirow, NEG)
        inter = bcol + m_st
        m_row = jnp.maximum(inter, jnp.max(d_mat, axis=-1, keepdims=True))
        w = jnp.exp(d_mat - m_row) * _dot(qh, kh, _NT)
        g_inter = jnp.exp(inter - m_row)
        ct_h = ct_ref[h]
        n_h = n_ref[h:h + 1, :]
        num = _dot(w.astype(bf16), vh) + g_inter * _dot(qh, ct_h.astype(bf16))
        den = jnp.sum(w, axis=-1, keepdims=True) + g_inter * jnp.sum(qh.astype(f32) * n_h, axis=-1, keepdims=True)
        hh = num / jnp.maximum(jnp.abs(den), jnp.exp(-m_row))
        hh = hh * lax.rsqrt(jnp.mean(hh * hh, axis=-1, keepdims=True) + LN_EPS) * ng_ref[:, h * DV_C:(h + 1) * DV_C]
        og = o_ref[0, :, h * DV_C:(h + 1) * DV_C]
        y_ref[0, :, h * DV_C:(h + 1) * DV_C] = (hh * _sigmoid(og)).astype(y_ref.dtype)
        b_tot = bcol[lc - 1:lc, :]
        d_end = b_tot - brow + lirow
        m_new = jnp.maximum(b_tot + m_st, jnp.max(d_end, axis=-1, keepdims=True))
        w_end = jnp.exp(d_end - m_new)
        g_old = jnp.exp(b_tot + m_st - m_new)
        kh_t = _dot(eye_k, kh, _NT)
        ct_ref[h] = g_old * ct_h + _dot((kh_t * w_end).astype(bf16), vh)
        n_upd = _dot(jnp.broadcast_to(w_end, (SUBLANE, lc)).astype(bf16), kh)[0:1, :]
        n_ref[h:h + 1, :] = g_old * n_h + n_upd
        m_out = jnp.where(lane[0:1, :] == h, m_new, m_out)
    m_ref[...] = m_out

    @pl.when(c == pl.num_programs(1) - 1)
    def _():
        cout_ref[0] = ct_ref[...]
        nout_ref[0] = n_ref[...]
        mout_ref[0] = m_ref[...]


def _mlstm_mixer(qkv, o, gates, c0_t, n0, m0, p, lc, t_valid):
    bsz, t, wq = qkv.shape
    const = lambda shape: pl.BlockSpec(shape, lambda b, c: (0,) * len(shape))
    return pl.pallas_call(
        functools.partial(_mlstm_kernel, lc=lc, t_valid=t_valid),
        grid=(bsz, t // lc),
        in_specs=[pl.BlockSpec((1, lc, wq), lambda b, c: (b, c, 0)),
                  pl.BlockSpec((1, lc, D_V_C), lambda b, c: (b, c, 0)),
                  pl.BlockSpec((1, lc, LANE), lambda b, c: (b, c, 0)),
                  const((1, LANE)), const((1, D_V_C)),
                  pl.BlockSpec((1, H_C, DK_C, DV_C), lambda b, c: (b, 0, 0, 0)),
                  pl.BlockSpec((1, H_C, DK_C), lambda b, c: (b, 0, 0)),
                  pl.BlockSpec((1, 1, LANE), lambda b, c: (b, 0, 0))],
        out_specs=[pl.BlockSpec((1, lc, D_V_C), lambda b, c: (b, c, 0)),
                   pl.BlockSpec((1, H_C, DK_C, DV_C), lambda b, c: (b, 0, 0, 0)),
                   pl.BlockSpec((1, H_C, DK_C), lambda b, c: (b, 0, 0)),
                   pl.BlockSpec((1, 1, LANE), lambda b, c: (b, 0, 0))],
        out_shape=[jax.ShapeDtypeStruct((bsz, t, D_V_C), bf16),
                   jax.ShapeDtypeStruct((bsz, H_C, DK_C, DV_C), f32),
                   jax.ShapeDtypeStruct((bsz, H_C, DK_C), f32),
                   jax.ShapeDtypeStruct((bsz, 1, LANE), f32)],
        scratch_shapes=[pltpu.VMEM((H_C, DK_C, DV_C), f32), pltpu.VMEM((H_C, DK_C), f32),
                        pltpu.VMEM((1, LANE), f32)],
        compiler_params=_cparams(("parallel", "arbitrary")),
        name="mlstm_mixer",
    )(qkv, o, gates, p["gate_bias"], p["norm_g"], c0_t, n0, m0)


def _pad_lanes(a, width=LANE):
    return jnp.pad(a, [(0, 0)] * (a.ndim - 1) + [(0, width - a.shape[-1])])


def _pad_seq(a, t_pad=SEQ_PAD):
    return jnp.pad(a, [(0, 0), (0, t_pad - a.shape[1])] + [(0, 0)] * (a.ndim - 2))


def _even_layer(xm_b, xm_f, xa_b, xa_f, n_p, t_main, cache, w):
    k_past, v_past, logf_past, conv_past, h_past = cache
    n_s = k_past.shape[0]
    n_aux = n_p + n_s
    t_aux = xa_f.shape[0] // n_aux
    past = k_past.shape[1]

    def project(xb, seq_out):
        q, = _matmul(xb, w["w_q"], [bf16], col_scale=jnp.full((1, D_A), DH_A ** -0.5 * LOG2E, f32))
        k_f, k_b = _matmul(xb, w["w_k"], [f32, bf16], seq_out=seq_out)
        v_f, v_b = _matmul(xb, w["w_v"], [f32, bf16], seq_out=seq_out)
        zx, = _matmul(xb, w["w_zx"], [f32])
        sm, = _matmul(xb, w["w_small"], [f32], tn_pref=2 * LANE)
        return q, k_f, k_b, v_f, v_b, zx, sm

    qm, kp_f, km_b, vp_f, vm_b, zxm, smm = project(xm_b, (n_p, t_aux + t_main, t_aux))
    qa, ka_f, ka_b, va_f, va_b, zxa, sma = project(xa_b, None)

    seq = lambda a, n: a.reshape(n, a.shape[0] // n, a.shape[1])
    qm, km_b, vm_b, zxm, smm = [seq(a, n_p) for a in (qm, km_b, vm_b, zxm, smm)]
    qa, ka_f, va_f = [seq(a, n_aux) for a in (qa, ka_f, va_f)]
    ka_b, va_b, zxa, sma = [_pad_seq(seq(a, n_aux)) for a in (ka_b, va_b, zxa, sma)]
    kp_f = kp_f.at[:, :t_aux].set(ka_f[:n_p])
    vp_f = vp_f.at[:, :t_aux].set(va_f[:n_p])

    zero_c = jnp.zeros((n_s, 1, LANE), f32)
    _, cum_past = _logf_cumsum(_pad_lanes(logf_past.astype(f32)), w["b_f"], zero_c, gate=False)
    carry_a = jnp.concatenate([jnp.zeros((n_p, 1, LANE), f32), cum_past[:, past - 1:past]], axis=0)
    logf_a, cum_a = _logf_cumsum(sma[:, :, LANE:], w["b_f"], carry_a, gate=True)
    logf_m, cum_m = _logf_cumsum(smm[:, :, LANE:], w["b_f"], cum_a[:n_p, t_aux - 1:t_aux], gate=True)
    rows = lambda cum: jnp.swapaxes(cum[:, :, :H_A], 1, 2)[:, :, None, :]

    fa = rows(cum_a)
    att_meta = _fox_attention(qa[:n_p], ka_b[:n_p], va_b[:n_p], fa[:n_p], tq=t_aux, hps=2)
    past_kv = [c.astype(bf16).reshape(n_s, past, D_A) for c in (k_past, v_past)]
    att_s = _fox_attention(qa[n_p:], ka_b[n_p:], va_b[n_p:], fa[n_p:],
                           prefix=(past_kv[0], past_kv[1], rows(cum_past)), s1_valid=past, tq=t_aux, hps=2)
    tq = min(t_main, 1024)
    att_m =_fox_attention(qm, km_b, vm_b, rows(cum_m),
                           prefix=(ka_b[:n_p], va_b[:n_p], fa[:n_p]), s1_valid=t_aux, tq=tq)
    att_a = jnp.concatenate([att_meta, att_s], axis=0)

    hist_a = jnp.concatenate([jnp.zeros((n_p, SUBLANE, CONV_DIM), f32),
                              jnp.pad(conv_past.astype(f32), ((0, 0), (SUBLANE - (CONV_W - 1), 0), (0, 0)))], axis=0)
    to_t = lambda h: jnp.transpose(h.reshape(-1, D_SSM, N_B), (0, 2, 1))
    from_t = lambda h: jnp.transpose(h, (0, 2, 1)).reshape(-1, H_B, P_B, N_B)
    h0_a = jnp.concatenate([jnp.zeros((n_p, N_B, D_SSM), f32), to_t(h_past.astype(f32))], axis=0)
    ya, tail_a, h_a = _ssd_mixer(zxa, sma[:, :, :LANE], hist_a, h0_a, w, SEQ_PAD, t_aux)
    ym, tail_m, h_m = _ssd_mixer(zxm, smm[:, :, :LANE], tail_a[:n_p], h_a[:n_p], w, min(t_main, 128), min(t_main, 128))

    unseq = lambda a: a.reshape(-1, a.shape[-1])
    x1m_f, x1m_b = _outproj_ln([unseq(att_m), unseq(ym)], w["w_out"], xm_f, w["ln_g"], w["ln_b"])
    x1a_f, x1a_b = _outproj_ln([unseq(att_a), unseq(ya[:, :t_aux])], w["w_out"], xa_f, w["ln_g"], w["ln_b"])

    heads = lambda a: a.reshape(a.shape[0], a.shape[1], H_A, DH_A)
    new = dict(
        k_p=heads(kp_f), v_p=heads(vp_f),
        f_p=jnp.concatenate([logf_a[:n_p, :t_aux, :H_A], logf_m[:, :, :H_A]], axis=1),
        conv_p=tail_m[:, SUBLANE - (CONV_W - 1):], h_p=from_t(h_m),
        k_s=heads(ka_f[n_p:]), v_s=heads(va_f[n_p:]), f_s=logf_a[n_p:, :t_aux, :H_A],
        conv_s=tail_a[n_p:, SUBLANE - (CONV_W - 1):], h_s=from_t(h_a[n_p:]))
    return (x1m_f, x1m_b, x1a_f, x1a_b), new


def _odd_layer(xm_b, xm_f, xa_b, xa_f, n_p, t_main, cache, w):
    c_past, n_past, m_past = cache
    n_s = c_past.shape[0]
    n_aux = n_p + n_s
    t_aux = xa_f.shape[0] // n_aux

    def project(xb):
        qkv, = _matmul(xb, w["w_qkv"], [bf16], col_scale=w["qkv_scale"])
        o, = _matmul(xb, w["w_o"], [f32])
        gt, = _matmul(xb, w["w_gates"], [f32])
        return qkv, o, gt

    qkvm, om, gm = project(xm_b)
    qkva, oa, ga = project(xa_b)
    seq = lambda a, n: a.reshape(n, a.shape[0] // n, a.shape[1])
    qkvm, om, gm = [seq(a, n_p) for a in (qkvm, om, gm)]
    qkva, oa, ga = [_pad_seq(seq(a, n_aux)) for a in (qkva, oa, ga)]

    c0 = jnp.concatenate([jnp.zeros((n_p, H_C, DK_C, DV_C), f32), jnp.swapaxes(c_past.astype(f32), 2, 3)], axis=0)
    n0 = jnp.concatenate([jnp.zeros((n_p, H_C, DK_C), f32), n_past.astype(f32)], axis=0)
    m0 = jnp.concatenate([jnp.zeros((n_p, 1, LANE), f32), _pad_lanes(m_past.astype(f32))[:, None, :]], axis=0)
    ha, c_a, n_a, m_a = _mlstm_mixer(qkva, oa, ga, c0, n0, m0, w, SEQ_PAD, t_aux)
    lc = min(t_main, 256)
    hm, c_m, n_m, m_m = _mlstm_mixer(qkvm, om, gm, c_a[:n_p], n_a[:n_p], m_a[:n_p], w, lc, lc)

    unseq = lambda a: a.reshape(-1, a.shape[-1])
    x1m_f, x1m_b = _outproj_ln([unseq(hm)], w["w_out"], xm_f, w["ln_g"], w["ln_b"])
    x1a_f, x1a_b = _outproj_ln([unseq(ha[:, :t_aux])], w["w_out"], xa_f, w["ln_g"], w["ln_b"])
    new = dict(c_p=jnp.swapaxes(c_m, 2, 3), n_p=n_m, m_p=m_m[:, 0, :H_C],
               c_s=jnp.swapaxes(c_a[n_p:], 2, 3), n_s=n_a[n_p:], m_s=m_a[n_p:, 0, :H_C])
    return (x1m_f, x1m_b, x1a_f, x1a_b), new


def _even_weights(e, w_in_a, b_fgate_a, conv_w, conv_b, dt_bias, a_log, d_skip, ssd_norm_g, w_out_a):
    wi = w_in_a[e]
    o_f = 3 * D_A
    o_z = o_f + H_A
    o_dt = o_z + D_SSM + CONV_DIM
    w_small = jnp.concatenate([_pad_lanes(wi[:, o_dt:o_dt + H_B]), _pad_lanes(wi[:, o_f:o_f + H_A])], axis=1)
    expand = (np.arange(LANE)[:, None] == (np.arange(D_SSM) // P_B)[None, :])
    return dict(
        w_q=wi[:, :D_A].astype(bf16), w_k=wi[:, D_A:2 * D_A].astype(bf16),
        w_v=wi[:, 2 * D_A:3 * D_A].astype(bf16),
        w_zx=wi[:, o_z:o_dt].astype(bf16), w_small=w_small.astype(bf16),
        b_f=_pad_lanes(b_fgate_a[e][None, :]),
        conv_w=jnp.pad(conv_w[e], ((0, SUBLANE - CONV_W), (0, 0))), conv_b=conv_b[e][None, :],
        dt_bias=_pad_lanes(dt_bias[e][None, :]), a_log_row=_pad_lanes(a_log[e][None, :]),
        a_log_col=a_log[e][:, None], d_skip=jnp.repeat(d_skip[e], P_B)[None, :],
        norm_g=ssd_norm_g[e][None, :], expand=jnp.asarray(expand, bf16),
        w_out=w_out_a[e].astype(bf16))


def _odd_weights(o, w_in_c, b_igate_c, b_fgate_c, mlstm_norm_g, w_out_c):
    wi = w_in_c[o]
    o_o = 2 * D_QK_C + D_V_C
    o_g = o_o + D_V_C
    scale = jnp.concatenate([jnp.ones((D_QK_C,), f32), jnp.full((D_QK_C,), DK_C ** -0.5, f32),
                             jnp.ones((D_V_C,), f32)])[None, :]
    return dict(
        w_qkv=wi[:, :o_o].astype(bf16), qkv_scale=scale, w_o=wi[:, o_o:o_g].astype(bf16),
        w_gates=_pad_lanes(wi[:, o_g:]).astype(bf16),
        gate_bias=_pad_lanes(jnp.concatenate([b_igate_c[o], b_fgate_c[o]])[None, :]),
        norm_g=mlstm_norm_g[o][None, :], w_out=w_out_c[o].astype(bf16))


def kernel(x_prompt, x_sample, cache_fox_k, cache_fox_v, cache_fox_logf, state_ssd_conv, state_ssd, state_mlstm_c, state_mlstm_n, state_mlstm_m, meta_tokens, w_in_a, b_fgate_a, conv_w, conv_b, dt_bias, a_log, d_skip, ssd_norm_g, w_out_a, w_in_c, b_igate_c, b_fgate_c, mlstm_norm_g, w_out_c, ln_mix_g, ln_mix_b, ln_ffn_g, ln_ffn_b, w_ffn_gate, w_ffn_up, w_ffn_down):
    n_p, t_main, d = x_prompt.shape
    n_s, t_s, _ = x_sample.shape
    assert t_s == N_META, "aux rows hold equal-length meta and running-stream sequences"
    xm_f = x_prompt.reshape(n_p * t_main, d)
    xa_f = jnp.concatenate([jnp.broadcast_to(meta_tokens.astype(x_prompt.dtype)[None], (n_p, N_META, d)),
                            x_sample], axis=0).reshape((n_p + n_s) * t_s, d)
    xm_b, xa_b = xm_f.astype(bf16), xa_f.astype(bf16)

    even_out, odd_out = [], []
    for layer in range(DEPTH):
        if layer % 2 == 0:
            e = layer // 2
            w = _even_weights(e, w_in_a, b_fgate_a, conv_w, conv_b, dt_bias, a_log, d_skip, ssd_norm_g, w_out_a)
            cache = (cache_fox_k[e], cache_fox_v[e], cache_fox_logf[e], state_ssd_conv[e], state_ssd[e])
            w["ln_g"], w["ln_b"] = ln_mix_g[layer], ln_mix_b[layer]
            (x1m_f, x1m_b, x1a_f, x1a_b), new = _even_layer(xm_b, xm_f, xa_b, xa_f, n_p, t_main, cache, w)
            even_out.append(new)
        else:
            o = layer // 2
            w = _odd_weights(o, w_in_c, b_igate_c, b_fgate_c, mlstm_norm_g, w_out_c)
            cache = (state_mlstm_c[o], state_mlstm_n[o], state_mlstm_m[o])
            w["ln_g"], w["ln_b"] = ln_mix_g[layer], ln_mix_b[layer]
            (x1m_f, x1m_b, x1a_f, x1a_b), new = _odd_layer(xm_b, xm_f, xa_b, xa_f, n_p, t_main, cache, w)
            odd_out.append(new)
        wg, wu, wd = w_ffn_gate[layer].astype(bf16), w_ffn_up[layer].astype(bf16), w_ffn_down[layer].astype(bf16)
        xm_f, xm_b = _ffn_ln(x1m_b, x1m_f, wg, wu, wd, ln_ffn_g[layer], ln_ffn_b[layer])
        xa_f, xa_b = _ffn_ln(x1a_b, x1a_f, wg, wu, wd, ln_ffn_g[layer], ln_ffn_b[layer])

    y_prompt = xm_f.reshape(n_p, t_main, d)
    y_sample = xa_f.reshape(n_p + n_s, t_s, d)[n_p:]
    st = lambda outs, key: jnp.stack([o[key] for o in outs])
    return (y_prompt, y_sample,
            st(even_out, "k_p"), st(even_out, "v_p"), st(even_out, "f_p"), st(even_out, "conv_p"), st(even_out, "h_p"),
            st(odd_out, "c_p"), st(odd_out, "n_p"), st(odd_out, "m_p"),
            st(even_out, "k_s"), st(even_out, "v_s"), st(even_out, "f_s"), st(even_out, "conv_s"), st(even_out, "h_s"),
            st(odd_out, "c_s"), st(odd_out, "n_s"), st(odd_out, "m_s"))
```

```python
import functools

import jax
import jax.numpy as jnp
import numpy as np
from jax import lax
from jax.experimental import pallas as pl
from jax.experimental.pallas import tpu as pltpu

f32 = jnp.float32
bf16 = jnp.bfloat16

D_MODEL = 2048
DEPTH = 2
N_META = 16
H_A, DH_A = 16, 128
D_A = H_A * DH_A
H_B, P_B, G_B, N_B = 32, 64, 4, 128
HG_B = H_B // G_B
D_SSM = H_B * P_B
CONV_W = 4
CONV_DIM = D_SSM + 2 * G_B * N_B
H_C, DK_C, DV_C = 8, 128, 256
D_QK_C = H_C * DK_C
D_V_C = H_C * DV_C
ALPHA = (2 * DEPTH) ** 0.25
NEG = -1e30
LN_EPS = 1e-5
LOG2E = 1.4426950408889634

LANE = 128
SUBLANE = 8
SEQ_PAD = 128
VMEM_LIMIT = 56 * 1024 * 1024


def _cparams(sem):
    return pltpu.CompilerParams(dimension_semantics=sem, vmem_limit_bytes=VMEM_LIMIT)


def _tile(n, pref, mult=16):
    if n <= pref:
        return n
    t = (pref // mult) * mult
    while t >= mult:
        if n % t == 0:
            return t
        t -= mult
    return n


def _softplus(x):
    return jnp.maximum(x, 0.0) + jnp.log1p(jnp.exp(-jnp.abs(x)))


def _log_sigmoid(x):
    return -_softplus(-x)


def _sigmoid(x):
    return 1.0 / (1.0 + jnp.exp(-x))


def _split3(a):
    a1 = a.astype(bf16)
    r1 = a - a1.astype(f32)
    a2 = r1.astype(bf16)
    a3 = (r1 - a2.astype(f32)).astype(bf16)
    return a1, a2, a3


_NN = (((1,), (0,)), ((), ()))
_NT = (((1,), (1,)), ((), ()))
_TN = (((0,), (0,)), ((), ()))


def _dot(a, b, dims=_NN):
    return lax.dot_general(a, b, dims, preferred_element_type=f32)


def _dot_exact_rhs(a_bf16_exact, b_f32, dims=_NN):
    b1, b2, b3 = _split3(b_f32)
    return _dot(a_bf16_exact, b1, dims) + _dot(a_bf16_exact, b2, dims) + _dot(a_bf16_exact, b3, dims)


def _dot_exact_lhs(a_f32, b_bf16_exact, dims=_NN):
    a1, a2, a3 = _split3(a_f32)
    return _dot(a1, b_bf16_exact, dims) + _dot(a2, b_bf16_exact, dims) + _dot(a3, b_bf16_exact, dims)


def _iota2(shape, axis):
    return lax.broadcasted_iota(jnp.int32, shape, axis)


def _tri_lower(n):
    return (_iota2((n, n), 0) >= _iota2((n, n), 1)).astype(bf16)


def _tri_upper(n):
    return (_iota2((n, n), 0) <= _iota2((n, n), 1)).astype(bf16)


def _eye(rows, cols):
    return (_iota2((rows, cols), 0) == _iota2((rows, cols), 1)).astype(bf16)


def _layer_norm(x, g, b):
    mu = jnp.mean(x, axis=-1, keepdims=True)
    xc = x - mu
    var = jnp.mean(xc * xc, axis=-1, keepdims=True)
    return xc * lax.rsqrt(var + LN_EPS) * g + b


def _mm_kernel(x_ref, w_ref, s_ref, *out_refs):
    acc = _dot(x_ref[...], w_ref[...]) * s_ref[...]
    for o in out_refs:
        o[...] = acc.astype(o.dtype)


def _matmul(x, w, out_dtypes, col_scale=None, tm_pref=1024, tn_pref=512):
    m, k = x.shape
    n = w.shape[1]
    tm = _tile(m, tm_pref)
    tn = _tile(n, tn_pref, LANE)
    if col_scale is None:
        col_scale = jnp.ones((1, n), f32)
    return pl.pallas_call(
        _mm_kernel,
        grid=(m // tm, n // tn),
        in_specs=[pl.BlockSpec((tm, k), lambda i, j: (i, 0)),
                  pl.BlockSpec((k, tn), lambda i, j: (0, j)),
                  pl.BlockSpec((1, tn), lambda i, j: (0, j))],
        out_specs=[pl.BlockSpec((tm, tn), lambda i, j: (i, j)) for _ in out_dtypes],
        out_shape=[jax.ShapeDtypeStruct((m, n), dt) for dt in out_dtypes],
        compiler_params=_cparams(("parallel", "arbitrary")),
        name="proj_matmul",
    )(x, w, col_scale)


def _kv_proj_kernel(x_ref, w_ref, o_ref, ob_ref, *, tm, nh):
    acc = _dot(x_ref[...], w_ref[...])
    ob_ref[...] = acc.astype(bf16)
    for h in range(nh):
        o_ref[0, pl.ds(h, tm, stride=nh), :] = acc[:, h * DH_A:(h + 1) * DH_A]


def _kv_proj(x, w, n_seq, t_out, row_off, tm_pref=512):
    m, k = x.shape
    n = w.shape[1]
    nh = n // DH_A
    tm = _tile(m // n_seq, tm_pref)
    per_seq = m // n_seq // tm
    o, ob = pl.pallas_call(
        functools.partial(_kv_proj_kernel, tm=tm, nh=nh),
        grid=(m // tm,),
        in_specs=[pl.BlockSpec((tm, k), lambda i: (i, 0)), pl.BlockSpec((k, n), lambda i: (0, 0))],
        out_specs=[pl.BlockSpec((pl.Element(1), pl.Element(tm * nh), pl.Element(DH_A)),
                                lambda i: (i // per_seq,
                                           pl.multiple_of((row_off + (i % per_seq) * tm) * nh, SUBLANE), 0)),
                   pl.BlockSpec((tm, n), lambda i: (i, 0))],
        out_shape=[jax.ShapeDtypeStruct((n_seq, t_out * nh, DH_A), f32), jax.ShapeDtypeStruct((m, n), bf16)],
        compiler_params=_cparams(("parallel",)),
        name="kv_proj",
    )(x, w)
    return o.reshape(n_seq, t_out, nh, DH_A), ob


def _outproj_ln_kernel(*refs, n_act, kc):
    acts = refs[:n_act]
    w_ref, res_ref, g_ref, b_ref, of_ref, ob_ref, acc_ref = refs[n_act:]
    k = pl.program_id(1)

    @pl.when(k == 0)
    def _():
        acc_ref[...] = jnp.zeros_like(acc_ref)

    for a in range(n_act):
        @pl.when((k >= a * kc) & (k < (a + 1) * kc))
        def _(a=a):
            acc_ref[...] += _dot(acts[a][...], w_ref[...])

    @pl.when(k == pl.num_programs(1) - 1)
    def _():
        y = _layer_norm(ALPHA * res_ref[...] + acc_ref[...], g_ref[...], b_ref[...])
        of_ref[...] = y
        ob_ref[...] = y.astype(bf16)


def _outproj_ln(acts, w, res, g, b, tm_pref=512, tk=1024):
    m, d = res.shape
    tm = _tile(m, tm_pref)
    kc = acts[0].shape[1] // tk
    n_act = len(acts)
    in_specs = [pl.BlockSpec((tm, tk), (lambda i, k, a=a: (i, jnp.clip(k - a * kc, 0, kc - 1))))
                for a in range(n_act)]
    in_specs += [pl.BlockSpec((tk, d), lambda i, k: (k, 0)),
                 pl.BlockSpec((tm, d), lambda i, k: (i, 0)),
                 pl.BlockSpec((1, d), lambda i, k: (0, 0)),
                 pl.BlockSpec((1, d), lambda i, k: (0, 0))]
    return pl.pallas_call(
        functools.partial(_outproj_ln_kernel, n_act=n_act, kc=kc),
        grid=(m // tm, n_act * kc),
        in_specs=in_specs,
        out_specs=[pl.BlockSpec((tm, d), lambda i, k: (i, 0)),
                   pl.BlockSpec((tm, d), lambda i, k: (i, 0))],
        out_shape=[jax.ShapeDtypeStruct((m, d), f32), jax.ShapeDtypeStruct((m, d), bf16)],
        scratch_shapes=[pltpu.VMEM((tm, d), f32)],
        compiler_params=_cparams(("parallel", "arbitrary")),
        name="outproj_ln",
    )(*acts, w, res, g.reshape(1, d), b.reshape(1, d))


def _ffn_ln_kernel(xb_ref, xf_ref, wg_ref, wu_ref, wd_ref, g_ref, b_ref, of_ref, ob_ref, acc_ref):
    f = pl.program_id(1)

    @pl.when(f == 0)
    def _():
        acc_ref[...] = jnp.zeros_like(acc_ref)

    x = xb_ref[...]
    gate = _dot(x, wg_ref[...])
    up = _dot(x, wu_ref[...])
    h = (gate * _sigmoid(gate) * up).astype(bf16)
    acc_ref[...] += _dot(h, wd_ref[...])

    @pl.when(f == pl.num_programs(1) - 1)
    def _():
        y = _layer_norm(ALPHA * xf_ref[...] + acc_ref[...], g_ref[...], b_ref[...])
        of_ref[...] = y
        ob_ref[...] = y.astype(bf16)


def _ffn_ln(xb, xf, wg, wu, wd, g, b, tm_pref=512, tf_pref=512):
    m, d = xf.shape
    dff = wg.shape[1]
    tm = _tile(m, tm_pref)
    tf = _tile(dff, tf_pref, LANE)
    return pl.pallas_call(
        _ffn_ln_kernel,
        grid=(m // tm, dff // tf),
        in_specs=[pl.BlockSpec((tm, d), lambda i, f: (i, 0)),
                  pl.BlockSpec((tm, d), lambda i, f: (i, 0)),
                  pl.BlockSpec((d, tf), lambda i, f: (0, f)),
                  pl.BlockSpec((d, tf), lambda i, f: (0, f)),
                  pl.BlockSpec((tf, d), lambda i, f: (f, 0)),
                  pl.BlockSpec((1, d), lambda i, f: (0, 0)),
                  pl.BlockSpec((1, d), lambda i, f: (0, 0))],
        out_specs=[pl.BlockSpec((tm, d), lambda i, f: (i, 0)),
                   pl.BlockSpec((tm, d), lambda i, f: (i, 0))],
        out_shape=[jax.ShapeDtypeStruct((m, d), f32), jax.ShapeDtypeStruct((m, d), bf16)],
        scratch_shapes=[pltpu.VMEM((tm, d), f32)],
        compiler_params=_cparams(("parallel", "arbitrary")),
        name="ffn_ln",
    )(xb, xf, wg, wu, wd, g.reshape(1, d), b.reshape(1, d))


def _cumsum_kernel(x_ref, bias_ref, carry_ref, logf_ref, cum_ref, acc_ref, *, gate):
    c = pl.program_id(1)

    @pl.when(c == 0)
    def _():
        acc_ref[...] = carry_ref[0]

    x = x_ref[0]
    lf = _log_sigmoid(x + bias_ref[...]) if gate else x
    tc = x.shape[0]
    cum = _dot_exact_rhs(_tri_lower(tc), lf) + acc_ref[...]
    logf_ref[0] = lf
    cum_ref[0] = cum
    acc_ref[...] = cum[tc - 1:tc, :]


def _logf_cumsum(x, bias, carry, gate, tc_pref=512):
    bsz, t, w = x.shape
    tc = _tile(t, tc_pref)
    spec = pl.BlockSpec((1, tc, w), lambda b, c: (b, c, 0))
    return pl.pallas_call(
        functools.partial(_cumsum_kernel, gate=gate),
        grid=(bsz, t // tc),
        in_specs=[spec,
                  pl.BlockSpec((1, w), lambda b, c: (0, 0)),
                  pl.BlockSpec((1, 1, w), lambda b, c: (b, 0, 0))],
        out_specs=[spec, spec],
        out_shape=[jax.ShapeDtypeStruct((bsz, t, w), f32)] * 2,
        scratch_shapes=[pltpu.VMEM((1, w), f32)],
        compiler_params=_cparams(("parallel", "arbitrary")),
        name="logf_cumsum",
    )(x, bias, carry)


def _attn_kernel(*refs, tq, tk, tk2, s1, s1_valid, hps):
    if s1:
        q_ref, k1_ref, v1_ref, f1_ref, k2_ref, v2_ref, f2_ref, o_ref = refs
    else:
        q_ref, k2_ref, v2_ref, f2_ref, o_ref = refs
    qi = pl.program_id(2)
    d = DH_A

    for hh in range(hps):
        cols = slice(hh * d, (hh + 1) * d)
        q = q_ref[0, :, cols]

        def scores(k, q=q):
            return _dot(q, k.astype(bf16), _NT)

        def update(carry, s, v, fk, mask):
            m, l, acc = carry
            s = s - fk * LOG2E
            if mask is not None:
                s = jnp.where(mask, s, NEG)
            m_new = jnp.maximum(m, jnp.max(s, axis=-1, keepdims=True))
            alpha = jnp.exp2(m - m_new)
            p = jnp.exp2(s - m_new)
            l = alpha * l + jnp.sum(p, axis=-1, keepdims=True)
            acc = alpha * acc + _dot(p.astype(bf16), v.astype(bf16))
            return m_new, l, acc

        def step(j, c, mask, row0=0, cols=cols, hh=hh, q=q, update=update):
            off = pl.multiple_of(j * tk, tk)
            s = _dot(q[row0:], k2_ref[0, pl.ds(off, tk), cols].astype(bf16), _NT)
            new = update(tuple(a[row0:] for a in c), s, v2_ref[0, pl.ds(off, tk), cols],
                         f2_ref[0, hh, :, pl.ds(off, tk)], mask)
            if row0 == 0:
                return new
            return tuple(jnp.concatenate([a[:row0], b], axis=0) for a, b in zip(c, new))

        carry = (jnp.full((tq, 1), NEG, f32), jnp.zeros((tq, 1), f32), jnp.zeros((tq, d), f32))
        if s1:
            pmask = None if s1_valid == s1 else (_iota2((tq, s1), 1) < s1_valid)
            carry = update(carry, scores(k1_ref[0, :, cols]), v1_ref[0, :, cols], f1_ref[0, hh], pmask)

        nd = tk2 // tk
        carry = lax.fori_loop(0, qi * nd, lambda j, c, step=step: step(j, c, None), carry)
        for r in range(nd):
            row0 = r * tk if tq == tk2 else 0
            causal = _iota2((tq - row0, tk), 0) + row0 >= _iota2((tq - row0, tk), 1) + r * tk
            carry = step(qi * nd + r, carry, causal, row0)
        m, l, acc = carry
        o_ref[0, :, cols] = (acc / l).astype(o_ref.dtype)


def _fox_attention(q, k2, v2, f2, prefix=None, s1_valid=0, tq=512, tk=512, hps=1):
    bsz, t, hd = q.shape
    t2 = k2.shape[1]
    nq = t // tq
    tk2 = t2 // nq
    w = hps * DH_A
    qspec = pl.BlockSpec((1, tq, w), lambda b, h, i: (b, i, h))
    kvspec = pl.BlockSpec((1, t2, w), lambda b, h, i: (b, 0, h))
    fspec = pl.BlockSpec((1, hps, 1, t2), lambda b, h, i: (b, h, 0, 0))
    args, in_specs = [q], [qspec]
    s1 = 0
    if prefix is not None:
        k1, v1, f1 = prefix
        s1 = k1.shape[1]
        p_spec = pl.BlockSpec((1, s1, w), lambda b, h, i: (b, 0, h))
        args += [k1, v1, f1]
        in_specs += [p_spec, p_spec, pl.BlockSpec((1, hps, 1, s1), lambda b, h, i: (b, h, 0, 0))]
    args += [k2, v2, f2]
    in_specs += [kvspec, kvspec, fspec]
    return pl.pallas_call(
        functools.partial(_attn_kernel, tq=tq, tk=min(tk, tk2), tk2=tk2, s1=s1, s1_valid=s1_valid, hps=hps),
        grid=(bsz, hd // w, nq),
        in_specs=in_specs,
        out_specs=qspec,
        out_shape=jax.ShapeDtypeStruct((bsz, t, hd), bf16),
        compiler_params=_cparams(("parallel", "parallel", "arbitrary")),
        name="fox_attention",
    )(*args)


def _forget_columns(cum, h):
    r, c = _iota2((LANE, LANE), 0), _iota2((LANE, LANE), 1)
    out = None
    for i, piece in enumerate(_split3(cum * LOG2E)):
        t = _dot(piece, ((r == h) & (c == i)).astype(bf16))
        out = t if out is None else out + t
    return out.astype(bf16)


def _attn_t_kernel(q_ref, k1_ref, v1_ref, c1_ref, k2_ref, v2_ref, c2_ref, o_ref, kaug_ref, *, tq, s1_valid, hps):
    hg = pl.program_id(1)
    qi = pl.program_id(2)
    d = DH_A

    @pl.when(qi == 0)
    def _():
        for hh in range(hps):
            kaug_ref[hh, :, 0:d] = k2_ref[0, :, hh * d:(hh + 1) * d]
            kaug_ref[hh, :, d:2 * d] = _forget_columns(c2_ref[0], hg * hps + hh)

    minus_one = jnp.where(_iota2((tq, d), 1) < 3, -1.0, 0.0).astype(bf16)
    q_aug = [jnp.concatenate([q_ref[0, :, hh * d:(hh + 1) * d], minus_one], axis=-1) for hh in range(hps)]

    def update(carry, s, v, mask):
        m, l, acc = carry
        if mask is not None:
            s = jnp.where(mask, s, NEG)
        m_new = jnp.maximum(m, jnp.max(s, axis=0, keepdims=True))
        alpha = jnp.exp2(m - m_new)
        p = jnp.exp2(s - m_new)
        l = alpha * l + jnp.sum(p, axis=0, keepdims=True)
        acc = alpha * acc + _dot(v, p.astype(bf16), _TN)
        return m_new, l, acc

    def multi(carries, k_augs, vs, mask):
        ss = [_dot(k_augs[hh], q_aug[hh], _NT) for hh in range(hps)]
        return [update(carries[hh], ss[hh], vs[hh], mask) for hh in range(hps)]

    carries = [(jnp.full((1, tq), NEG, f32), jnp.zeros((1, tq), f32), jnp.zeros((d, tq), f32))
               for _ in range(hps)]
    s1 = k1_ref.shape[1]
    k1_aug = [jnp.concatenate([k1_ref[0, :, hh * d:(hh + 1) * d], _forget_columns(c1_ref[0], hg * hps + hh)], axis=-1)
              for hh in range(hps)]
    carries = multi(carries, k1_aug, [v1_ref[0, :, hh * d:(hh + 1) * d] for hh in range(hps)],
                    _iota2((s1, tq), 0) < s1_valid)

    def blk(j, c, mask):
        off = pl.multiple_of(j * tq, tq)
        return multi(c, [kaug_ref[hh, pl.ds(off, tq), :] for hh in range(hps)],
                     [v2_ref[0, pl.ds(off, tq), hh * d:(hh + 1) * d] for hh in range(hps)], mask)

    carries = lax.fori_loop(0, qi, lambda j, c: blk(j, c, None), carries)
    carries = blk(qi, carries, _iota2((tq, tq), 0) <= _iota2((tq, tq), 1))
    for hh in range(hps):
        m, l, acc = carries[hh]
        o_ref[0, :, hh * d:(hh + 1) * d] = jnp.transpose(acc / l).astype(o_ref.dtype)


def _fox_attention_main(q, k1, v1, c1, k2, v2, c2, s1_valid, tq=512, hps=4):
    bsz, t, hd = q.shape
    s1 = k1.shape[1]
    w = hps * DH_A
    qspec = pl.BlockSpec((1, tq, w), lambda b, h, i: (b, i, h))
    kvspec = pl.BlockSpec((1, t, w), lambda b, h, i: (b, 0, h))
    pspec = pl.BlockSpec((1, s1, w), lambda b, h, i: (b, 0, h))
    return pl.pallas_call(
        functools.partial(_attn_t_kernel, tq=tq, s1_valid=s1_valid, hps=hps),
        grid=(bsz, hd // w, t // tq),
        in_specs=[qspec, pspec, pspec, pl.BlockSpec((1, s1, LANE), lambda b, h, i: (b, 0, 0)),
                  kvspec, kvspec, pl.BlockSpec((1, t, LANE), lambda b, h, i: (b, 0, 0))],
        out_specs=qspec,
        out_shape=jax.ShapeDtypeStruct((bsz, t, hd), bf16),
        scratch_shapes=[pltpu.VMEM((hps, t, 2 * DH_A), bf16)],
        compiler_params=_cparams(("parallel", "parallel", "arbitrary")),
        name="fox_attention_main",
    )(q, k1, v1, c1, k2, v2, c2)


def _ssd_kernel(zx_ref, dt_ref, hist_ref, h0_ref, cw_ref, cb_ref, dtb_ref, alog_ref, alogc_ref,
                dskip_ref, ng_ref, e_ref, y_ref, tail_ref, hout_ref, ext_ref, st_ref, *, lc, t_valid):
    c = pl.program_id(1)

    @pl.when(c == 0)
    def _():
        ext_ref[0:SUBLANE, :] = hist_ref[0]
        st_ref[...] = h0_ref[0]

    ext_ref[SUBLANE:SUBLANE + lc, :] = zx_ref[0, :, D_SSM:]
    conv = cb_ref[...]
    for i in range(CONV_W):
        lo = SUBLANE - (CONV_W - 1) + i
        conv = conv + cw_ref[i:i + 1, :] * ext_ref[lo:lo + lc, :]
    xc = conv * _sigmoid(conv)
    tail = ext_ref[t_valid:t_valid + SUBLANE, :]
    ext_ref[0:SUBLANE, :] = tail

    dt = _softplus(dt_ref[0] + dtb_ref[...])
    if t_valid < lc:
        dt = jnp.where(_iota2((lc, LANE), 0) < t_valid, dt, 0.0)
    a_row = -jnp.exp(alog_ref[...])
    a_col = -jnp.exp(alogc_ref[...])
    a_cum = _dot_exact_rhs(_tri_lower(lc), dt * a_row)
    dt_t = _dot_exact_rhs(_eye(H_B, LANE), dt, _NT)
    a_cum_t = _dot_exact_lhs(dt_t * a_col, _tri_upper(lc))
    a_last = a_cum[lc - 1:lc, :]
    e = e_ref[...]
    to_end_x = _dot_exact_lhs(jnp.exp(a_last - a_cum) * dt, e)
    ea_x = _dot_exact_lhs(jnp.exp(a_cum), e)
    cdec_x = _dot_exact_lhs(jnp.broadcast_to(jnp.exp(a_last), (SUBLANE, LANE)), e)[0:1, :]

    causal = _iota2((lc, lc), 0) >= _iota2((lc, lc), 1)
    left = _iota2((lc, LANE), 1) < P_B
    eye_n = _eye(N_B, N_B)
    gw = D_SSM // G_B
    for g in range(G_B):
        bg = xc[:, D_SSM + g * N_B:D_SSM + (g + 1) * N_B].astype(bf16)
        cg = xc[:, D_SSM + G_B * N_B + g * N_B:D_SSM + G_B * N_B + (g + 1) * N_B].astype(bf16)
        cb = _dot(cg, bg, _NT)
        bg_t = _dot(eye_n, bg, _NT).astype(bf16)
        xg = xc[:, g * gw:(g + 1) * gw]
        st_g = st_ref[:, g * gw:(g + 1) * gw]
        y_off = _dot(cg, st_g.astype(bf16)) * ea_x[:, g * gw:(g + 1) * gw]
        upd = _dot(bg_t, (xg * to_end_x[:, g * gw:(g + 1) * gw]).astype(bf16))
        st_ref[:, g * gw:(g + 1) * gw] = cdec_x[:, g * gw:(g + 1) * gw] * st_g + upd
        pairs = []
        for j in range(gw // LANE):
            xp = xg[:, j * LANE:(j + 1) * LANE].astype(bf16)
            res = []
            for h in (g * HG_B + 2 * j, g * HG_B + 2 * j + 1):
                seg = a_cum[:, h:h + 1] - a_cum_t[h:h + 1, :]
                dec = jnp.exp(jnp.where(causal, seg, NEG))
                w = (cb * dec * dt_t[h:h + 1, :]).astype(bf16)
                res.append(_dot(w, xp))
            pairs.append(jnp.where(left, res[0], res[1]))
        y = jnp.concatenate(pairs, axis=-1) + y_off + dskip_ref[:, g * gw:(g + 1) * gw] * xg
        z = zx_ref[0, :, g * gw:(g + 1) * gw]
        y = y * (z * _sigmoid(z))
        y = y * lax.rsqrt(jnp.mean(y * y, axis=-1, keepdims=True) + LN_EPS) * ng_ref[:, g * gw:(g + 1) * gw]
        y_ref[0, :, g * gw:(g + 1) * gw] = y.astype(y_ref.dtype)

    @pl.when(c == pl.num_programs(1) - 1)
    def _():
        tail_ref[0] = tail
        hout_ref[0] = st_ref[...]


def _ssd_mixer(zx, dt_raw, hist, h0_t, p, lc, t_valid):
    bsz, t, wz = zx.shape
    const = lambda shape: pl.BlockSpec(shape, lambda b, c: (0,) * len(shape))
    return pl.pallas_call(
        functools.partial(_ssd_kernel, lc=lc, t_valid=t_valid),
        grid=(bsz, t // lc),
        in_specs=[pl.BlockSpec((1, lc, wz), lambda b, c: (b, c, 0)),
                  pl.BlockSpec((1, lc, LANE), lambda b, c: (b, c, 0)),
                  pl.BlockSpec((1, SUBLANE, CONV_DIM), lambda b, c: (b, 0, 0)),
                  pl.BlockSpec((1, N_B, D_SSM), lambda b, c: (b, 0, 0)),
                  const((SUBLANE, CONV_DIM)), const((1, CONV_DIM)), const((1, LANE)), const((1, LANE)),
                  const((H_B, 1)), const((1, D_SSM)), const((1, D_SSM)), const((LANE, D_SSM))],
        out_specs=[pl.BlockSpec((1, lc, D_SSM), lambda b, c: (b, c, 0)),
                   pl.BlockSpec((1, SUBLANE, CONV_DIM), lambda b, c: (b, 0, 0)),
                   pl.BlockSpec((1, N_B, D_SSM), lambda b, c: (b, 0, 0))],
        out_shape=[jax.ShapeDtypeStruct((bsz, t, D_SSM), bf16),
                   jax.ShapeDtypeStruct((bsz, SUBLANE, CONV_DIM), f32),
                   jax.ShapeDtypeStruct((bsz, N_B, D_SSM), f32)],
        scratch_shapes=[pltpu.VMEM((SUBLANE + lc, CONV_DIM), f32), pltpu.VMEM((N_B, D_SSM), f32)],
        compiler_params=_cparams(("parallel", "arbitrary")),
        name="ssd_mixer",
    )(zx, dt_raw, hist, h0_t, p["conv_w"], p["conv_b"], p["dt_bias"], p["a_log_row"], p["a_log_col"],
      p["d_skip"], p["norm_g"], p["expand"])


def _mlstm_kernel(qkv_ref, o_ref, gates_ref, gb_ref, ng_ref, c0_ref, n0_ref, m0_ref,
                  y_ref, cout_ref, nout_ref, mout_ref, ct_ref, n_ref, m_ref, *, lc, t_valid):
    c = pl.program_id(1)

    @pl.when(c == 0)
    def _():
        ct_ref[...] = c0_ref[0]
        n_ref[...] = n0_ref[0]
        m_ref[...] = m0_ref[0]

    lane = _iota2((lc, LANE), 1)
    g = gates_ref[0] + gb_ref[...]
    is_f = (lane >= H_C) & (lane < 2 * H_C)
    lf = jnp.where(is_f, _log_sigmoid(g), 0.0)
    li = g
    if t_valid < lc:
        valid = _iota2((lc, LANE), 0) < t_valid
        lf = jnp.where(valid, lf, 0.0)
        li = jnp.where(valid, li, NEG)
    b_cum = _dot_exact_rhs(_tri_lower(lc), lf)
    rows = 2 * H_C
    li_t = _dot_exact_rhs(_eye(rows, LANE), li, _NT)
    lf_t = _dot_exact_rhs(_eye(rows, LANE), lf, _NT)
    b_cum_t = _dot_exact_lhs(lf_t, _tri_upper(lc))

    causal = _iota2((lc, lc), 0) >= _iota2((lc, lc), 1)
    eye_k = _eye(DK_C, DK_C)
    m_all = m_ref[...]
    m_out = m_all
    for h in range(H_C):
        qh = qkv_ref[0, :, h * DK_C:(h + 1) * DK_C]
        kh = qkv_ref[0, :, D_QK_C + h * DK_C:D_QK_C + (h + 1) * DK_C]
        vh = qkv_ref[0, :, 2 * D_QK_C + h * DV_C:2 * D_QK_C + (h + 1) * DV_C]
        bcol = b_cum[:, H_C + h:H_C + h + 1]
        brow = b_cum_t[H_C + h:H_C + h + 1, :]
        lirow = li_t[h:h + 1, :]
        m_st = m_all[:, h:h + 1]
        d_mat = jnp.where(causal, bcol - brow + lirow, NEG)
        inter = bcol + m_st
        m_row = jnp.maximum(inter, jnp.max(d_mat, axis=-1, keepdims=True))
        w = jnp.exp(d_mat - m_row) * _dot(qh, kh, _NT)
        g_inter = jnp.exp(inter - m_row)
        ct_h = ct_ref[h]
        n_h = n_ref[h:h + 1, :]
        num = _dot(w.astype(bf16), vh) + g_inter * _dot(qh, ct_h.astype(bf16))
        den = jnp.sum(w, axis=-1, keepdims=True) + g_inter * jnp.sum(qh.astype(f32) * n_h, axis=-1, keepdims=True)
        hh = num / jnp.maximum(jnp.abs(den), jnp.exp(-m_row))
        hh = hh * lax.rsqrt(jnp.mean(hh * hh, axis=-1, keepdims=True) + LN_EPS) * ng_ref[:, h * DV_C:(h + 1) * DV_C]
        og = o_ref[0, :, h * DV_C:(h + 1) * DV_C]
        y_ref[0, :, h * DV_C:(h + 1) * DV_C] = (hh * _sigmoid(og)).astype(y_ref.dtype)
        b_tot = bcol[lc - 1:lc, :]
        d_end = b_tot - brow + lirow
        m_new = jnp.maximum(b_tot + m_st, jnp.max(d_end, axis=-1, keepdims=True))
        w_end = jnp.exp(d_end - m_new)
        g_old = jnp.exp(b_tot + m_st - m_new)
        kh_t = _dot(eye_k, kh, _NT)
        ct_ref[h] = g_old * ct_h + _dot((kh_t * w_end).astype(bf16), vh)
        n_upd = _dot(jnp.broadcast_to(w_end, (SUBLANE, lc)).astype(bf16), kh)[0:1, :]
        n_ref[h:h + 1, :] = g_old * n_h + n_upd
        m_out = jnp.where(lane[0:1, :] == h, m_new, m_out)
    m_ref[...] = m_out

    @pl.when(c == pl.num_programs(1) - 1)
    def _():
        cout_ref[0] = ct_ref[...]
        nout_ref[0] = n_ref[...]
        mout_ref[0] = m_ref[...]


def _mlstm_mixer(qkv, o, gates, c0_t, n0, m0, p, lc, t_valid):
    bsz, t, wq = qkv.shape
    const = lambda shape: pl.BlockSpec(shape, lambda b, c: (0,) * len(shape))
    return pl.pallas_call(
        functools.partial(_mlstm_kernel, lc=lc, t_valid=t_valid),
        grid=(bsz, t // lc),
        in_specs=[pl.BlockSpec((1, lc, wq), lambda b, c: (b, c, 0)),
                  pl.BlockSpec((1, lc, D_V_C), lambda b, c: (b, c, 0)),
                  pl.BlockSpec((1, lc, LANE), lambda b, c: (b, c, 0)),
                  const((1, LANE)), const((1, D_V_C)),
                  pl.BlockSpec((1, H_C, DK_C, DV_C), lambda b, c: (b, 0, 0, 0)),
                  pl.BlockSpec((1, H_C, DK_C), lambda b, c: (b, 0, 0)),
                  pl.BlockSpec((1, 1, LANE), lambda b, c: (b, 0, 0))],
        out_specs=[pl.BlockSpec((1, lc, D_V_C), lambda b, c: (b, c, 0)),
                   pl.BlockSpec((1, H_C, DK_C, DV_C), lambda b, c: (b, 0, 0, 0)),
                   pl.BlockSpec((1, H_C, DK_C), lambda b, c: (b, 0, 0)),
                   pl.BlockSpec((1, 1, LANE), lambda b, c: (b, 0, 0))],
        out_shape=[jax.ShapeDtypeStruct((bsz, t, D_V_C), bf16),
                   jax.ShapeDtypeStruct((bsz, H_C, DK_C, DV_C), f32),
                   jax.ShapeDtypeStruct((bsz, H_C, DK_C), f32),
                   jax.ShapeDtypeStruct((bsz, 1, LANE), f32)],
        scratch_shapes=[pltpu.VMEM((H_C, DK_C, DV_C), f32), pltpu.VMEM((H_C, DK_C), f32),
                        pltpu.VMEM((1, LANE), f32)],
        compiler_params=_cparams(("parallel", "arbitrary")),
        name="mlstm_mixer",
    )(qkv, o, gates, p["gate_bias"], p["norm_g"], c0_t, n0, m0)


def _pad_lanes(a, width=LANE):
    return jnp.pad(a, [(0, 0)] * (a.ndim - 1) + [(0, width - a.shape[-1])])


def _pad_seq(a, t_pad=SEQ_PAD):
    return jnp.pad(a, [(0, 0), (0, t_pad - a.shape[1])] + [(0, 0)] * (a.ndim - 2))


def _even_layer(xm_b, xm_f, xa_b, xa_f, n_p, t_main, cache, w):
    k_past, v_past, logf_past, conv_past, h_past = cache
    n_s = k_past.shape[0]
    n_aux = n_p + n_s
    t_aux = xa_f.shape[0] // n_aux
    past = k_past.shape[1]

    def project(xb, main):
        q, = _matmul(xb, w["w_q"], [bf16], col_scale=jnp.full((1, D_A), DH_A ** -0.5 * LOG2E, f32))
        if main:
            k_f, k_b = _kv_proj(xb, w["w_k"], n_p, t_aux + t_main, t_aux)
            v_f, v_b = _kv_proj(xb, w["w_v"], n_p, t_aux + t_main, t_aux)
        else:
            k_f, k_b = _matmul(xb, w["w_k"], [f32, bf16])
            v_f, v_b = _matmul(xb, w["w_v"], [f32, bf16])
        zx, = _matmul(xb, w["w_zx"], [f32])
        sm, = _matmul(xb, w["w_small"], [f32], tn_pref=2 * LANE)
        return q, k_f, k_b, v_f, v_b, zx, sm

    qm, kp_f, km_b, vp_f, vm_b, zxm, smm = project(xm_b, True)
    qa, ka_f, ka_b, va_f, va_b, zxa, sma = project(xa_b, False)

    seq = lambda a, n: a.reshape(n, a.shape[0] // n, a.shape[1])
    heads = lambda a: a.reshape(a.shape[0], a.shape[1], H_A, DH_A)
    qm, km_b, vm_b, zxm, smm = [seq(a, n_p) for a in (qm, km_b, vm_b, zxm, smm)]
    qa, ka_f, va_f = [seq(a, n_aux) for a in (qa, ka_f, va_f)]
    ka_b, va_b, zxa, sma = [_pad_seq(seq(a, n_aux)) for a in (ka_b, va_b, zxa, sma)]
    ka_f, va_f = heads(ka_f), heads(va_f)
    kp_f = kp_f.at[:, :t_aux].set(ka_f[:n_p])
    vp_f = vp_f.at[:, :t_aux].set(va_f[:n_p])

    zero_c = jnp.zeros((n_s, 1, LANE), f32)
    _, cum_past = _logf_cumsum(_pad_lanes(logf_past.astype(f32)), w["b_f"], zero_c, gate=False)
    carry_a = jnp.concatenate([jnp.zeros((n_p, 1, LANE), f32), cum_past[:, past - 1:past]], axis=0)
    logf_a, cum_a = _logf_cumsum(sma[:, :, LANE:], w["b_f"], carry_a, gate=True)
    logf_m, cum_m = _logf_cumsum(smm[:, :, LANE:], w["b_f"], cum_a[:n_p, t_aux - 1:t_aux], gate=True)
    rows = lambda cum: jnp.swapaxes(cum[:, :, :H_A], 1, 2)[:, :, None, :]

    fa = rows(cum_a)
    att_meta = _fox_attention(qa[:n_p], ka_b[:n_p], va_b[:n_p], fa[:n_p], tq=t_aux, hps=2)
    past_kv = [c.astype(bf16).reshape(n_s, past, D_A) for c in (k_past, v_past)]
    att_s = _fox_attention(qa[n_p:], ka_b[n_p:], va_b[n_p:], fa[n_p:],
                           prefix=(past_kv[0], past_kv[1], rows(cum_past)), s1_valid=past, tq=t_aux, hps=2)
    att_m = _fox_attention_main(qm, ka_b[:n_p], va_b[:n_p], cum_a[:n_p], km_b, vm_b, cum_m, t_aux,
                                tq=min(t_main, 512))
    att_a = jnp.concatenate([att_meta, att_s], axis=0)

    hist_a = jnp.concatenate([jnp.zeros((n_p, SUBLANE, CONV_DIM), f32),
                              jnp.pad(conv_past.astype(f32), ((0, 0), (SUBLANE - (CONV_W - 1), 0), (0, 0)))], axis=0)
    to_t = lambda h: jnp.transpose(h.reshape(-1, D_SSM, N_B), (0, 2, 1))
    from_t = lambda h: jnp.transpose(h, (0, 2, 1)).reshape(-1, H_B, P_B, N_B)
    h0_a = jnp.concatenate([jnp.zeros((n_p, N_B, D_SSM), f32), to_t(h_past.astype(f32))], axis=0)
    ya, tail_a, h_a = _ssd_mixer(zxa, sma[:, :, :LANE], hist_a, h0_a, w, SEQ_PAD, t_aux)
    ym, tail_m, h_m = _ssd_mixer(zxm, smm[:, :, :LANE], tail_a[:n_p], h_a[:n_p], w, min(t_main, 128), min(t_main, 128))

    unseq = lambda a: a.reshape(-1, a.shape[-1])
    x1m_f, x1m_b = _outproj_ln([unseq(att_m), unseq(ym)], w["w_out"], xm_f, w["ln_g"], w["ln_b"])
    x1a_f, x1a_b = _outproj_ln([unseq(att_a), unseq(ya[:, :t_aux])], w["w_out"], xa_f, w["ln_g"], w["ln_b"])

    new = dict(
        k_p=kp_f, v_p=vp_f,
        f_p=jnp.concatenate([logf_a[:n_p, :t_aux, :H_A], logf_m[:, :, :H_A]], axis=1),
        conv_p=tail_m[:, SUBLANE - (CONV_W - 1):], h_p=from_t(h_m),
        k_s=ka_f[n_p:], v_s=va_f[n_p:], f_s=logf_a[n_p:, :t_aux, :H_A],
        conv_s=tail_a[n_p:, SUBLANE - (CONV_W - 1):], h_s=from_t(h_a[n_p:]))
    return (x1m_f, x1m_b, x1a_f, x1a_b), new


def _odd_layer(xm_b, xm_f, xa_b, xa_f, n_p, t_main, cache, w):
    c_past, n_past, m_past = cache
    n_s = c_past.shape[0]
    n_aux = n_p + n_s
    t_aux = xa_f.shape[0] // n_aux

    def project(xb):
        qkv, = _matmul(xb, w["w_qkv"], [bf16], col_scale=w["qkv_scale"])
        o, = _matmul(xb, w["w_o"], [f32])
        gt, = _matmul(xb, w["w_gates"], [f32])
        return qkv, o, gt

    qkvm, om, gm = project(xm_b)
    qkva, oa, ga = project(xa_b)
    seq = lambda a, n: a.reshape(n, a.shape[0] // n, a.shape[1])
    qkvm, om, gm = [seq(a, n_p) for a in (qkvm, om, gm)]
    qkva, oa, ga = [_pad_seq(seq(a, n_aux)) for a in (qkva, oa, ga)]

    c0 = jnp.concatenate([jnp.zeros((n_p, H_C, DK_C, DV_C), f32), jnp.swapaxes(c_past.astype(f32), 2, 3)], axis=0)
    n0 = jnp.concatenate([jnp.zeros((n_p, H_C, DK_C), f32), n_past.astype(f32)], axis=0)
    m0 = jnp.concatenate([jnp.zeros((n_p, 1, LANE), f32), _pad_lanes(m_past.astype(f32))[:, None, :]], axis=0)
    ha, c_a, n_a, m_a = _mlstm_mixer(qkva, oa, ga, c0, n0, m0, w, SEQ_PAD, t_aux)
    lc = min(t_main, 256)
    hm, c_m, n_m, m_m = _mlstm_mixer(qkvm, om, gm, c_a[:n_p], n_a[:n_p], m_a[:n_p], w, lc, lc)

    unseq = lambda a: a.reshape(-1, a.shape[-1])
    x1m_f, x1m_b = _outproj_ln([unseq(hm)], w["w_out"], xm_f, w["ln_g"], w["ln_b"])
    x1a_f, x1a_b = _outproj_ln([unseq(ha[:, :t_aux])], w["w_out"], xa_f, w["ln_g"], w["ln_b"])
    new = dict(c_p=jnp.swapaxes(c_m, 2, 3), n_p=n_m, m_p=m_m[:, 0, :H_C],
               c_s=jnp.swapaxes(c_a[n_p:], 2, 3), n_s=n_a[n_p:], m_s=m_a[n_p:, 0, :H_C])
    return (x1m_f, x1m_b, x1a_f, x1a_b), new


def _even_weights(e, w_in_a, b_fgate_a, conv_w, conv_b, dt_bias, a_log, d_skip, ssd_norm_g, w_out_a):
    wi = w_in_a[e]
    o_f = 3 * D_A
    o_z = o_f + H_A
    o_dt = o_z + D_SSM + CONV_DIM
    w_small = jnp.concatenate([_pad_lanes(wi[:, o_dt:o_dt + H_B]), _pad_lanes(wi[:, o_f:o_f + H_A])], axis=1)
    expand = (np.arange(LANE)[:, None] == (np.arange(D_SSM) // P_B)[None, :])
    return dict(
        w_q=wi[:, :D_A].astype(bf16), w_k=wi[:, D_A:2 * D_A].astype(bf16),
        w_v=wi[:, 2 * D_A:3 * D_A].astype(bf16),
        w_zx=wi[:, o_z:o_dt].astype(bf16), w_small=w_small.astype(bf16),
        b_f=_pad_lanes(b_fgate_a[e][None, :]),
        conv_w=jnp.pad(conv_w[e], ((0, SUBLANE - CONV_W), (0, 0))), conv_b=conv_b[e][None, :],
        dt_bias=_pad_lanes(dt_bias[e][None, :]), a_log_row=_pad_lanes(a_log[e][None, :]),
        a_log_col=a_log[e][:, None], d_skip=jnp.repeat(d_skip[e], P_B)[None, :],
        norm_g=ssd_norm_g[e][None, :], expand=jnp.asarray(expand, bf16),
        w_out=w_out_a[e].astype(bf16))


def _odd_weights(o, w_in_c, b_igate_c, b_fgate_c, mlstm_norm_g, w_out_c):
    wi = w_in_c[o]
    o_o = 2 * D_QK_C + D_V_C
    o_g = o_o + D_V_C
    scale = jnp.concatenate([jnp.ones((D_QK_C,), f32), jnp.full((D_QK_C,), DK_C ** -0.5, f32),
                             jnp.ones((D_V_C,), f32)])[None, :]
    return dict(
        w_qkv=wi[:, :o_o].astype(bf16), qkv_scale=scale, w_o=wi[:, o_o:o_g].astype(bf16),
        w_gates=_pad_lanes(wi[:, o_g:]).astype(bf16),
        gate_bias=_pad_lanes(jnp.concatenate([b_igate_c[o], b_fgate_c[o]])[None, :]),
        norm_g=mlstm_norm_g[o][None, :], w_out=w_out_c[o].astype(bf16))


def kernel(x_prompt, x_sample, cache_fox_k, cache_fox_v, cache_fox_logf, state_ssd_conv, state_ssd, state_mlstm_c, state_mlstm_n, state_mlstm_m, meta_tokens, w_in_a, b_fgate_a, conv_w, conv_b, dt_bias, a_log, d_skip, ssd_norm_g, w_out_a, w_in_c, b_igate_c, b_fgate_c, mlstm_norm_g, w_out_c, ln_mix_g, ln_mix_b, ln_ffn_g, ln_ffn_b, w_ffn_gate, w_ffn_up, w_ffn_down):
    n_p, t_main, d = x_prompt.shape
    n_s, t_s, _ = x_sample.shape
    assert t_s == N_META, "aux rows hold equal-length meta and running-stream sequences"
    xm_f = x_prompt.reshape(n_p * t_main, d)
    xa_f = jnp.concatenate([jnp.broadcast_to(meta_tokens.astype(x_prompt.dtype)[None], (n_p, N_META, d)),
                            x_sample], axis=0).reshape((n_p + n_s) * t_s, d)
    xm_b, xa_b = xm_f.astype(bf16), xa_f.astype(bf16)

    even_out, odd_out = [], []
    for layer in range(DEPTH):
        if layer % 2 == 0:
            e = layer // 2
            w = _even_weights(e, w_in_a, b_fgate_a, conv_w, conv_b, dt_bias, a_log, d_skip, ssd_norm_g, w_out_a)
            cache = (cache_fox_k[e], cache_fox_v[e], cache_fox_logf[e], state_ssd_conv[e], state_ssd[e])
            w["ln_g"], w["ln_b"] = ln_mix_g[layer], ln_mix_b[layer]
            (x1m_f, x1m_b, x1a_f, x1a_b), new = _even_layer(xm_b, xm_f, xa_b, xa_f, n_p, t_main, cache, w)
            even_out.append(new)
        else:
            o = layer // 2
            w = _odd_weights(o, w_in_c, b_igate_c, b_fgate_c, mlstm_norm_g, w_out_c)
            cache = (state_mlstm_c[o], state_mlstm_n[o], state_mlstm_m[o])
            w["ln_g"], w["ln_b"] = ln_mix_g[layer], ln_mix_b[layer]
            (x1m_f, x1m_b, x1a_f, x1a_b), new = _odd_layer(xm_b, xm_f, xa_b, xa_f, n_p, t_main, cache, w)
            odd_out.append(new)
        wg, wu, wd = w_ffn_gate[layer].astype(bf16), w_ffn_up[layer].astype(bf16), w_ffn_down[layer].astype(bf16)
        xm_f, xm_b = _ffn_ln(x1m_b, x1m_f, wg, wu, wd, ln_ffn_g[layer], ln_ffn_b[layer])
        xa_f, xa_b = _ffn_ln(x1a_b, x1a_f, wg, wu, wd, ln_ffn_g[layer], ln_ffn_b[layer])

    y_prompt = xm_f.reshape(n_p, t_main, d)
    y_sample = xa_f.reshape(n_p + n_s, t_s, d)[n_p:]
    st = lambda outs, key: jnp.stack([o[key] for o in outs])
    return (y_prompt, y_sample,
            st(even_out, "k_p"), st(even_out, "v_p"), st(even_out, "f_p"), st(even_out, "conv_p"), st(even_out, "h_p"),
            st(odd_out, "c_p"), st(odd_out, "n_p"), st(odd_out, "m_p"),
            st(even_out, "k_s"), st(even_out, "v_s"), st(even_out, "f_s"), st(even_out, "conv_s"), st(even_out, "h_s"),
            st(odd_out, "c_s"), st(odd_out, "n_s"), st(odd_out, "m_s"))
```

```python
import functools

import jax
import jax.numpy as jnp
import numpy as np
from jax import lax
from jax.experimental import pallas as pl
from jax.experimental.pallas import tpu as pltpu

f32 = jnp.float32
bf16 = jnp.bfloat16

D_MODEL = 2048
DEPTH = 2
N_META = 16
H_A, DH_A = 16, 128
D_A = H_A * DH_A
H_B, P_B, G_B, N_B = 32, 64, 4, 128
HG_B = H_B // G_B
D_SSM = H_B * P_B
CONV_W = 4
CONV_DIM = D_SSM + 2 * G_B * N_B
H_C, DK_C, DV_C = 8, 128, 256
D_QK_C = H_C * DK_C
D_V_C = H_C * DV_C
ALPHA = (2 * DEPTH) ** 0.25
NEG = -1e30
LN_EPS = 1e-5
LOG2E = 1.4426950408889634

LANE = 128
SUBLANE = 8
SEQ_PAD = 128
VMEM_LIMIT = 56 * 1024 * 1024


def _cparams(sem):
    return pltpu.CompilerParams(dimension_semantics=sem, vmem_limit_bytes=VMEM_LIMIT)


def _tile(n, pref, mult=16):
    if n <= pref:
        return n
    t = (pref // mult) * mult
    while t >= mult:
        if n % t == 0:
            return t
        t -= mult
    return n


def _softplus(x):
    return jnp.maximum(x, 0.0) + jnp.log1p(jnp.exp(-jnp.abs(x)))


def _log_sigmoid(x):
    return -_softplus(-x)


def _sigmoid(x):
    return 1.0 / (1.0 + jnp.exp(-x))


def _split3(a):
    a1 = a.astype(bf16)
    r1 = a - a1.astype(f32)
    a2 = r1.astype(bf16)
    a3 = (r1 - a2.astype(f32)).astype(bf16)
    return a1, a2, a3


_NN = (((1,), (0,)), ((), ()))
_NT = (((1,), (1,)), ((), ()))
_TN = (((0,), (0,)), ((), ()))


def _dot(a, b, dims=_NN):
    return lax.dot_general(a, b, dims, preferred_element_type=f32)


def _dot_exact_rhs(a_bf16_exact, b_f32, dims=_NN):
    b1, b2, b3 = _split3(b_f32)
    return _dot(a_bf16_exact, b1, dims) + _dot(a_bf16_exact, b2, dims) + _dot(a_bf16_exact, b3, dims)


def _dot_exact_lhs(a_f32, b_bf16_exact, dims=_NN):
    a1, a2, a3 = _split3(a_f32)
    return _dot(a1, b_bf16_exact, dims) + _dot(a2, b_bf16_exact, dims) + _dot(a3, b_bf16_exact, dims)


def _iota2(shape, axis):
    return lax.broadcasted_iota(jnp.int32, shape, axis)


def _tri_lower(n):
    return (_iota2((n, n), 0) >= _iota2((n, n), 1)).astype(bf16)


def _tri_upper(n):
    return (_iota2((n, n), 0) <= _iota2((n, n), 1)).astype(bf16)


def _eye(rows, cols):
    return (_iota2((rows, cols), 0) == _iota2((rows, cols), 1)).astype(bf16)


def _layer_norm(x, g, b):
    mu = jnp.mean(x, axis=-1, keepdims=True)
    xc = x - mu
    var = jnp.mean(xc * xc, axis=-1, keepdims=True)
    return xc * lax.rsqrt(var + LN_EPS) * g + b


def _mm_kernel(x_ref, w_ref, s_ref, *out_refs):
    acc = _dot(x_ref[...], w_ref[...]) * s_ref[...]
    for o in out_refs:
        o[...] = acc.astype(o.dtype)


def _matmul(x, w, out_dtypes, col_scale=None, tm_pref=1024, tn_pref=512):
    m, k = x.shape
    n = w.shape[1]
    tm = _tile(m, tm_pref)
    tn = _tile(n, tn_pref, LANE)
    if col_scale is None:
        col_scale = jnp.ones((1, n), f32)
    return pl.pallas_call(
        _mm_kernel,
        grid=(m // tm, n // tn),
        in_specs=[pl.BlockSpec((tm, k), lambda i, j: (i, 0)),
                  pl.BlockSpec((k, tn), lambda i, j: (0, j)),
                  pl.BlockSpec((1, tn), lambda i, j: (0, j))],
        out_specs=[pl.BlockSpec((tm, tn), lambda i, j: (i, j)) for _ in out_dtypes],
        out_shape=[jax.ShapeDtypeStruct((m, n), dt) for dt in out_dtypes],
        compiler_params=_cparams(("parallel", "arbitrary")),
        name="proj_matmul",
    )(x, w, col_scale)


def _kv_proj_kernel(x_ref, w_ref, o_ref, ob_ref, *, tm, nh):
    acc = _dot(x_ref[...], w_ref[...])
    ob_ref[...] = acc.astype(bf16)
    for h in range(nh):
        o_ref[0, pl.ds(h, tm, stride=nh), :] = acc[:, h * DH_A:(h + 1) * DH_A]


def _kv_proj(x, w, n_seq, t_out, row_off, tm_pref=512):
    m, k = x.shape
    n = w.shape[1]
    nh = n // DH_A
    tm = _tile(m // n_seq, tm_pref)
    per_seq = m // n_seq // tm
    o, ob = pl.pallas_call(
        functools.partial(_kv_proj_kernel, tm=tm, nh=nh),
        grid=(m // tm,),
        in_specs=[pl.BlockSpec((tm, k), lambda i: (i, 0)), pl.BlockSpec((k, n), lambda i: (0, 0))],
        out_specs=[pl.BlockSpec((pl.Element(1), pl.Element(tm * nh), pl.Element(DH_A)),
                                lambda i: (i // per_seq,
                                           pl.multiple_of((row_off + (i % per_seq) * tm) * nh, SUBLANE), 0)),
                   pl.BlockSpec((tm, n), lambda i: (i, 0))],
        out_shape=[jax.ShapeDtypeStruct((n_seq, t_out * nh, DH_A), f32), jax.ShapeDtypeStruct((m, n), bf16)],
        compiler_params=_cparams(("parallel",)),
        name="kv_proj",
    )(x, w)
    return o.reshape(n_seq, t_out, nh, DH_A), ob


def _outproj_ln_kernel(*refs, n_act, kc):
    acts = refs[:n_act]
    w_ref, res_ref, g_ref, b_ref, of_ref, ob_ref, acc_ref = refs[n_act:]
    k = pl.program_id(1)

    @pl.when(k == 0)
    def _():
        acc_ref[...] = jnp.zeros_like(acc_ref)

    for a in range(n_act):
        @pl.when((k >= a * kc) & (k < (a + 1) * kc))
        def _(a=a):
            acc_ref[...] += _dot(acts[a][...], w_ref[...])

    @pl.when(k == pl.num_programs(1) - 1)
    def _():
        y = _layer_norm(ALPHA * res_ref[...] + acc_ref[...], g_ref[...], b_ref[...])
        of_ref[...] = y
        ob_ref[...] = y.astype(bf16)


def _outproj_ln(acts, w, res, g, b, tm_pref=512, tk=1024):
    m, d = res.shape
    tm = _tile(m, tm_pref)
    kc = acts[0].shape[1] // tk
    n_act = len(acts)
    in_specs = [pl.BlockSpec((tm, tk), (lambda i, k, a=a: (i, jnp.clip(k - a * kc, 0, kc - 1))))
                for a in range(n_act)]
    in_specs += [pl.BlockSpec((tk, d), lambda i, k: (k, 0)),
                 pl.BlockSpec((tm, d), lambda i, k: (i, 0)),
                 pl.BlockSpec((1, d), lambda i, k: (0, 0)),
                 pl.BlockSpec((1, d), lambda i, k: (0, 0))]
    return pl.pallas_call(
        functools.partial(_outproj_ln_kernel, n_act=n_act, kc=kc),
        grid=(m // tm, n_act * kc),
        in_specs=in_specs,
        out_specs=[pl.BlockSpec((tm, d), lambda i, k: (i, 0)),
                   pl.BlockSpec((tm, d), lambda i, k: (i, 0))],
        out_shape=[jax.ShapeDtypeStruct((m, d), f32), jax.ShapeDtypeStruct((m, d), bf16)],
        scratch_shapes=[pltpu.VMEM((tm, d), f32)],
        compiler_params=_cparams(("parallel", "arbitrary")),
        name="outproj_ln",
    )(*acts, w, res, g.reshape(1, d), b.reshape(1, d))


def _ffn_ln_kernel(xb_ref, xf_ref, wg_ref, wu_ref, wd_ref, g_ref, b_ref, of_ref, ob_ref, acc_ref):
    f = pl.program_id(1)

    @pl.when(f == 0)
    def _():
        acc_ref[...] = jnp.zeros_like(acc_ref)

    x = xb_ref[...]
    gate = _dot(x, wg_ref[...])
    up = _dot(x, wu_ref[...])
    h = (gate * _sigmoid(gate) * up).astype(bf16)
    acc_ref[...] += _dot(h, wd_ref[...])

    @pl.when(f == pl.num_programs(1) - 1)
    def _():
        y = _layer_norm(ALPHA * xf_ref[...] + acc_ref[...], g_ref[...], b_ref[...])
        of_ref[...] = y
        ob_ref[...] = y.astype(bf16)


def _ffn_ln(xb, xf, wg, wu, wd, g, b, tm_pref=512, tf_pref=512):
    m, d = xf.shape
    dff = wg.shape[1]
    tm = _tile(m, tm_pref)
    tf = _tile(dff, tf_pref, LANE)
    return pl.pallas_call(
        _ffn_ln_kernel,
        grid=(m // tm, dff // tf),
        in_specs=[pl.BlockSpec((tm, d), lambda i, f: (i, 0)),
                  pl.BlockSpec((tm, d), lambda i, f: (i, 0)),
                  pl.BlockSpec((d, tf), lambda i, f: (0, f)),
                  pl.BlockSpec((d, tf), lambda i, f: (0, f)),
                  pl.BlockSpec((tf, d), lambda i, f: (f, 0)),
                  pl.BlockSpec((1, d), lambda i, f: (0, 0)),
                  pl.BlockSpec((1, d), lambda i, f: (0, 0))],
        out_specs=[pl.BlockSpec((tm, d), lambda i, f: (i, 0)),
                   pl.BlockSpec((tm, d), lambda i, f: (i, 0))],
        out_shape=[jax.ShapeDtypeStruct((m, d), f32), jax.ShapeDtypeStruct((m, d), bf16)],
        scratch_shapes=[pltpu.VMEM((tm, d), f32)],
        compiler_params=_cparams(("parallel", "arbitrary")),
        name="ffn_ln",
    )(xb, xf, wg, wu, wd, g.reshape(1, d), b.reshape(1, d))


def _cumsum_kernel(x_ref, bias_ref, carry_ref, logf_ref, cum_ref, acc_ref, *, gate):
    c = pl.program_id(1)

    @pl.when(c == 0)
    def _():
        acc_ref[...] = carry_ref[0]

    x = x_ref[0]
    lf = _log_sigmoid(x + bias_ref[...]) if gate else x
    tc = x.shape[0]
    cum = _dot_exact_rhs(_tri_lower(tc), lf) + acc_ref[...]
    logf_ref[0] = lf
    cum_ref[0] = cum
    acc_ref[...] = cum[tc - 1:tc, :]


def _logf_cumsum(x, bias, carry, gate, tc_pref=512):
    bsz, t, w = x.shape
    tc = _tile(t, tc_pref)
    spec = pl.BlockSpec((1, tc, w), lambda b, c: (b, c, 0))
    return pl.pallas_call(
        functools.partial(_cumsum_kernel, gate=gate),
        grid=(bsz, t // tc),
        in_specs=[spec,
                  pl.BlockSpec((1, w), lambda b, c: (0, 0)),
                  pl.BlockSpec((1, 1, w), lambda b, c: (b, 0, 0))],
        out_specs=[spec, spec],
        out_shape=[jax.ShapeDtypeStruct((bsz, t, w), f32)] * 2,
        scratch_shapes=[pltpu.VMEM((1, w), f32)],
        compiler_params=_cparams(("parallel", "arbitrary")),
        name="logf_cumsum",
    )(x, bias, carry)


def _attn_kernel(*refs, tq, tk, tk2, s1, s1_valid, hps):
    if s1:
        q_ref, k1_ref, v1_ref, f1_ref, k2_ref, v2_ref, f2_ref, o_ref = refs
    else:
        q_ref, k2_ref, v2_ref, f2_ref, o_ref = refs
    qi = pl.program_id(2)
    d = DH_A

    for hh in range(hps):
        cols = slice(hh * d, (hh + 1) * d)
        q = q_ref[0, :, cols]

        def scores(k, q=q):
            return _dot(q, k.astype(bf16), _NT)

        def update(carry, s, v, fk, mask):
            m, l, acc = carry
            s = s - fk * LOG2E
            if mask is not None:
                s = jnp.where(mask, s, NEG)
            m_new = jnp.maximum(m, jnp.max(s, axis=-1, keepdims=True))
            alpha = jnp.exp2(m - m_new)
            p = jnp.exp2(s - m_new)
            l = alpha * l + jnp.sum(p, axis=-1, keepdims=True)
            acc = alpha * acc + _dot(p.astype(bf16), v.astype(bf16))
            return m_new, l, acc

        def step(j, c, mask, row0=0, cols=cols, hh=hh, q=q, update=update):
            off = pl.multiple_of(j * tk, tk)
            s = _dot(q[row0:], k2_ref[0, pl.ds(off, tk), cols].astype(bf16), _NT)
            new = update(tuple(a[row0:] for a in c), s, v2_ref[0, pl.ds(off, tk), cols],
                         f2_ref[0, hh, :, pl.ds(off, tk)], mask)
            if row0 == 0:
                return new
            return tuple(jnp.concatenate([a[:row0], b], axis=0) for a, b in zip(c, new))

        carry = (jnp.full((tq, 1), NEG, f32), jnp.zeros((tq, 1), f32), jnp.zeros((tq, d), f32))
        if s1:
            pmask = None if s1_valid == s1 else (_iota2((tq, s1), 1) < s1_valid)
            carry = update(carry, scores(k1_ref[0, :, cols]), v1_ref[0, :, cols], f1_ref[0, hh], pmask)

        nd = tk2 // tk
        carry = lax.fori_loop(0, qi * nd, lambda j, c, step=step: step(j, c, None), carry)
        for r in range(nd):
            row0 = r * tk if tq == tk2 else 0
            causal = _iota2((tq - row0, tk), 0) + row0 >= _iota2((tq - row0, tk), 1) + r * tk
            carry = step(qi * nd + r, carry, causal, row0)
        m, l, acc = carry
        o_ref[0, :, cols] = (acc / l).astype(o_ref.dtype)


def _fox_attention(q, k2, v2, f2, prefix=None, s1_valid=0, tq=512, tk=512, hps=1):
    bsz, t, hd = q.shape
    t2 = k2.shape[1]
    nq = t // tq
    tk2 = t2 // nq
    w = hps * DH_A
    qspec = pl.BlockSpec((1, tq, w), lambda b, h, i: (b, i, h))
    kvspec = pl.BlockSpec((1, t2, w), lambda b, h, i: (b, 0, h))
    fspec = pl.BlockSpec((1, hps, 1, t2), lambda b, h, i: (b, h, 0, 0))
    args, in_specs = [q], [qspec]
    s1 = 0
    if prefix is not None:
        k1, v1, f1 = prefix
        s1 = k1.shape[1]
        p_spec = pl.BlockSpec((1, s1, w), lambda b, h, i: (b, 0, h))
        args += [k1, v1, f1]
        in_specs += [p_spec, p_spec, pl.BlockSpec((1, hps, 1, s1), lambda b, h, i: (b, h, 0, 0))]
    args += [k2, v2, f2]
    in_specs += [kvspec, kvspec, fspec]
    return pl.pallas_call(
        functools.partial(_attn_kernel, tq=tq, tk=min(tk, tk2), tk2=tk2, s1=s1, s1_valid=s1_valid, hps=hps),
        grid=(bsz, hd // w, nq),
        in_specs=in_specs,
        out_specs=qspec,
        out_shape=jax.ShapeDtypeStruct((bsz, t, hd), bf16),
        compiler_params=_cparams(("parallel", "parallel", "arbitrary")),
        name="fox_attention",
    )(*args)


def _attn_cache_kernel(q_ref, k1_ref, v1_ref, f1_ref, k2_ref, v2_ref, f2_ref, o_ref, m_ref, l_ref, acc_ref,
                       *, tk, nh):
    j = pl.program_id(1)
    d = DH_A
    tq = q_ref.shape[1]

    @pl.when(j == 0)
    def _():
        m_ref[...] = jnp.full(m_ref.shape, NEG, f32)
        l_ref[...] = jnp.zeros(l_ref.shape, f32)
        acc_ref[...] = jnp.zeros(acc_ref.shape, f32)

    def update(h, s, v, fk, mask):
        s = s - fk * LOG2E
        if mask is not None:
            s = jnp.where(mask, s, NEG)
        m = m_ref[h][:, 0:1]
        m_new = jnp.maximum(m, jnp.max(s, axis=-1, keepdims=True))
        alpha = jnp.exp2(m - m_new)
        p = jnp.exp2(s - m_new)
        l_ref[h] = jnp.broadcast_to(alpha * l_ref[h][:, 0:1] + jnp.sum(p, axis=-1, keepdims=True), (tq, LANE))
        acc_ref[h] = alpha * acc_ref[h] + _dot(p.astype(bf16), v)
        m_ref[h] = jnp.broadcast_to(m_new, (tq, LANE))

    qs = [q_ref[0, :, h * d:(h + 1) * d] for h in range(nh)]
    ss = [_dot(qs[h], k1_ref[0, pl.ds(h, tk, stride=nh), :].astype(bf16), _NT) for h in range(nh)]
    for h in range(nh):
        update(h, ss[h], v1_ref[0, pl.ds(h, tk, stride=nh), :].astype(bf16), f1_ref[0, h], None)

    @pl.when(j == pl.num_programs(1) - 1)
    def _():
        t2 = k2_ref.shape[1]
        causal = _iota2((tq, t2), 0) >= _iota2((tq, t2), 1)
        for h in range(nh):
            cols = slice(h * d, (h + 1) * d)
            update(h, _dot(qs[h], k2_ref[0, :, cols], _NT), v2_ref[0, :, cols], f2_ref[0, h], causal)
            o_ref[0, :, cols] = (acc_ref[h] / l_ref[h][:, 0:1]).astype(o_ref.dtype)


def _fox_attention_cache(q, k1, v1, f1, k2, v2, f2, tk=512):
    bsz, tq, hd = q.shape
    s, nh = k1.shape[1], k1.shape[2]
    t2 = k2.shape[1]
    flat = lambda a: a.reshape(bsz, s * nh, DH_A)
    qspec = pl.BlockSpec((1, tq, hd), lambda b, j: (b, 0, 0))
    cspec = pl.BlockSpec((1, tk * nh, DH_A), lambda b, j: (b, j, 0))
    nspec = pl.BlockSpec((1, t2, hd), lambda b, j: (b, 0, 0))
    return pl.pallas_call(
        functools.partial(_attn_cache_kernel, tk=tk, nh=nh),
        grid=(bsz, s // tk),
        in_specs=[qspec, cspec, cspec, pl.BlockSpec((1, nh, 1, tk), lambda b, j: (b, 0, 0, j)),
                  nspec, nspec, pl.BlockSpec((1, nh, 1, t2), lambda b, j: (b, 0, 0, 0))],
        out_specs=qspec,
        out_shape=jax.ShapeDtypeStruct((bsz, tq, hd), bf16),
        scratch_shapes=[pltpu.VMEM((nh, tq, LANE), f32), pltpu.VMEM((nh, tq, LANE), f32),
                        pltpu.VMEM((nh, tq, DH_A), f32)],
        compiler_params=_cparams(("parallel", "arbitrary")),
        name="fox_attention_cache",
    )(q, flat(k1), flat(v1), f1, k2, v2, f2)


def _forget_columns(cum, h):
    r, c = _iota2((LANE, LANE), 0), _iota2((LANE, LANE), 1)
    out = None
    for i, piece in enumerate(_split3(cum * LOG2E)):
        t = _dot(piece, ((r == h) & (c == i)).astype(bf16))
        out = t if out is None else out + t
    return out.astype(bf16)


def _attn_t_kernel(q_ref, k1_ref, v1_ref, c1_ref, k2_ref, v2_ref, c2_ref, o_ref, kaug_ref, *, tq, s1_valid, hps):
    hg = pl.program_id(1)
    qi = pl.program_id(2)
    d = DH_A

    @pl.when(qi == 0)
    def _():
        for hh in range(hps):
            kaug_ref[hh, :, 0:d] = k2_ref[0, :, hh * d:(hh + 1) * d]
            kaug_ref[hh, :, d:2 * d] = _forget_columns(c2_ref[0], hg * hps + hh)

    minus_one = jnp.where(_iota2((tq, d), 1) < 3, -1.0, 0.0).astype(bf16)
    q_aug = [jnp.concatenate([q_ref[0, :, hh * d:(hh + 1) * d], minus_one], axis=-1) for hh in range(hps)]

    def update(carry, s, v, mask):
        m, l, acc = carry
        if mask is not None:
            s = jnp.where(mask, s, NEG)
        m_new = jnp.maximum(m, jnp.max(s, axis=0, keepdims=True))
        alpha = jnp.exp2(m - m_new)
        p = jnp.exp2(s - m_new)
        l = alpha * l + jnp.sum(p, axis=0, keepdims=True)
        acc = alpha * acc + _dot(v, p.astype(bf16), _TN)
        return m_new, l, acc

    def multi(carries, k_augs, vs, mask):
        ss = [_dot(k_augs[hh], q_aug[hh], _NT) for hh in range(hps)]
        return [update(carries[hh], ss[hh], vs[hh], mask) for hh in range(hps)]

    carries = [(jnp.full((1, tq), NEG, f32), jnp.zeros((1, tq), f32), jnp.zeros((d, tq), f32))
               for _ in range(hps)]
    s1 = k1_ref.shape[1]
    k1_aug = [jnp.concatenate([k1_ref[0, :, hh * d:(hh + 1) * d], _forget_columns(c1_ref[0], hg * hps + hh)], axis=-1)
              for hh in range(hps)]
    carries = multi(carries, k1_aug, [v1_ref[0, :, hh * d:(hh + 1) * d] for hh in range(hps)],
                    _iota2((s1, tq), 0) < s1_valid)

    def blk(j, c, mask):
        off = pl.multiple_of(j * tq, tq)
        return multi(c, [kaug_ref[hh, pl.ds(off, tq), :] for hh in range(hps)],
                     [v2_ref[0, pl.ds(off, tq), hh * d:(hh + 1) * d] for hh in range(hps)], mask)

    carries = lax.fori_loop(0, qi, lambda j, c: blk(j, c, None), carries)
    carries = blk(qi, carries, _iota2((tq, tq), 0) <= _iota2((tq, tq), 1))
    for hh in range(hps):
        m, l, acc = carries[hh]
        o_ref[0, :, hh * d:(hh + 1) * d] = jnp.transpose(acc / l).astype(o_ref.dtype)


def _fox_attention_main(q, k1, v1, c1, k2, v2, c2, s1_valid, tq=512, hps=4):
    bsz, t, hd = q.shape
    s1 = k1.shape[1]
    w = hps * DH_A
    qspec = pl.BlockSpec((1, tq, w), lambda b, h, i: (b, i, h))
    kvspec = pl.BlockSpec((1, t, w), lambda b, h, i: (b, 0, h))
    pspec = pl.BlockSpec((1, s1, w), lambda b, h, i: (b, 0, h))
    return pl.pallas_call(
        functools.partial(_attn_t_kernel, tq=tq, s1_valid=s1_valid, hps=hps),
        grid=(bsz, hd // w, t // tq),
        in_specs=[qspec, pspec, pspec, pl.BlockSpec((1, s1, LANE), lambda b, h, i: (b, 0, 0)),
                  kvspec, kvspec, pl.BlockSpec((1, t, LANE), lambda b, h, i: (b, 0, 0))],
        out_specs=qspec,
        out_shape=jax.ShapeDtypeStruct((bsz, t, hd), bf16),
        scratch_shapes=[pltpu.VMEM((hps, t, 2 * DH_A), bf16)],
        compiler_params=_cparams(("parallel", "parallel", "arbitrary")),
        name="fox_attention_main",
    )(q, k1, v1, c1, k2, v2, c2)


def _ssd_kernel(zx_ref, dt_ref, hist_ref, h0_ref, cw_ref, cb_ref, dtb_ref, alog_ref, alogc_ref,
                dskip_ref, ng_ref, e_ref, y_ref, tail_ref, hout_ref, ext_ref, st_ref, *, lc, t_valid):
    c = pl.program_id(1)

    @pl.when(c == 0)
    def _():
        ext_ref[0:SUBLANE, :] = hist_ref[0]
        st_ref[...] = h0_ref[0]

    ext_ref[SUBLANE:SUBLANE + lc, :] = zx_ref[0, :, D_SSM:]
    conv = cb_ref[...]
    for i in range(CONV_W):
        lo = SUBLANE - (CONV_W - 1) + i
        conv = conv + cw_ref[i:i + 1, :] * ext_ref[lo:lo + lc, :]
    xc = conv * _sigmoid(conv)
    tail = ext_ref[t_valid:t_valid + SUBLANE, :]
    ext_ref[0:SUBLANE, :] = tail

    dt = _softplus(dt_ref[0] + dtb_ref[...])
    if t_valid < lc:
        dt = jnp.where(_iota2((lc, LANE), 0) < t_valid, dt, 0.0)
    a_row = -jnp.exp(alog_ref[...])
    a_col = -jnp.exp(alogc_ref[...])
    a_cum = _dot_exact_rhs(_tri_lower(lc), dt * a_row)
    dt_t = _dot_exact_rhs(_eye(H_B, LANE), dt, _NT)
    a_cum_t = _dot_exact_lhs(dt_t * a_col, _tri_upper(lc))
    a_last = a_cum[lc - 1:lc, :]
    e = e_ref[...]
    to_end_x = _dot_exact_lhs(jnp.exp(a_last - a_cum) * dt, e)
    ea_x = _dot_exact_lhs(jnp.exp(a_cum), e)
    cdec_x = _dot_exact_lhs(jnp.broadcast_to(jnp.exp(a_last), (SUBLANE, LANE)), e)[0:1, :]

    causal = _iota2((lc, lc), 0) >= _iota2((lc, lc), 1)
    left = _iota2((lc, LANE), 1) < P_B
    eye_n = _eye(N_B, N_B)
    gw = D_SSM // G_B
    groups = range(G_B)
    gcols = [slice(g * gw, (g + 1) * gw) for g in groups]
    bg = [xc[:, D_SSM + g * N_B:D_SSM + (g + 1) * N_B].astype(bf16) for g in groups]
    cg = [xc[:, D_SSM + G_B * N_B + g * N_B:D_SSM + G_B * N_B + (g + 1) * N_B].astype(bf16) for g in groups]
    cb = [_dot(cg[g], bg[g], _NT) for g in groups]
    bg_t = [_dot(eye_n, bg[g], _NT).astype(bf16) for g in groups]
    st_in = [st_ref[:, gcols[g]] for g in groups]
    y_off = [_dot(cg[g], st_in[g].astype(bf16)) * ea_x[:, gcols[g]] for g in groups]
    upd = [_dot(bg_t[g], (xc[:, gcols[g]] * to_end_x[:, gcols[g]]).astype(bf16)) for g in groups]
    for g in groups:
        st_ref[:, gcols[g]] = cdec_x[:, gcols[g]] * st_in[g] + upd[g]

    def decay_weights(h):
        seg = a_cum[:, h:h + 1] - a_cum_t[h:h + 1, :]
        dec = jnp.exp(jnp.where(causal, seg, NEG))
        return (cb[h // HG_B] * dec * dt_t[h:h + 1, :]).astype(bf16)

    w_all = [decay_weights(h) for h in range(H_B)]
    y_pair = []
    for j in range(D_SSM // LANE):
        xp = xc[:, j * LANE:(j + 1) * LANE].astype(bf16)
        y_pair.append(jnp.where(left, _dot(w_all[2 * j], xp), _dot(w_all[2 * j + 1], xp)))
    ppg = gw // LANE
    for g in groups:
        xg = xc[:, gcols[g]]
        y = jnp.concatenate(y_pair[g * ppg:(g + 1) * ppg], axis=-1) + y_off[g] + dskip_ref[:, gcols[g]] * xg
        z = zx_ref[0, :, g * gw:(g + 1) * gw]
        y = y * (z * _sigmoid(z))
        y = y * lax.rsqrt(jnp.mean(y * y, axis=-1, keepdims=True) + LN_EPS) * ng_ref[:, g * gw:(g + 1) * gw]
        y_ref[0, :, g * gw:(g + 1) * gw] = y.astype(y_ref.dtype)

    @pl.when(c == pl.num_programs(1) - 1)
    def _():
        tail_ref[0] = tail
        hout_ref[0] = st_ref[...]


def _ssd_mixer(zx, dt_raw, hist, h0_t, p, lc, t_valid):
    bsz, t, wz = zx.shape
    const = lambda shape: pl.BlockSpec(shape, lambda b, c: (0,) * len(shape))
    return pl.pallas_call(
        functools.partial(_ssd_kernel, lc=lc, t_valid=t_valid),
        grid=(bsz, t // lc),
        in_specs=[pl.BlockSpec((1, lc, wz), lambda b, c: (b, c, 0)),
                  pl.BlockSpec((1, lc, LANE), lambda b, c: (b, c, 0)),
                  pl.BlockSpec((1, SUBLANE, CONV_DIM), lambda b, c: (b, 0, 0)),
                  pl.BlockSpec((1, N_B, D_SSM), lambda b, c: (b, 0, 0)),
                  const((SUBLANE, CONV_DIM)), const((1, CONV_DIM)), const((1, LANE)), const((1, LANE)),
                  const((H_B, 1)), const((1, D_SSM)), const((1, D_SSM)), const((LANE, D_SSM))],
        out_specs=[pl.BlockSpec((1, lc, D_SSM), lambda b, c: (b, c, 0)),
                   pl.BlockSpec((1, SUBLANE, CONV_DIM), lambda b, c: (b, 0, 0)),
                   pl.BlockSpec((1, N_B, D_SSM), lambda b, c: (b, 0, 0))],
        out_shape=[jax.ShapeDtypeStruct((bsz, t, D_SSM), bf16),
                   jax.ShapeDtypeStruct((bsz, SUBLANE, CONV_DIM), f32),
                   jax.ShapeDtypeStruct((bsz, N_B, D_SSM), f32)],
        scratch_shapes=[pltpu.VMEM((SUBLANE + lc, CONV_DIM), f32), pltpu.VMEM((N_B, D_SSM), f32)],
        compiler_params=_cparams(("parallel", "arbitrary")),
        name="ssd_mixer",
    )(zx, dt_raw, hist, h0_t, p["conv_w"], p["conv_b"], p["dt_bias"], p["a_log_row"], p["a_log_col"],
      p["d_skip"], p["norm_g"], p["expand"])


def _mlstm_kernel(qkv_ref, o_ref, gates_ref, gb_ref, ng_ref, c0_ref, n0_ref, m0_ref,
                  y_ref, cout_ref, nout_ref, mout_ref, ct_ref, n_ref, m_ref, *, lc, t_valid):
    c = pl.program_id(1)

    @pl.when(c == 0)
    def _():
        ct_ref[...] = c0_ref[0]
        n_ref[...] = n0_ref[0]
        m_ref[...] = m0_ref[0]

    lane = _iota2((lc, LANE), 1)
    g = gates_ref[0] + gb_ref[...]
    is_f = (lane >= H_C) & (lane < 2 * H_C)
    lf = jnp.where(is_f, _log_sigmoid(g), 0.0)
    li = g
    if t_valid < lc:
        valid = _iota2((lc, LANE), 0) < t_valid
        lf = jnp.where(valid, lf, 0.0)
        li = jnp.where(valid, li, NEG)
    b_cum = _dot_exact_rhs(_tri_lower(lc), lf)
    rows = 2 * H_C
    li_t = _dot_exact_rhs(_eye(rows, LANE), li, _NT)
    lf_t = _dot_exact_rhs(_eye(rows, LANE), lf, _NT)
    b_cum_t = _dot_exact_lhs(lf_t, _tri_upper(lc))

    causal = _iota2((lc, lc), 0) >= _iota2((lc, lc), 1)
    eye_k = _eye(DK_C, DK_C)
    m_all = m_ref[...]
    m_out = m_all
    q_of = lambda h: qkv_ref[0, :, h * DK_C:(h + 1) * DK_C]
    k_of = lambda h: qkv_ref[0, :, D_QK_C + h * DK_C:D_QK_C + (h + 1) * DK_C]
    v_of = lambda h: qkv_ref[0, :, 2 * D_QK_C + h * DV_C:2 * D_QK_C + (h + 1) * DV_C]
    heads = range(H_C)
    qk_all = [_dot(q_of(h), k_of(h), _NT) for h in heads]
    qc_all = [_dot(q_of(h), ct_ref[h].astype(bf16)) for h in heads]
    kt_all = [_dot(eye_k, k_of(h), _NT) for h in heads]
    bcol = [b_cum[:, H_C + h:H_C + h + 1] for h in heads]
    brow = [b_cum_t[H_C + h:H_C + h + 1, :] for h in heads]
    lirow = [li_t[h:h + 1, :] for h in heads]
    m_st = [m_all[:, h:h + 1] for h in heads]
    d_mat = [jnp.where(causal, bcol[h] - brow[h] + lirow[h], NEG) for h in heads]
    inter = [bcol[h] + m_st[h] for h in heads]
    m_row = [jnp.maximum(inter[h], jnp.max(d_mat[h], axis=-1, keepdims=True)) for h in heads]
    w = [jnp.exp(d_mat[h] - m_row[h]) * qk_all[h] for h in heads]
    g_inter = [jnp.exp(inter[h] - m_row[h]) for h in heads]
    n_st = [n_ref[h:h + 1, :] for h in heads]
    num = [_dot(w[h].astype(bf16), v_of(h)) + g_inter[h] * qc_all[h] for h in heads]
    den = [jnp.sum(w[h], axis=-1, keepdims=True)
           + g_inter[h] * jnp.sum(q_of(h).astype(f32) * n_st[h], axis=-1, keepdims=True) for h in heads]
    hh = [num[h] / jnp.maximum(jnp.abs(den[h]), jnp.exp(-m_row[h])) for h in heads]
    ms = [jnp.mean(hh[h] * hh[h], axis=-1, keepdims=True) for h in heads]
    for h in heads:
        cols = slice(h * DV_C, (h + 1) * DV_C)
        y = hh[h] * lax.rsqrt(ms[h] + LN_EPS) * ng_ref[:, cols] * _sigmoid(o_ref[0, :, cols])
        y_ref[0, :, cols] = y.astype(y_ref.dtype)
    b_tot = [bcol[h][lc - 1:lc, :] for h in heads]
    d_end = [b_tot[h] - brow[h] + lirow[h] for h in heads]
    m_new = [jnp.maximum(b_tot[h] + m_st[h], jnp.max(d_end[h], axis=-1, keepdims=True)) for h in heads]
    w_end = [jnp.exp(d_end[h] - m_new[h]) for h in heads]
    g_old = [jnp.exp(b_tot[h] + m_st[h] - m_new[h]) for h in heads]
    for h in heads:
        ct_ref[h] = g_old[h] * ct_ref[h] + _dot((kt_all[h] * w_end[h]).astype(bf16), v_of(h))
        n_upd = _dot(jnp.broadcast_to(w_end[h], (SUBLANE, lc)).astype(bf16), k_of(h))[0:1, :]
        n_ref[h:h + 1, :] = g_old[h] * n_st[h] + n_upd
        m_out = jnp.where(lane[0:1, :] == h, m_new[h], m_out)
    m_ref[...] = m_out

    @pl.when(c == pl.num_programs(1) - 1)
    def _():
        cout_ref[0] = ct_ref[...]
        nout_ref[0] = n_ref[...]
        mout_ref[0] = m_ref[...]


def _mlstm_mixer(qkv, o, gates, c0_t, n0, m0, p, lc, t_valid):
    bsz, t, wq = qkv.shape
    const = lambda shape: pl.BlockSpec(shape, lambda b, c: (0,) * len(shape))
    return pl.pallas_call(
        functools.partial(_mlstm_kernel, lc=lc, t_valid=t_valid),
        grid=(bsz, t // lc),
        in_specs=[pl.BlockSpec((1, lc, wq), lambda b, c: (b, c, 0)),
                  pl.BlockSpec((1, lc, D_V_C), lambda b, c: (b, c, 0)),
                  pl.BlockSpec((1, lc, LANE), lambda b, c: (b, c, 0)),
                  const((1, LANE)), const((1, D_V_C)),
                  pl.BlockSpec((1, H_C, DK_C, DV_C), lambda b, c: (b, 0, 0, 0)),
                  pl.BlockSpec((1, H_C, DK_C), lambda b, c: (b, 0, 0)),
                  pl.BlockSpec((1, 1, LANE), lambda b, c: (b, 0, 0))],
        out_specs=[pl.BlockSpec((1, lc, D_V_C), lambda b, c: (b, c, 0)),
                   pl.BlockSpec((1, H_C, DK_C, DV_C), lambda b, c: (b, 0, 0, 0)),
                   pl.BlockSpec((1, H_C, DK_C), lambda b, c: (b, 0, 0)),
                   pl.BlockSpec((1, 1, LANE), lambda b, c: (b, 0, 0))],
        out_shape=[jax.ShapeDtypeStruct((bsz, t, D_V_C), bf16),
                   jax.ShapeDtypeStruct((bsz, H_C, DK_C, DV_C), f32),
                   jax.ShapeDtypeStruct((bsz, H_C, DK_C), f32),
                   jax.ShapeDtypeStruct((bsz, 1, LANE), f32)],
        scratch_shapes=[pltpu.VMEM((H_C, DK_C, DV_C), f32), pltpu.VMEM((H_C, DK_C), f32),
                        pltpu.VMEM((1, LANE), f32)],
        compiler_params=_cparams(("parallel", "arbitrary")),
        name="mlstm_mixer",
    )(qkv, o, gates, p["gate_bias"], p["norm_g"], c0_t, n0, m0)


def _pad_lanes(a, width=LANE):
    return jnp.pad(a, [(0, 0)] * (a.ndim - 1) + [(0, width - a.shape[-1])])


def _pad_seq(a, t_pad=SEQ_PAD):
    return jnp.pad(a, [(0, 0), (0, t_pad - a.shape[1])] + [(0, 0)] * (a.ndim - 2))


def _even_layer(xm_b, xm_f, xa_b, xa_f, n_p, t_main, cache, w):
    k_past, v_past, logf_past, conv_past, h_past = cache
    n_s = k_past.shape[0]
    n_aux = n_p + n_s
    t_aux = xa_f.shape[0] // n_aux
    past = k_past.shape[1]

    def project(xb, main):
        q, = _matmul(xb, w["w_q"], [bf16], col_scale=jnp.full((1, D_A), DH_A ** -0.5 * LOG2E, f32))
        if main:
            k_f, k_b = _kv_proj(xb, w["w_k"], n_p, t_aux + t_main, t_aux)
            v_f, v_b = _kv_proj(xb, w["w_v"], n_p, t_aux + t_main, t_aux)
        else:
            k_f, k_b = _matmul(xb, w["w_k"], [f32, bf16])
            v_f, v_b = _matmul(xb, w["w_v"], [f32, bf16])
        zx, = _matmul(xb, w["w_zx"], [f32])
        sm, = _matmul(xb, w["w_small"], [f32], tn_pref=2 * LANE)
        return q, k_f, k_b, v_f, v_b, zx, sm

    qm, kp_f, km_b, vp_f, vm_b, zxm, smm = project(xm_b, True)
    qa, ka_f, ka_b, va_f, va_b, zxa, sma = project(xa_b, False)

    seq = lambda a, n: a.reshape(n, a.shape[0] // n, a.shape[1])
    heads = lambda a: a.reshape(a.shape[0], a.shape[1], H_A, DH_A)
    qm, km_b, vm_b, zxm, smm = [seq(a, n_p) for a in (qm, km_b, vm_b, zxm, smm)]
    qa, ka_f, va_f = [seq(a, n_aux) for a in (qa, ka_f, va_f)]
    ka_b, va_b, zxa, sma = [_pad_seq(seq(a, n_aux)) for a in (ka_b, va_b, zxa, sma)]
    ka_f, va_f = heads(ka_f), heads(va_f)
    kp_f = kp_f.at[:, :t_aux].set(ka_f[:n_p])
    vp_f = vp_f.at[:, :t_aux].set(va_f[:n_p])

    zero_c = jnp.zeros((n_s, 1, LANE), f32)
    _, cum_past = _logf_cumsum(_pad_lanes(logf_past.astype(f32)), w["b_f"], zero_c, gate=False)
    carry_a = jnp.concatenate([jnp.zeros((n_p, 1, LANE), f32), cum_past[:, past - 1:past]], axis=0)
    logf_a, cum_a = _logf_cumsum(sma[:, :, LANE:], w["b_f"], carry_a, gate=True)
    logf_m, cum_m = _logf_cumsum(smm[:, :, LANE:], w["b_f"], cum_a[:n_p, t_aux - 1:t_aux], gate=True)
    rows = lambda cum: jnp.swapaxes(cum[:, :, :H_A], 1, 2)[:, :, None, :]

    fa = rows(cum_a)
    att_meta = _fox_attention(qa[:n_p], ka_b[:n_p], va_b[:n_p], fa[:n_p], tq=t_aux, hps=2)
    att_s = _fox_attention_cache(qa[n_p:], k_past, v_past, rows(cum_past), ka_b[n_p:], va_b[n_p:], fa[n_p:],
                                 tk=min(past, 512))
    att_m = _fox_attention_main(qm, ka_b[:n_p], va_b[:n_p], cum_a[:n_p], km_b, vm_b, cum_m, t_aux,
                                tq=min(t_main, 512))
    att_a = jnp.concatenate([att_meta, att_s], axis=0)

    hist_a = jnp.concatenate([jnp.zeros((n_p, SUBLANE, CONV_DIM), f32),
                              jnp.pad(conv_past.astype(f32), ((0, 0), (SUBLANE - (CONV_W - 1), 0), (0, 0)))], axis=0)
    to_t = lambda h: jnp.transpose(h.reshape(-1, D_SSM, N_B), (0, 2, 1))
    from_t = lambda h: jnp.transpose(h, (0, 2, 1)).reshape(-1, H_B, P_B, N_B)
    h0_a = jnp.concatenate([jnp.zeros((n_p, N_B, D_SSM), f32), to_t(h_past.astype(f32))], axis=0)
    ya, tail_a, h_a = _ssd_mixer(zxa, sma[:, :, :LANE], hist_a, h0_a, w, SEQ_PAD, t_aux)
    ym, tail_m, h_m = _ssd_mixer(zxm, smm[:, :, :LANE], tail_a[:n_p], h_a[:n_p], w, min(t_main, 128), min(t_main, 128))

    unseq = lambda a: a.reshape(-1, a.shape[-1])
    x1m_f, x1m_b = _outproj_ln([unseq(att_m), unseq(ym)], w["w_out"], xm_f, w["ln_g"], w["ln_b"])
    x1a_f, x1a_b = _outproj_ln([unseq(att_a), unseq(ya[:, :t_aux])], w["w_out"], xa_f, w["ln_g"], w["ln_b"])

    new = dict(
        k_p=kp_f, v_p=vp_f,
        f_p=jnp.concatenate([logf_a[:n_p, :t_aux, :H_A], logf_m[:, :, :H_A]], axis=1),
        conv_p=tail_m[:, SUBLANE - (CONV_W - 1):], h_p=from_t(h_m),
        k_s=ka_f[n_p:], v_s=va_f[n_p:], f_s=logf_a[n_p:, :t_aux, :H_A],
        conv_s=tail_a[n_p:, SUBLANE - (CONV_W - 1):], h_s=from_t(h_a[n_p:]))
    return (x1m_f, x1m_b, x1a_f, x1a_b), new


def _odd_layer(xm_b, xm_f, xa_b, xa_f, n_p, t_main, cache, w):
    c_past, n_past, m_past = cache
    n_s = c_past.shape[0]
    n_aux = n_p + n_s
    t_aux = xa_f.shape[0] // n_aux

    def project(xb):
        qkv, = _matmul(xb, w["w_qkv"], [bf16], col_scale=w["qkv_scale"])
        o, = _matmul(xb, w["w_o"], [f32])
        gt, = _matmul(xb, w["w_gates"], [f32])
        return qkv, o, gt

    qkvm, om, gm = project(xm_b)
    qkva, oa, ga = project(xa_b)
    seq = lambda a, n: a.reshape(n, a.shape[0] // n, a.shape[1])
    qkvm, om, gm = [seq(a, n_p) for a in (qkvm, om, gm)]
    qkva, oa, ga = [_pad_seq(seq(a, n_aux)) for a in (qkva, oa, ga)]

    c0 = jnp.concatenate([jnp.zeros((n_p, H_C, DK_C, DV_C), f32), jnp.swapaxes(c_past.astype(f32), 2, 3)], axis=0)
    n0 = jnp.concatenate([jnp.zeros((n_p, H_C, DK_C), f32), n_past.astype(f32)], axis=0)
    m0 = jnp.concatenate([jnp.zeros((n_p, 1, LANE), f32), _pad_lanes(m_past.astype(f32))[:, None, :]], axis=0)
    ha, c_a, n_a, m_a = _mlstm_mixer(qkva, oa, ga, c0, n0, m0, w, SEQ_PAD, t_aux)
    lc = min(t_main, 256)
    hm, c_m, n_m, m_m = _mlstm_mixer(qkvm, om, gm, c_a[:n_p], n_a[:n_p], m_a[:n_p], w, lc, lc)

    unseq = lambda a: a.reshape(-1, a.shape[-1])
    x1m_f, x1m_b = _outproj_ln([unseq(hm)], w["w_out"], xm_f, w["ln_g"], w["ln_b"])
    x1a_f, x1a_b = _outproj_ln([unseq(ha[:, :t_aux])], w["w_out"], xa_f, w["ln_g"], w["ln_b"])
    new = dict(c_p=jnp.swapaxes(c_m, 2, 3), n_p=n_m, m_p=m_m[:, 0, :H_C],
               c_s=jnp.swapaxes(c_a[n_p:], 2, 3), n_s=n_a[n_p:], m_s=m_a[n_p:, 0, :H_C])
    return (x1m_f, x1m_b, x1a_f, x1a_b), new


def _even_weights(e, w_in_a, b_fgate_a, conv_w, conv_b, dt_bias, a_log, d_skip, ssd_norm_g, w_out_a):
    wi = w_in_a[e]
    o_f = 3 * D_A
    o_z = o_f + H_A
    o_dt = o_z + D_SSM + CONV_DIM
    w_small = jnp.concatenate([_pad_lanes(wi[:, o_dt:o_dt + H_B]), _pad_lanes(wi[:, o_f:o_f + H_A])], axis=1)
    expand = (np.arange(LANE)[:, None] == (np.arange(D_SSM) // P_B)[None, :])
    return dict(
        w_q=wi[:, :D_A].astype(bf16), w_k=wi[:, D_A:2 * D_A].astype(bf16),
        w_v=wi[:, 2 * D_A:3 * D_A].astype(bf16),
        w_zx=wi[:, o_z:o_dt].astype(bf16), w_small=w_small.astype(bf16),
        b_f=_pad_lanes(b_fgate_a[e][None, :]),
        conv_w=jnp.pad(conv_w[e], ((0, SUBLANE - CONV_W), (0, 0))), conv_b=conv_b[e][None, :],
        dt_bias=_pad_lanes(dt_bias[e][None, :]), a_log_row=_pad_lanes(a_log[e][None, :]),
        a_log_col=a_log[e][:, None], d_skip=jnp.repeat(d_skip[e], P_B)[None, :],
        norm_g=ssd_norm_g[e][None, :], expand=jnp.asarray(expand, bf16),
        w_out=w_out_a[e].astype(bf16))


def _odd_weights(o, w_in_c, b_igate_c, b_fgate_c, mlstm_norm_g, w_out_c):
    wi = w_in_c[o]
    o_o = 2 * D_QK_C + D_V_C
    o_g = o_o + D_V_C
    scale = jnp.concatenate([jnp.ones((D_QK_C,), f32), jnp.full((D_QK_C,), DK_C ** -0.5, f32),
                             jnp.ones((D_V_C,), f32)])[None, :]
    return dict(
        w_qkv=wi[:, :o_o].astype(bf16), qkv_scale=scale, w_o=wi[:, o_o:o_g].astype(bf16),
        w_gates=_pad_lanes(wi[:, o_g:]).astype(bf16),
        gate_bias=_pad_lanes(jnp.concatenate([b_igate_c[o], b_fgate_c[o]])[None, :]),
        norm_g=mlstm_norm_g[o][None, :], w_out=w_out_c[o].astype(bf16))


def kernel(x_prompt, x_sample, cache_fox_k, cache_fox_v, cache_fox_logf, state_ssd_conv, state_ssd, state_mlstm_c, state_mlstm_n, state_mlstm_m, meta_tokens, w_in_a, b_fgate_a, conv_w, conv_b, dt_bias, a_log, d_skip, ssd_norm_g, w_out_a, w_in_c, b_igate_c, b_fgate_c, mlstm_norm_g, w_out_c, ln_mix_g, ln_mix_b, ln_ffn_g, ln_ffn_b, w_ffn_gate, w_ffn_up, w_ffn_down):
    n_p, t_main, d = x_prompt.shape
    n_s, t_s, _ = x_sample.shape
    assert t_s == N_META, "aux rows hold equal-length meta and running-stream sequences"
    xm_f = x_prompt.reshape(n_p * t_main, d)
    xa_f = jnp.concatenate([jnp.broadcast_to(meta_tokens.astype(x_prompt.dtype)[None], (n_p, N_META, d)),
                            x_sample], axis=0).reshape((n_p + n_s) * t_s, d)
    xm_b, xa_b = xm_f.astype(bf16), xa_f.astype(bf16)

    even_out, odd_out = [], []
    for layer in range(DEPTH):
        if layer % 2 == 0:
            e = layer // 2
            w = _even_weights(e, w_in_a, b_fgate_a, conv_w, conv_b, dt_bias, a_log, d_skip, ssd_norm_g, w_out_a)
            cache = (cache_fox_k[e], cache_fox_v[e], cache_fox_logf[e], state_ssd_conv[e], state_ssd[e])
            w["ln_g"], w["ln_b"] = ln_mix_g[layer], ln_mix_b[layer]
            (x1m_f, x1m_b, x1a_f, x1a_b), new = _even_layer(xm_b, xm_f, xa_b, xa_f, n_p, t_main, cache, w)
            even_out.append(new)
        else:
            o = layer // 2
            w = _odd_weights(o, w_in_c, b_igate_c, b_fgate_c, mlstm_norm_g, w_out_c)
            cache = (state_mlstm_c[o], state_mlstm_n[o], state_mlstm_m[o])
            w["ln_g"], w["ln_b"] = ln_mix_g[layer], ln_mix_b[layer]
            (x1m_f, x1m_b, x1a_f, x1a_b), new = _odd_layer(xm_b, xm_f, xa_b, xa_f, n_p, t_main, cache, w)
            odd_out.append(new)
        wg, wu, wd = w_ffn_gate[layer].astype(bf16), w_ffn_up[layer].astype(bf16), w_ffn_down[layer].astype(bf16)
        xm_f, xm_b = _ffn_ln(x1m_b, x1m_f, wg, wu, wd, ln_ffn_g[layer], ln_ffn_b[layer])
        xa_f, xa_b = _ffn_ln(x1a_b, x1a_f, wg, wu, wd, ln_ffn_g[layer], ln_ffn_b[layer])

    y_prompt = xm_f.reshape(n_p, t_main, d)
    y_sample = xa_f.reshape(n_p + n_s, t_s, d)[n_p:]
    st = lambda outs, key: jnp.stack([o[key] for o in outs])
    return (y_prompt, y_sample,
            st(even_out, "k_p"), st(even_out, "v_p"), st(even_out, "f_p"), st(even_out, "conv_p"), st(even_out, "h_p"),
            st(odd_out, "c_p"), st(odd_out, "n_p"), st(odd_out, "m_p"),
            st(even_out, "k_s"), st(even_out, "v_s"), st(even_out, "f_s"), st(even_out, "conv_s"), st(even_out, "h_s"),
            st(odd_out, "c_s"), st(odd_out, "n_s"), st(odd_out, "m_s"))
```

```python
import functools

import jax
import jax.numpy as jnp
import numpy as np
from jax import lax
from jax.experimental import pallas as pl
from jax.experimental.pallas import tpu as pltpu

f32 = jnp.float32
bf16 = jnp.bfloat16

D_MODEL = 2048
DEPTH = 2
N_META = 16
H_A, DH_A = 16, 128
D_A = H_A * DH_A
H_B, P_B, G_B, N_B = 32, 64, 4, 128
HG_B = H_B // G_B
D_SSM = H_B * P_B
CONV_W = 4
CONV_DIM = D_SSM + 2 * G_B * N_B
H_C, DK_C, DV_C = 8, 128, 256
D_QK_C = H_C * DK_C
D_V_C = H_C * DV_C
ALPHA = (2 * DEPTH) ** 0.25
NEG = -1e30
LN_EPS = 1e-5
LOG2E = 1.4426950408889634

LANE = 128
SUBLANE = 8
SEQ_PAD = 128
VMEM_LIMIT = 56 * 1024 * 1024


def _cparams(sem):
    return pltpu.CompilerParams(dimension_semantics=sem, vmem_limit_bytes=VMEM_LIMIT)


def _tile(n, pref, mult=16):
    if n <= pref:
        return n
    t = (pref // mult) * mult
    while t >= mult:
        if n % t == 0:
            return t
        t -= mult
    return n


def _softplus(x):
    return jnp.maximum(x, 0.0) + jnp.log1p(jnp.exp(-jnp.abs(x)))


def _log_sigmoid(x):
    return -_softplus(-x)


def _sigmoid(x):
    return 1.0 / (1.0 + jnp.exp(-x))


def _split3(a):
    a1 = a.astype(bf16)
    r1 = a - a1.astype(f32)
    a2 = r1.astype(bf16)
    a3 = (r1 - a2.astype(f32)).astype(bf16)
    return a1, a2, a3


_NN = (((1,), (0,)), ((), ()))
_NT = (((1,), (1,)), ((), ()))
_TN = (((0,), (0,)), ((), ()))


def _dot(a, b, dims=_NN):
    return lax.dot_general(a, b, dims, preferred_element_type=f32)


def _dot_exact_rhs(a_bf16_exact, b_f32, dims=_NN):
    b1, b2, b3 = _split3(b_f32)
    return _dot(a_bf16_exact, b1, dims) + _dot(a_bf16_exact, b2, dims) + _dot(a_bf16_exact, b3, dims)


def _dot_exact_lhs(a_f32, b_bf16_exact, dims=_NN):
    a1, a2, a3 = _split3(a_f32)
    return _dot(a1, b_bf16_exact, dims) + _dot(a2, b_bf16_exact, dims) + _dot(a3, b_bf16_exact, dims)


def _iota2(shape, axis):
    return lax.broadcasted_iota(jnp.int32, shape, axis)


def _tri_lower(n):
    return (_iota2((n, n), 0) >= _iota2((n, n), 1)).astype(bf16)


def _tri_upper(n):
    return (_iota2((n, n), 0) <= _iota2((n, n), 1)).astype(bf16)


def _eye(rows, cols):
    return (_iota2((rows, cols), 0) == _iota2((rows, cols), 1)).astype(bf16)


def _layer_norm(x, g, b):
    mu = jnp.mean(x, axis=-1, keepdims=True)
    xc = x - mu
    var = jnp.mean(xc * xc, axis=-1, keepdims=True)
    return xc * lax.rsqrt(var + LN_EPS) * g + b


def _mm_kernel(x_ref, w_ref, s_ref, *out_refs):
    acc = _dot(x_ref[...], w_ref[...]) * s_ref[...]
    for o in out_refs:
        o[...] = acc.astype(o.dtype)


def _matmul(x, w, out_dtypes, col_scale=None, tm_pref=1024, tn_pref=512):
    m, k = x.shape
    n = w.shape[1]
    tm = _tile(m, tm_pref)
    tn = _tile(n, tn_pref, LANE)
    if col_scale is None:
        col_scale = jnp.ones((1, n), f32)
    return pl.pallas_call(
        _mm_kernel,
        grid=(m // tm, n // tn),
        in_specs=[pl.BlockSpec((tm, k), lambda i, j: (i, 0)),
                  pl.BlockSpec((k, tn), lambda i, j: (0, j)),
                  pl.BlockSpec((1, tn), lambda i, j: (0, j))],
        out_specs=[pl.BlockSpec((tm, tn), lambda i, j: (i, j)) for _ in out_dtypes],
        out_shape=[jax.ShapeDtypeStruct((m, n), dt) for dt in out_dtypes],
        compiler_params=_cparams(("parallel", "arbitrary")),
        name="proj_matmul",
    )(x, w, col_scale)


def _kv_proj_kernel(x_ref, w_ref, base_ref, o_ref, ob_ref, *, tm, nh):
    del base_ref
    acc = _dot(x_ref[...], w_ref[...])
    ob_ref[...] = acc.astype(bf16)
    for h in range(nh):
        o_ref[0, pl.ds(h, tm, stride=nh), :] = acc[:, h * DH_A:(h + 1) * DH_A]


def _kv_proj(x, w, head_rows, tm_pref=512):
    m, k = x.shape
    n = w.shape[1]
    n_seq, row_off, nh, _ = head_rows.shape
    t_out = row_off + m // n_seq
    tm = _tile(m // n_seq, tm_pref)
    per_seq = m // n_seq // tm
    base = jnp.pad(head_rows.reshape(n_seq, row_off * nh, DH_A), ((0, 0), (0, (t_out - row_off) * nh), (0, 0)))
    o, ob = pl.pallas_call(
        functools.partial(_kv_proj_kernel, tm=tm, nh=nh),
        grid=(m // tm,),
        in_specs=[pl.BlockSpec((tm, k), lambda i: (i, 0)), pl.BlockSpec((k, n), lambda i: (0, 0)),
                  pl.BlockSpec(memory_space=pl.ANY)],
        out_specs=[pl.BlockSpec((pl.Element(1), pl.Element(tm * nh), pl.Element(DH_A)),
                                lambda i: (i // per_seq,
                                           pl.multiple_of((row_off + (i % per_seq) * tm) * nh, SUBLANE), 0)),
                   pl.BlockSpec((tm, n), lambda i: (i, 0))],
        out_shape=[jax.ShapeDtypeStruct((n_seq, t_out * nh, DH_A), f32), jax.ShapeDtypeStruct((m, n), bf16)],
        input_output_aliases={2: 0},
        compiler_params=_cparams(("parallel",)),
        name="kv_proj",
    )(x, w, base)
    return o.reshape(n_seq, t_out, nh, DH_A), ob


def _outproj_ln_kernel(*refs, n_act):
    acts = refs[:n_act]
    w_ref, res_ref, g_ref, b_ref, of_ref, ob_ref = refs[n_act:]
    kw = acts[0].shape[1]
    mix = _dot(acts[0][...], w_ref[0:kw, :])
    for a in range(1, n_act):
        mix = mix + _dot(acts[a][...], w_ref[a * kw:(a + 1) * kw, :])
    y = _layer_norm(ALPHA * res_ref[...] + mix, g_ref[...], b_ref[...])
    of_ref[...] = y
    ob_ref[...] = y.astype(bf16)


def _outproj_ln(acts, w, res, g, b, tm_pref=512):
    m, d = res.shape
    tm = _tile(m, tm_pref)
    n_act = len(acts)
    row = lambda width: pl.BlockSpec((tm, width), lambda i: (i, 0))
    once = lambda shape: pl.BlockSpec(shape, lambda i: (0, 0), pipeline_mode=pl.Buffered(1))
    return pl.pallas_call(
        functools.partial(_outproj_ln_kernel, n_act=n_act),
        grid=(m // tm,),
        in_specs=[row(a.shape[1]) for a in acts] + [once(w.shape), row(d), once((1, d)), once((1, d))],
        out_specs=[row(d), row(d)],
        out_shape=[jax.ShapeDtypeStruct((m, d), f32), jax.ShapeDtypeStruct((m, d), bf16)],
        compiler_params=_cparams(("parallel",)),
        name="outproj_ln",
    )(*acts, w, res, g.reshape(1, d), b.reshape(1, d))


def _ffn_ln_kernel(xb_ref, xf_ref, wg_ref, wu_ref, wd_ref, g_ref, b_ref, of_ref, ob_ref, acc_ref):
    f = pl.program_id(1)

    @pl.when(f == 0)
    def _():
        acc_ref[...] = jnp.zeros_like(acc_ref)

    x = xb_ref[...]
    gate = _dot(x, wg_ref[...])
    up = _dot(x, wu_ref[...])
    h = (gate * _sigmoid(gate) * up).astype(bf16)
    acc_ref[...] += _dot(h, wd_ref[...])

    @pl.when(f == pl.num_programs(1) - 1)
    def _():
        y = _layer_norm(ALPHA * xf_ref[...] + acc_ref[...], g_ref[...], b_ref[...])
        of_ref[...] = y
        ob_ref[...] = y.astype(bf16)


def _ffn_ln(xb, xf, wg, wu, wd, g, b, tm_pref=512, tf_pref=512):
    m, d = xf.shape
    dff = wg.shape[1]
    tm = _tile(m, tm_pref)
    tf = _tile(dff, tf_pref, LANE)
    return pl.pallas_call(
        _ffn_ln_kernel,
        grid=(m // tm, dff // tf),
        in_specs=[pl.BlockSpec((tm, d), lambda i, f: (i, 0)),
                  pl.BlockSpec((tm, d), lambda i, f: (i, 0)),
                  pl.BlockSpec((d, tf), lambda i, f: (0, f)),
                  pl.BlockSpec((d, tf), lambda i, f: (0, f)),
                  pl.BlockSpec((tf, d), lambda i, f: (f, 0)),
                  pl.BlockSpec((1, d), lambda i, f: (0, 0)),
                  pl.BlockSpec((1, d), lambda i, f: (0, 0))],
        out_specs=[pl.BlockSpec((tm, d), lambda i, f: (i, 0)),
                   pl.BlockSpec((tm, d), lambda i, f: (i, 0))],
        out_shape=[jax.ShapeDtypeStruct((m, d), f32), jax.ShapeDtypeStruct((m, d), bf16)],
        scratch_shapes=[pltpu.VMEM((tm, d), f32)],
        compiler_params=_cparams(("parallel", "arbitrary")),
        name="ffn_ln",
    )(xb, xf, wg, wu, wd, g.reshape(1, d), b.reshape(1, d))


def _cumsum_kernel(x_ref, bias_ref, carry_ref, logf_ref, cum_ref, acc_ref, *, gate):
    c = pl.program_id(1)

    @pl.when(c == 0)
    def _():
        acc_ref[...] = carry_ref[0]

    x = x_ref[0]
    lf = _log_sigmoid(x + bias_ref[...]) if gate else x
    tc = x.shape[0]
    cum = _dot_exact_rhs(_tri_lower(tc), lf) + acc_ref[...]
    logf_ref[0] = lf
    cum_ref[0] = cum
    acc_ref[...] = cum[tc - 1:tc, :]


def _logf_cumsum(x, bias, carry, gate, tc_pref=512):
    bsz, t, w = x.shape
    tc = _tile(t, tc_pref)
    spec = pl.BlockSpec((1, tc, w), lambda b, c: (b, c, 0))
    return pl.pallas_call(
        functools.partial(_cumsum_kernel, gate=gate),
        grid=(bsz, t // tc),
        in_specs=[spec,
                  pl.BlockSpec((1, w), lambda b, c: (0, 0)),
                  pl.BlockSpec((1, 1, w), lambda b, c: (b, 0, 0))],
        out_specs=[spec, spec],
        out_shape=[jax.ShapeDtypeStruct((bsz, t, w), f32)] * 2,
        scratch_shapes=[pltpu.VMEM((1, w), f32)],
        compiler_params=_cparams(("parallel", "arbitrary")),
        name="logf_cumsum",
    )(x, bias, carry)


def _attn_kernel(*refs, tq, tk, tk2, s1, s1_valid, hps):
    if s1:
        q_ref, k1_ref, v1_ref, f1_ref, k2_ref, v2_ref, f2_ref, o_ref = refs
    else:
        q_ref, k2_ref, v2_ref, f2_ref, o_ref = refs
    qi = pl.program_id(2)
    d = DH_A

    for hh in range(hps):
        cols = slice(hh * d, (hh + 1) * d)
        q = q_ref[0, :, cols]

        def scores(k, q=q):
            return _dot(q, k.astype(bf16), _NT)

        def update(carry, s, v, fk, mask):
            m, l, acc = carry
            s = s - fk * LOG2E
            if mask is not None:
                s = jnp.where(mask, s, NEG)
            m_new = jnp.maximum(m, jnp.max(s, axis=-1, keepdims=True))
            alpha = jnp.exp2(m - m_new)
            p = jnp.exp2(s - m_new)
            l = alpha * l + jnp.sum(p, axis=-1, keepdims=True)
            acc = alpha * acc + _dot(p.astype(bf16), v.astype(bf16))
            return m_new, l, acc

        def step(j, c, mask, row0=0, cols=cols, hh=hh, q=q, update=update):
            off = pl.multiple_of(j * tk, tk)
            s = _dot(q[row0:], k2_ref[0, pl.ds(off, tk), cols].astype(bf16), _NT)
            new = update(tuple(a[row0:] for a in c), s, v2_ref[0, pl.ds(off, tk), cols],
                         f2_ref[0, hh, :, pl.ds(off, tk)], mask)
            if row0 == 0:
                return new
            return tuple(jnp.concatenate([a[:row0], b], axis=0) for a, b in zip(c, new))

        carry = (jnp.full((tq, 1), NEG, f32), jnp.zeros((tq, 1), f32), jnp.zeros((tq, d), f32))
        if s1:
            pmask = None if s1_valid == s1 else (_iota2((tq, s1), 1) < s1_valid)
            carry = update(carry, scores(k1_ref[0, :, cols]), v1_ref[0, :, cols], f1_ref[0, hh], pmask)

        nd = tk2 // tk
        carry = lax.fori_loop(0, qi * nd, lambda j, c, step=step: step(j, c, None), carry)
        for r in range(nd):
            row0 = r * tk if tq == tk2 else 0
            causal = _iota2((tq - row0, tk), 0) + row0 >= _iota2((tq - row0, tk), 1) + r * tk
            carry = step(qi * nd + r, carry, causal, row0)
        m, l, acc = carry
        o_ref[0, :, cols] = (acc / l).astype(o_ref.dtype)


def _fox_attention(q, k2, v2, f2, prefix=None, s1_valid=0, tq=512, tk=512, hps=1):
    bsz, t, hd = q.shape
    t2 = k2.shape[1]
    nq = t // tq
    tk2 = t2 // nq
    w = hps * DH_A
    qspec = pl.BlockSpec((1, tq, w), lambda b, h, i: (b, i, h))
    kvspec = pl.BlockSpec((1, t2, w), lambda b, h, i: (b, 0, h))
    fspec = pl.BlockSpec((1, hps, 1, t2), lambda b, h, i: (b, h, 0, 0))
    args, in_specs = [q], [qspec]
    s1 = 0
    if prefix is not None:
        k1, v1, f1 = prefix
        s1 = k1.shape[1]
        p_spec = pl.BlockSpec((1, s1, w), lambda b, h, i: (b, 0, h))
        args += [k1, v1, f1]
        in_specs += [p_spec, p_spec, pl.BlockSpec((1, hps, 1, s1), lambda b, h, i: (b, h, 0, 0))]
    args += [k2, v2, f2]
    in_specs += [kvspec, kvspec, fspec]
    return pl.pallas_call(
        functools.partial(_attn_kernel, tq=tq, tk=min(tk, tk2), tk2=tk2, s1=s1, s1_valid=s1_valid, hps=hps),
        grid=(bsz, hd // w, nq),
        in_specs=in_specs,
        out_specs=qspec,
        out_shape=jax.ShapeDtypeStruct((bsz, t, hd), bf16),
        compiler_params=_cparams(("parallel", "parallel", "arbitrary")),
        name="fox_attention",
    )(*args)


def _attn_cache_kernel(q_ref, k1_ref, v1_ref, f1_ref, k2_ref, v2_ref, f2_ref, o_ref, m_ref, l_ref, acc_ref,
                       *, tk, nh):
    j = pl.program_id(1)
    d = DH_A
    tq = q_ref.shape[1]

    @pl.when(j == 0)
    def _():
        m_ref[...] = jnp.full(m_ref.shape, NEG, f32)
        l_ref[...] = jnp.zeros(l_ref.shape, f32)
        acc_ref[...] = jnp.zeros(acc_ref.shape, f32)

    def update(h, s, v, fk, mask):
        s = s - fk * LOG2E
        if mask is not None:
            s = jnp.where(mask, s, NEG)
        m = m_ref[h][:, 0:1]
        m_new = jnp.maximum(m, jnp.max(s, axis=-1, keepdims=True))
        alpha = jnp.exp2(m - m_new)
        p = jnp.exp2(s - m_new)
        l_ref[h] = jnp.broadcast_to(alpha * l_ref[h][:, 0:1] + jnp.sum(p, axis=-1, keepdims=True), (tq, LANE))
        acc_ref[h] = alpha * acc_ref[h] + _dot(p.astype(bf16), v)
        m_ref[h] = jnp.broadcast_to(m_new, (tq, LANE))

    qs = [q_ref[0, :, h * d:(h + 1) * d] for h in range(nh)]
    ss = [_dot(qs[h], k1_ref[0, pl.ds(h, tk, stride=nh), :].astype(bf16), _NT) for h in range(nh)]
    for h in range(nh):
        update(h, ss[h], v1_ref[0, pl.ds(h, tk, stride=nh), :].astype(bf16), f1_ref[0, h], None)

    @pl.when(j == pl.num_programs(1) - 1)
    def _():
        t2 = k2_ref.shape[1]
        causal = _iota2((tq, t2), 0) >= _iota2((tq, t2), 1)
        for h in range(nh):
            cols = slice(h * d, (h + 1) * d)
            update(h, _dot(qs[h], k2_ref[0, :, cols], _NT), v2_ref[0, :, cols], f2_ref[0, h], causal)
            o_ref[0, :, cols] = (acc_ref[h] / l_ref[h][:, 0:1]).astype(o_ref.dtype)


def _fox_attention_cache(q, k1, v1, f1, k2, v2, f2, tk=512):
    bsz, tq, hd = q.shape
    s, nh = k1.shape[1], k1.shape[2]
    t2 = k2.shape[1]
    flat = lambda a: a.reshape(bsz, s * nh, DH_A)
    qspec = pl.BlockSpec((1, tq, hd), lambda b, j: (b, 0, 0))
    cspec = pl.BlockSpec((1, tk * nh, DH_A), lambda b, j: (b, j, 0))
    nspec = pl.BlockSpec((1, t2, hd), lambda b, j: (b, 0, 0))
    return pl.pallas_call(
        functools.partial(_attn_cache_kernel, tk=tk, nh=nh),
        grid=(bsz, s // tk),
        in_specs=[qspec, cspec, cspec, pl.BlockSpec((1, nh, 1, tk), lambda b, j: (b, 0, 0, j)),
                  nspec, nspec, pl.BlockSpec((1, nh, 1, t2), lambda b, j: (b, 0, 0, 0))],
        out_specs=qspec,
        out_shape=jax.ShapeDtypeStruct((bsz, tq, hd), bf16),
        scratch_shapes=[pltpu.VMEM((nh, tq, LANE), f32), pltpu.VMEM((nh, tq, LANE), f32),
                        pltpu.VMEM((nh, tq, DH_A), f32)],
        compiler_params=_cparams(("parallel", "arbitrary")),
        name="fox_attention_cache",
    )(q, flat(k1), flat(v1), f1, k2, v2, f2)


def _forget_columns(cum, h):
    r, c = _iota2((LANE, LANE), 0), _iota2((LANE, LANE), 1)
    out = None
    for i, piece in enumerate(_split3(cum * LOG2E)):
        t = _dot(piece, ((r == h) & (c == i)).astype(bf16))
        out = t if out is None else out + t
    return out.astype(bf16)


def _attn_t_kernel(q_ref, k1_ref, v1_ref, c1_ref, k2_ref, v2_ref, c2_ref, o_ref, kaug_ref, *, tq, s1_valid, hps):
    hg = pl.program_id(1)
    qi = pl.program_id(2)
    d = DH_A

    @pl.when(qi == 0)
    def _():
        for hh in range(hps):
            kaug_ref[hh, :, 0:d] = k2_ref[0, :, hh * d:(hh + 1) * d]
            kaug_ref[hh, :, d:2 * d] = _forget_columns(c2_ref[0], hg * hps + hh)

    minus_one = jnp.where(_iota2((tq, d), 1) < 3, -1.0, 0.0).astype(bf16)
    q_aug = [jnp.concatenate([q_ref[0, :, hh * d:(hh + 1) * d], minus_one], axis=-1) for hh in range(hps)]

    def update(carry, s, v, mask):
        m, l, acc = carry
        if mask is not None:
            s = jnp.where(mask, s, NEG)
        m_new = jnp.maximum(m, jnp.max(s, axis=0, keepdims=True))
        alpha = jnp.exp2(m - m_new)
        p = jnp.exp2(s - m_new)
        l = alpha * l + jnp.sum(p, axis=0, keepdims=True)
        acc = alpha * acc + _dot(v, p.astype(bf16), _TN)
        return m_new, l, acc

    def multi(carries, k_augs, vs, mask):
        ss = [_dot(k_augs[hh], q_aug[hh], _NT) for hh in range(hps)]
        return [update(carries[hh], ss[hh], vs[hh], mask) for hh in range(hps)]

    carries = [(jnp.full((1, tq), NEG, f32), jnp.zeros((1, tq), f32), jnp.zeros((d, tq), f32))
               for _ in range(hps)]
    s1 = k1_ref.shape[1]
    k1_aug = [jnp.concatenate([k1_ref[0, :, hh * d:(hh + 1) * d], _forget_columns(c1_ref[0], hg * hps + hh)], axis=-1)
              for hh in range(hps)]
    carries = multi(carries, k1_aug, [v1_ref[0, :, hh * d:(hh + 1) * d] for hh in range(hps)],
                    _iota2((s1, tq), 0) < s1_valid)

    def blk(j, c, mask):
        off = pl.multiple_of(j * tq, tq)
        return multi(c, [kaug_ref[hh, pl.ds(off, tq), :] for hh in range(hps)],
                     [v2_ref[0, pl.ds(off, tq), hh * d:(hh + 1) * d] for hh in range(hps)], mask)

    carries = lax.fori_loop(0, qi, lambda j, c: blk(j, c, None), carries)
    carries = blk(qi, carries, _iota2((tq, tq), 0) <= _iota2((tq, tq), 1))
    for hh in range(hps):
        m, l, acc = carries[hh]
        o_ref[0, :, hh * d:(hh + 1) * d] = jnp.transpose(acc / l).astype(o_ref.dtype)


def _fox_attention_main(q, k1, v1, c1, k2, v2, c2, s1_valid, tq=512, hps=4):
    bsz, t, hd = q.shape
    s1 = k1.shape[1]
    w = hps * DH_A
    qspec = pl.BlockSpec((1, tq, w), lambda b, h, i: (b, i, h))
    kvspec = pl.BlockSpec((1, t, w), lambda b, h, i: (b, 0, h))
    pspec = pl.BlockSpec((1, s1, w), lambda b, h, i: (b, 0, h))
    return pl.pallas_call(
        functools.partial(_attn_t_kernel, tq=tq, s1_valid=s1_valid, hps=hps),
        grid=(bsz, hd // w, t // tq),
        in_specs=[qspec, pspec, pspec, pl.BlockSpec((1, s1, LANE), lambda b, h, i: (b, 0, 0)),
                  kvspec, kvspec, pl.BlockSpec((1, t, LANE), lambda b, h, i: (b, 0, 0))],
        out_specs=qspec,
        out_shape=jax.ShapeDtypeStruct((bsz, t, hd), bf16),
        scratch_shapes=[pltpu.VMEM((hps, t, 2 * DH_A), bf16)],
        compiler_params=_cparams(("parallel", "parallel", "arbitrary")),
        name="fox_attention_main",
    )(q, k1, v1, c1, k2, v2, c2)


def _ssd_kernel(zx_ref, dt_ref, hist_ref, h0_ref, cw_ref, cb_ref, dtb_ref, alog_ref, alogc_ref,
                dskip_ref, ng_ref, e_ref, y_ref, tail_ref, hout_ref, ext_ref, st_ref, *, lc, t_valid):
    c = pl.program_id(1)

    @pl.when(c == 0)
    def _():
        ext_ref[0:SUBLANE, :] = hist_ref[0]
        st_ref[...] = h0_ref[0]

    ext_ref[SUBLANE:SUBLANE + lc, :] = zx_ref[0, :, D_SSM:]
    conv = cb_ref[...]
    for i in range(CONV_W):
        lo = SUBLANE - (CONV_W - 1) + i
        conv = conv + cw_ref[i:i + 1, :] * ext_ref[lo:lo + lc, :]
    xc = conv * _sigmoid(conv)
    tail = ext_ref[t_valid:t_valid + SUBLANE, :]
    ext_ref[0:SUBLANE, :] = tail

    dt = _softplus(dt_ref[0] + dtb_ref[...])
    if t_valid < lc:
        dt = jnp.where(_iota2((lc, LANE), 0) < t_valid, dt, 0.0)
    a_row = -jnp.exp(alog_ref[...])
    a_col = -jnp.exp(alogc_ref[...])
    a_cum = _dot_exact_rhs(_tri_lower(lc), dt * a_row)
    dt_t = _dot_exact_rhs(_eye(H_B, LANE), dt, _NT)
    a_cum_t = _dot_exact_lhs(dt_t * a_col, _tri_upper(lc))
    a_last = a_cum[lc - 1:lc, :]
    e = e_ref[...]
    to_end_x = _dot_exact_lhs(jnp.exp(a_last - a_cum) * dt, e)
    ea_x = _dot_exact_lhs(jnp.exp(a_cum), e)
    cdec_x = _dot_exact_lhs(jnp.broadcast_to(jnp.exp(a_last), (SUBLANE, LANE)), e)[0:1, :]

    causal = _iota2((lc, lc), 0) >= _iota2((lc, lc), 1)
    left = _iota2((lc, LANE), 1) < P_B
    eye_n = _eye(N_B, N_B)
    gw = D_SSM // G_B
    groups = range(G_B)
    gcols = [slice(g * gw, (g + 1) * gw) for g in groups]
    bg = [xc[:, D_SSM + g * N_B:D_SSM + (g + 1) * N_B].astype(bf16) for g in groups]
    cg = [xc[:, D_SSM + G_B * N_B + g * N_B:D_SSM + G_B * N_B + (g + 1) * N_B].astype(bf16) for g in groups]
    cb = [_dot(cg[g], bg[g], _NT) for g in groups]
    bg_t = [_dot(eye_n, bg[g], _NT).astype(bf16) for g in groups]
    st_in = [st_ref[:, gcols[g]] for g in groups]
    y_off = [_dot(cg[g], st_in[g].astype(bf16)) * ea_x[:, gcols[g]] for g in groups]
    upd = [_dot(bg_t[g], (xc[:, gcols[g]] * to_end_x[:, gcols[g]]).astype(bf16)) for g in groups]
    for g in groups:
        st_ref[:, gcols[g]] = cdec_x[:, gcols[g]] * st_in[g] + upd[g]

    def decay_weights(h):
        seg = a_cum[:, h:h + 1] - a_cum_t[h:h + 1, :]
        dec = jnp.exp(jnp.where(causal, seg, NEG))
        return (cb[h // HG_B] * dec * dt_t[h:h + 1, :]).astype(bf16)

    w_all = [decay_weights(h) for h in range(H_B)]
    y_pair = []
    for j in range(D_SSM // LANE):
        xp = xc[:, j * LANE:(j + 1) * LANE].astype(bf16)
        y_pair.append(jnp.where(left, _dot(w_all[2 * j], xp), _dot(w_all[2 * j + 1], xp)))
    ppg = gw // LANE
    for g in groups:
        xg = xc[:, gcols[g]]
        y = jnp.concatenate(y_pair[g * ppg:(g + 1) * ppg], axis=-1) + y_off[g] + dskip_ref[:, gcols[g]] * xg
        z = zx_ref[0, :, g * gw:(g + 1) * gw]
        y = y * (z * _sigmoid(z))
        y = y * lax.rsqrt(jnp.mean(y * y, axis=-1, keepdims=True) + LN_EPS) * ng_ref[:, g * gw:(g + 1) * gw]
        y_ref[0, :, g * gw:(g + 1) * gw] = y.astype(y_ref.dtype)

    @pl.when(c == pl.num_programs(1) - 1)
    def _():
        tail_ref[0] = tail
        hout_ref[0] = st_ref[...]


def _ssd_mixer(zx, dt_raw, hist, h0_t, p, lc, t_valid):
    bsz, t, wz = zx.shape
    const = lambda shape: pl.BlockSpec(shape, lambda b, c: (0,) * len(shape))
    return pl.pallas_call(
        functools.partial(_ssd_kernel, lc=lc, t_valid=t_valid),
        grid=(bsz, t // lc),
        in_specs=[pl.BlockSpec((1, lc, wz), lambda b, c: (b, c, 0)),
                  pl.BlockSpec((1, lc, LANE), lambda b, c: (b, c, 0)),
                  pl.BlockSpec((1, SUBLANE, CONV_DIM), lambda b, c: (b, 0, 0)),
                  pl.BlockSpec((1, N_B, D_SSM), lambda b, c: (b, 0, 0)),
                  const((SUBLANE, CONV_DIM)), const((1, CONV_DIM)), const((1, LANE)), const((1, LANE)),
                  const((H_B, 1)), const((1, D_SSM)), const((1, D_SSM)), const((LANE, D_SSM))],
        out_specs=[pl.BlockSpec((1, lc, D_SSM), lambda b, c: (b, c, 0)),
                   pl.BlockSpec((1, SUBLANE, CONV_DIM), lambda b, c: (b, 0, 0)),
                   pl.BlockSpec((1, N_B, D_SSM), lambda b, c: (b, 0, 0))],
        out_shape=[jax.ShapeDtypeStruct((bsz, t, D_SSM), bf16),
                   jax.ShapeDtypeStruct((bsz, SUBLANE, CONV_DIM), f32),
                   jax.ShapeDtypeStruct((bsz, N_B, D_SSM), f32)],
        scratch_shapes=[pltpu.VMEM((SUBLANE + lc, CONV_DIM), f32), pltpu.VMEM((N_B, D_SSM), f32)],
        compiler_params=_cparams(("parallel", "arbitrary")),
        name="ssd_mixer",
    )(zx, dt_raw, hist, h0_t, p["conv_w"], p["conv_b"], p["dt_bias"], p["a_log_row"], p["a_log_col"],
      p["d_skip"], p["norm_g"], p["expand"])


def _mlstm_kernel(qkv_ref, o_ref, gates_ref, gb_ref, ng_ref, c0_ref, n0_ref, m0_ref,
                  y_ref, cout_ref, nout_ref, mout_ref, ct_ref, n_ref, m_ref, *, lc, t_valid):
    c = pl.program_id(1)

    @pl.when(c == 0)
    def _():
        ct_ref[...] = c0_ref[0]
        n_ref[...] = n0_ref[0]
        m_ref[...] = m0_ref[0]

    lane = _iota2((lc, LANE), 1)
    g = gates_ref[0] + gb_ref[...]
    is_f = (lane >= H_C) & (lane < 2 * H_C)
    lf = jnp.where(is_f, _log_sigmoid(g), 0.0)
    li = g
    if t_valid < lc:
        valid = _iota2((lc, LANE), 0) < t_valid
        lf = jnp.where(valid, lf, 0.0)
        li = jnp.where(valid, li, NEG)
    b_cum = _dot_exact_rhs(_tri_lower(lc), lf)
    rows = 2 * H_C
    li_t = _dot_exact_rhs(_eye(rows, LANE), li, _NT)
    lf_t = _dot_exact_rhs(_eye(rows, LANE), lf, _NT)
    b_cum_t = _dot_exact_lhs(lf_t, _tri_upper(lc))

    causal = _iota2((lc, lc), 0) >= _iota2((lc, lc), 1)
    eye_k = _eye(DK_C, DK_C)
    m_all = m_ref[...]
    m_out = m_all
    q_of = lambda h: qkv_ref[0, :, h * DK_C:(h + 1) * DK_C]
    k_of = lambda h: qkv_ref[0, :, D_QK_C + h * DK_C:D_QK_C + (h + 1) * DK_C]
    v_of = lambda h: qkv_ref[0, :, 2 * D_QK_C + h * DV_C:2 * D_QK_C + (h + 1) * DV_C]
    heads = range(H_C)
    qk_all = [_dot(q_of(h), k_of(h), _NT) for h in heads]
    qc_all = [_dot(q_of(h), ct_ref[h].astype(bf16)) for h in heads]
    kt_all = [_dot(eye_k, k_of(h), _NT) for h in heads]
    bcol = [b_cum[:, H_C + h:H_C + h + 1] for h in heads]
    brow = [b_cum_t[H_C + h:H_C + h + 1, :] for h in heads]
    lirow = [li_t[h:h + 1, :] for h in heads]
    m_st = [m_all[:, h:h + 1] for h in heads]
    d_mat = [jnp.where(causal, bcol[h] - brow[h] + lirow[h], NEG) for h in heads]
    inter = [bcol[h] + m_st[h] for h in heads]
    m_row = [jnp.maximum(inter[h], jnp.max(d_mat[h], axis=-1, keepdims=True)) for h in heads]
    w = [jnp.exp(d_mat[h] - m_row[h]) * qk_all[h] for h in heads]
    g_inter = [jnp.exp(inter[h] - m_row[h]) for h in heads]
    n_st = [n_ref[h:h + 1, :] for h in heads]
    num = [_dot(w[h].astype(bf16), v_of(h)) + g_inter[h] * qc_all[h] for h in heads]
    den = [jnp.sum(w[h], axis=-1, keepdims=True)
           + g_inter[h] * jnp.sum(q_of(h).astype(f32) * n_st[h], axis=-1, keepdims=True) for h in heads]
    hh = [num[h] / jnp.maximum(jnp.abs(den[h]), jnp.exp(-m_row[h])) for h in heads]
    ms = [jnp.mean(hh[h] * hh[h], axis=-1, keepdims=True) for h in heads]
    for h in heads:
        cols = slice(h * DV_C, (h + 1) * DV_C)
        y = hh[h] * lax.rsqrt(ms[h] + LN_EPS) * ng_ref[:, cols] * _sigmoid(o_ref[0, :, cols])
        y_ref[0, :, cols] = y.astype(y_ref.dtype)
    b_tot = [bcol[h][lc - 1:lc, :] for h in heads]
    d_end = [b_tot[h] - brow[h] + lirow[h] for h in heads]
    m_new = [jnp.maximum(b_tot[h] + m_st[h], jnp.max(d_end[h], axis=-1, keepdims=True)) for h in heads]
    w_end = [jnp.exp(d_end[h] - m_new[h]) for h in heads]
    g_old = [jnp.exp(b_tot[h] + m_st[h] - m_new[h]) for h in heads]
    for h in heads:
        ct_ref[h] = g_old[h] * ct_ref[h] + _dot((kt_all[h] * w_end[h]).astype(bf16), v_of(h))
        n_upd = _dot(jnp.broadcast_to(w_end[h], (SUBLANE, lc)).astype(bf16), k_of(h))[0:1, :]
        n_ref[h:h + 1, :] = g_old[h] * n_st[h] + n_upd
        m_out = jnp.where(lane[0:1, :] == h, m_new[h], m_out)
    m_ref[...] = m_out

    @pl.when(c == pl.num_programs(1) - 1)
    def _():
        cout_ref[0] = ct_ref[...]
        nout_ref[0] = n_ref[...]
        mout_ref[0] = m_ref[...]


def _mlstm_mixer(qkv, o, gates, c0_t, n0, m0, p, lc, t_valid):
    bsz, t, wq = qkv.shape
    const = lambda shape: pl.BlockSpec(shape, lambda b, c: (0,) * len(shape))
    return pl.pallas_call(
        functools.partial(_mlstm_kernel, lc=lc, t_valid=t_valid),
        grid=(bsz, t // lc),
        in_specs=[pl.BlockSpec((1, lc, wq), lambda b, c: (b, c, 0)),
                  pl.BlockSpec((1, lc, D_V_C), lambda b, c: (b, c, 0)),
                  pl.BlockSpec((1, lc, LANE), lambda b, c: (b, c, 0)),
                  const((1, LANE)), const((1, D_V_C)),
                  pl.BlockSpec((1, H_C, DK_C, DV_C), lambda b, c: (b, 0, 0, 0)),
                  pl.BlockSpec((1, H_C, DK_C), lambda b, c: (b, 0, 0)),
                  pl.BlockSpec((1, 1, LANE), lambda b, c: (b, 0, 0))],
        out_specs=[pl.BlockSpec((1, lc, D_V_C), lambda b, c: (b, c, 0)),
                   pl.BlockSpec((1, H_C, DK_C, DV_C), lambda b, c: (b, 0, 0, 0)),
                   pl.BlockSpec((1, H_C, DK_C), lambda b, c: (b, 0, 0)),
                   pl.BlockSpec((1, 1, LANE), lambda b, c: (b, 0, 0))],
        out_shape=[jax.ShapeDtypeStruct((bsz, t, D_V_C), bf16),
                   jax.ShapeDtypeStruct((bsz, H_C, DK_C, DV_C), f32),
                   jax.ShapeDtypeStruct((bsz, H_C, DK_C), f32),
                   jax.ShapeDtypeStruct((bsz, 1, LANE), f32)],
        scratch_shapes=[pltpu.VMEM((H_C, DK_C, DV_C), f32), pltpu.VMEM((H_C, DK_C), f32),
                        pltpu.VMEM((1, LANE), f32)],
        compiler_params=_cparams(("parallel", "arbitrary")),
        name="mlstm_mixer",
    )(qkv, o, gates, p["gate_bias"], p["norm_g"], c0_t, n0, m0)


def _pad_lanes(a, width=LANE):
    return jnp.pad(a, [(0, 0)] * (a.ndim - 1) + [(0, width - a.shape[-1])])


def _pad_seq(a, t_pad=SEQ_PAD):
    return jnp.pad(a, [(0, 0), (0, t_pad - a.shape[1])] + [(0, 0)] * (a.ndim - 2))


def _even_layer(xm_b, xm_f, xa_b, xa_f, n_p, t_main, cache, w):
    k_past, v_past, logf_past, conv_past, h_past = cache
    n_s = k_past.shape[0]
    n_aux = n_p + n_s
    t_aux = xa_f.shape[0] // n_aux
    past = k_past.shape[1]

    seq = lambda a, n: a.reshape(n, a.shape[0] // n, a.shape[1])
    heads = lambda a: a.reshape(a.shape[0], a.shape[1], H_A, DH_A)

    def project(xb, meta_kv):
        q, = _matmul(xb, w["w_q"], [bf16], col_scale=jnp.full((1, D_A), DH_A ** -0.5 * LOG2E, f32))
        if meta_kv is None:
            k_f, k_b = _matmul(xb, w["w_k"], [f32, bf16])
            v_f, v_b = _matmul(xb, w["w_v"], [f32, bf16])
        else:
            k_f, k_b = _kv_proj(xb, w["w_k"], meta_kv[0])
            v_f, v_b = _kv_proj(xb, w["w_v"], meta_kv[1])
        zx, = _matmul(xb, w["w_zx"], [f32])
        sm, = _matmul(xb, w["w_small"], [f32], tn_pref=2 * LANE)
        return q, k_f, k_b, v_f, v_b, zx, sm

    qa, ka_f, ka_b, va_f, va_b, zxa, sma = project(xa_b, None)
    qa, ka_f, va_f = [seq(a, n_aux) for a in (qa, ka_f, va_f)]
    ka_b, va_b, zxa, sma = [_pad_seq(seq(a, n_aux)) for a in (ka_b, va_b, zxa, sma)]
    ka_f, va_f = heads(ka_f), heads(va_f)
    qm, kp_f, km_b, vp_f, vm_b, zxm, smm = project(xm_b, (ka_f[:n_p], va_f[:n_p]))
    qm, km_b, vm_b, zxm, smm = [seq(a, n_p) for a in (qm, km_b, vm_b, zxm, smm)]

    zero_c = jnp.zeros((n_s, 1, LANE), f32)
    _, cum_past = _logf_cumsum(_pad_lanes(logf_past.astype(f32)), w["b_f"], zero_c, gate=False)
    carry_a = jnp.concatenate([jnp.zeros((n_p, 1, LANE), f32), cum_past[:, past - 1:past]], axis=0)
    logf_a, cum_a = _logf_cumsum(sma[:, :, LANE:], w["b_f"], carry_a, gate=True)
    logf_m, cum_m = _logf_cumsum(smm[:, :, LANE:], w["b_f"], cum_a[:n_p, t_aux - 1:t_aux], gate=True)
    rows = lambda cum: jnp.swapaxes(cum[:, :, :H_A], 1, 2)[:, :, None, :]

    fa = rows(cum_a)
    att_meta = _fox_attention(qa[:n_p], ka_b[:n_p], va_b[:n_p], fa[:n_p], tq=t_aux, hps=2)
    att_s = _fox_attention_cache(qa[n_p:], k_past, v_past, rows(cum_past), ka_b[n_p:], va_b[n_p:], fa[n_p:],
                                 tk=min(past, 512))
    att_m = _fox_attention_main(qm, ka_b[:n_p], va_b[:n_p], cum_a[:n_p], km_b, vm_b, cum_m, t_aux,
                                tq=min(t_main, 512))
    att_a = jnp.concatenate([att_meta, att_s], axis=0)

    hist_a = jnp.concatenate([jnp.zeros((n_p, SUBLANE, CONV_DIM), f32),
                              jnp.pad(conv_past.astype(f32), ((0, 0), (SUBLANE - (CONV_W - 1), 0), (0, 0)))], axis=0)
    to_t = lambda h: jnp.transpose(h.reshape(-1, D_SSM, N_B), (0, 2, 1))
    from_t = lambda h: jnp.transpose(h, (0, 2, 1)).reshape(-1, H_B, P_B, N_B)
    h0_a = jnp.concatenate([jnp.zeros((n_p, N_B, D_SSM), f32), to_t(h_past.astype(f32))], axis=0)
    ya, tail_a, h_a = _ssd_mixer(zxa, sma[:, :, :LANE], hist_a, h0_a, w, SEQ_PAD, t_aux)
    ym, tail_m, h_m = _ssd_mixer(zxm, smm[:, :, :LANE], tail_a[:n_p], h_a[:n_p], w, min(t_main, 128), min(t_main, 128))

    unseq = lambda a: a.reshape(-1, a.shape[-1])
    x1m_f, x1m_b = _outproj_ln([unseq(att_m), unseq(ym)], w["w_out"], xm_f, w["ln_g"], w["ln_b"])
    x1a_f, x1a_b = _outproj_ln([unseq(att_a), unseq(ya[:, :t_aux])], w["w_out"], xa_f, w["ln_g"], w["ln_b"])

    new = dict(
        k_p=kp_f, v_p=vp_f,
        f_p=jnp.concatenate([logf_a[:n_p, :t_aux, :H_A], logf_m[:, :, :H_A]], axis=1),
        conv_p=tail_m[:, SUBLANE - (CONV_W - 1):], h_p=from_t(h_m),
        k_s=ka_f[n_p:], v_s=va_f[n_p:], f_s=logf_a[n_p:, :t_aux, :H_A],
        conv_s=tail_a[n_p:, SUBLANE - (CONV_W - 1):], h_s=from_t(h_a[n_p:]))
    return (x1m_f, x1m_b, x1a_f, x1a_b), new


def _odd_layer(xm_b, xm_f, xa_b, xa_f, n_p, t_main, cache, w):
    c_past, n_past, m_past = cache
    n_s = c_past.shape[0]
    n_aux = n_p + n_s
    t_aux = xa_f.shape[0] // n_aux

    def project(xb):
        qkv, = _matmul(xb, w["w_qkv"], [bf16], col_scale=w["qkv_scale"])
        o, = _matmul(xb, w["w_o"], [f32])
        gt, = _matmul(xb, w["w_gates"], [f32])
        return qkv, o, gt

    qkvm, om, gm = project(xm_b)
    qkva, oa, ga = project(xa_b)
    seq = lambda a, n: a.reshape(n, a.shape[0] // n, a.shape[1])
    qkvm, om, gm = [seq(a, n_p) for a in (qkvm, om, gm)]
    qkva, oa, ga = [_pad_seq(seq(a, n_aux)) for a in (qkva, oa, ga)]

    c0 = jnp.concatenate([jnp.zeros((n_p, H_C, DK_C, DV_C), f32), jnp.swapaxes(c_past.astype(f32), 2, 3)], axis=0)
    n0 = jnp.concatenate([jnp.zeros((n_p, H_C, DK_C), f32), n_past.astype(f32)], axis=0)
    m0 = jnp.concatenate([jnp.zeros((n_p, 1, LANE), f32), _pad_lanes(m_past.astype(f32))[:, None, :]], axis=0)
    ha, c_a, n_a, m_a = _mlstm_mixer(qkva, oa, ga, c0, n0, m0, w, SEQ_PAD, t_aux)
    lc = min(t_main, 256)
    hm, c_m, n_m, m_m = _mlstm_mixer(qkvm, om, gm, c_a[:n_p], n_a[:n_p], m_a[:n_p], w, lc, lc)

    unseq = lambda a: a.reshape(-1, a.shape[-1])
    x1m_f, x1m_b = _outproj_ln([unseq(hm)], w["w_out"], xm_f, w["ln_g"], w["ln_b"])
    x1a_f, x1a_b = _outproj_ln([unseq(ha[:, :t_aux])], w["w_out"], xa_f, w["ln_g"], w["ln_b"])
    new = dict(c_p=jnp.swapaxes(c_m, 2, 3), n_p=n_m, m_p=m_m[:, 0, :H_C],
               c_s=jnp.swapaxes(c_a[n_p:], 2, 3), n_s=n_a[n_p:], m_s=m_a[n_p:, 0, :H_C])
    return (x1m_f, x1m_b, x1a_f, x1a_b), new


def _even_weights(e, w_in_a, b_fgate_a, conv_w, conv_b, dt_bias, a_log, d_skip, ssd_norm_g, w_out_a):
    wi = w_in_a[e]
    o_f = 3 * D_A
    o_z = o_f + H_A
    o_dt = o_z + D_SSM + CONV_DIM
    w_small = jnp.concatenate([_pad_lanes(wi[:, o_dt:o_dt + H_B]), _pad_lanes(wi[:, o_f:o_f + H_A])], axis=1)
    expand = (np.arange(LANE)[:, None] == (np.arange(D_SSM) // P_B)[None, :])
    return dict(
        w_q=wi[:, :D_A].astype(bf16), w_k=wi[:, D_A:2 * D_A].astype(bf16),
        w_v=wi[:, 2 * D_A:3 * D_A].astype(bf16),
        w_zx=wi[:, o_z:o_dt].astype(bf16), w_small=w_small.astype(bf16),
        b_f=_pad_lanes(b_fgate_a[e][None, :]),
        conv_w=jnp.pad(conv_w[e], ((0, SUBLANE - CONV_W), (0, 0))), conv_b=conv_b[e][None, :],
        dt_bias=_pad_lanes(dt_bias[e][None, :]), a_log_row=_pad_lanes(a_log[e][None, :]),
        a_log_col=a_log[e][:, None], d_skip=jnp.repeat(d_skip[e], P_B)[None, :],
        norm_g=ssd_norm_g[e][None, :], expand=jnp.asarray(expand, bf16),
        w_out=w_out_a[e].astype(bf16))


def _odd_weights(o, w_in_c, b_igate_c, b_fgate_c, mlstm_norm_g, w_out_c):
    wi = w_in_c[o]
    o_o = 2 * D_QK_C + D_V_C
    o_g = o_o + D_V_C
    scale = jnp.concatenate([jnp.ones((D_QK_C,), f32), jnp.full((D_QK_C,), DK_C ** -0.5, f32),
                             jnp.ones((D_V_C,), f32)])[None, :]
    return dict(
        w_qkv=wi[:, :o_o].astype(bf16), qkv_scale=scale, w_o=wi[:, o_o:o_g].astype(bf16),
        w_gates=_pad_lanes(wi[:, o_g:]).astype(bf16),
        gate_bias=_pad_lanes(jnp.concatenate([b_igate_c[o], b_fgate_c[o]])[None, :]),
        norm_g=mlstm_norm_g[o][None, :], w_out=w_out_c[o].astype(bf16))


def kernel(x_prompt, x_sample, cache_fox_k, cache_fox_v, cache_fox_logf, state_ssd_conv, state_ssd, state_mlstm_c, state_mlstm_n, state_mlstm_m, meta_tokens, w_in_a, b_fgate_a, conv_w, conv_b, dt_bias, a_log, d_skip, ssd_norm_g, w_out_a, w_in_c, b_igate_c, b_fgate_c, mlstm_norm_g, w_out_c, ln_mix_g, ln_mix_b, ln_ffn_g, ln_ffn_b, w_ffn_gate, w_ffn_up, w_ffn_down):
    n_p, t_main, d = x_prompt.shape
    n_s, t_s, _ = x_sample.shape
    assert t_s == N_META, "aux rows hold equal-length meta and running-stream sequences"
    xm_f = x_prompt.reshape(n_p * t_main, d)
    xa_f = jnp.concatenate([jnp.broadcast_to(meta_tokens.astype(x_prompt.dtype)[None], (n_p, N_META, d)),
                            x_sample], axis=0).reshape((n_p + n_s) * t_s, d)
    xm_b, xa_b = xm_f.astype(bf16), xa_f.astype(bf16)

    even_out, odd_out = [], []
    for layer in range(DEPTH):
        if layer % 2 == 0:
            e = layer // 2
            w = _even_weights(e, w_in_a, b_fgate_a, conv_w, conv_b, dt_bias, a_log, d_skip, ssd_norm_g, w_out_a)
            cache = (cache_fox_k[e], cache_fox_v[e], cache_fox_logf[e], state_ssd_conv[e], state_ssd[e])
            w["ln_g"], w["ln_b"] = ln_mix_g[layer], ln_mix_b[layer]
            (x1m_f, x1m_b, x1a_f, x1a_b), new = _even_layer(xm_b, xm_f, xa_b, xa_f, n_p, t_main, cache, w)
            even_out.append(new)
        else:
            o = layer // 2
            w = _odd_weights(o, w_in_c, b_igate_c, b_fgate_c, mlstm_norm_g, w_out_c)
            cache = (state_mlstm_c[o], state_mlstm_n[o], state_mlstm_m[o])
            w["ln_g"], w["ln_b"] = ln_mix_g[layer], ln_mix_b[layer]
            (x1m_f, x1m_b, x1a_f, x1a_b), new = _odd_layer(xm_b, xm_f, xa_b, xa_f, n_p, t_main, cache, w)
            odd_out.append(new)
        wg, wu, wd = w_ffn_gate[layer].astype(bf16), w_ffn_up[layer].astype(bf16), w_ffn_down[layer].astype(bf16)
        xm_f, xm_b = _ffn_ln(x1m_b, x1m_f, wg, wu, wd, ln_ffn_g[layer], ln_ffn_b[layer])
        xa_f, xa_b = _ffn_ln(x1a_b, x1a_f, wg, wu, wd, ln_ffn_g[layer], ln_ffn_b[layer])

    y_prompt = xm_f.reshape(n_p, t_main, d)
    y_sample = xa_f.reshape(n_p + n_s, t_s, d)[n_p:]
    st = lambda outs, key: jnp.stack([o[key] for o in outs])
    return (y_prompt, y_sample,
            st(even_out, "k_p"), st(even_out, "v_p"), st(even_out, "f_p"), st(even_out, "conv_p"), st(even_out, "h_p"),
            st(odd_out, "c_p"), st(odd_out, "n_p"), st(odd_out, "m_p"),
            st(even_out, "k_s"), st(even_out, "v_s"), st(even_out, "f_s"), st(even_out, "conv_s"), st(even_out, "h_s"),
            st(odd_out, "c_s"), st(odd_out, "n_s"), st(odd_out, "m_s"))
```

```python
import functools

import jax
import jax.numpy as jnp
import numpy as np
from jax import lax
from jax.experimental import pallas as pl
from jax.experimental.pallas import tpu as pltpu

f32 = jnp.float32
bf16 = jnp.bfloat16

D_MODEL = 2048
DEPTH = 2
N_META = 16
H_A, DH_A = 16, 128
D_A = H_A * DH_A
H_B, P_B, G_B, N_B = 32, 64, 4, 128
HG_B = H_B // G_B
D_SSM = H_B * P_B
CONV_W = 4
CONV_DIM = D_SSM + 2 * G_B * N_B
H_C, DK_C, DV_C = 8, 128, 256
D_QK_C = H_C * DK_C
D_V_C = H_C * DV_C
ALPHA = (2 * DEPTH) ** 0.25
NEG = -1e30
LN_EPS = 1e-5
LOG2E = 1.4426950408889634

LANE = 128
SUBLANE = 8
SEQ_PAD = 128
VMEM_LIMIT = 56 * 1024 * 1024


def _cparams(sem):
    return pltpu.CompilerParams(dimension_semantics=sem, vmem_limit_bytes=VMEM_LIMIT)


def _tile(n, pref, mult=16):
    if n <= pref:
        return n
    t = (pref // mult) * mult
    while t >= mult:
        if n % t == 0:
            return t
        t -= mult
    return n


def _softplus(x):
    return jnp.maximum(x, 0.0) + jnp.log1p(jnp.exp(-jnp.abs(x)))


def _log_sigmoid(x):
    return -_softplus(-x)


def _sigmoid(x):
    return 1.0 / (1.0 + jnp.exp(-x))


def _split3(a):
    a1 = a.astype(bf16)
    r1 = a - a1.astype(f32)
    a2 = r1.astype(bf16)
    a3 = (r1 - a2.astype(f32)).astype(bf16)
    return a1, a2, a3


_NN = (((1,), (0,)), ((), ()))
_NT = (((1,), (1,)), ((), ()))
_TN = (((0,), (0,)), ((), ()))


def _dot(a, b, dims=_NN):
    return lax.dot_general(a, b, dims, preferred_element_type=f32)


def _dot_exact_rhs(a_bf16_exact, b_f32, dims=_NN):
    b1, b2, b3 = _split3(b_f32)
    return _dot(a_bf16_exact, b1, dims) + _dot(a_bf16_exact, b2, dims) + _dot(a_bf16_exact, b3, dims)


def _dot_exact_lhs(a_f32, b_bf16_exact, dims=_NN):
    a1, a2, a3 = _split3(a_f32)
    return _dot(a1, b_bf16_exact, dims) + _dot(a2, b_bf16_exact, dims) + _dot(a3, b_bf16_exact, dims)


def _iota2(shape, axis):
    return lax.broadcasted_iota(jnp.int32, shape, axis)


def _tri_lower(n):
    return (_iota2((n, n), 0) >= _iota2((n, n), 1)).astype(bf16)


def _tri_upper(n):
    return (_iota2((n, n), 0) <= _iota2((n, n), 1)).astype(bf16)


def _eye(rows, cols):
    return (_iota2((rows, cols), 0) == _iota2((rows, cols), 1)).astype(bf16)


def _layer_norm(x, g, b):
    mu = jnp.mean(x, axis=-1, keepdims=True)
    xc = x - mu
    var = jnp.mean(xc * xc, axis=-1, keepdims=True)
    return xc * lax.rsqrt(var + LN_EPS) * g + b


def _mm_kernel(x_ref, w_ref, s_ref, *out_refs):
    acc = _dot(x_ref[...], w_ref[...]) * s_ref[...]
    for o in out_refs:
        o[...] = acc.astype(o.dtype)


def _matmul(x, w, out_dtypes, col_scale=None, tm_pref=1024, tn_pref=1024):
    m, k = x.shape
    n = w.shape[1]
    tm = _tile(m, tm_pref)
    tn = _tile(n, tn_pref, LANE)
    if col_scale is None:
        col_scale = jnp.ones((1, n), f32)
    return pl.pallas_call(
        _mm_kernel,
        grid=(m // tm, n // tn),
        in_specs=[pl.BlockSpec((tm, k), lambda i, j: (i, 0)),
                  pl.BlockSpec((k, tn), lambda i, j: (0, j)),
                  pl.BlockSpec((1, tn), lambda i, j: (0, j))],
        out_specs=[pl.BlockSpec((tm, tn), lambda i, j: (i, j)) for _ in out_dtypes],
        out_shape=[jax.ShapeDtypeStruct((m, n), dt) for dt in out_dtypes],
        compiler_params=_cparams(("parallel", "arbitrary")),
        name="proj_matmul",
    )(x, w, col_scale)


def _kv_proj_kernel(x_ref, w_ref, base_ref, o_ref, ob_ref, *, tm, nh):
    del base_ref
    acc = _dot(x_ref[...], w_ref[...])
    ob_ref[...] = acc.astype(bf16)
    for h in range(nh):
        o_ref[0, pl.ds(h, tm, stride=nh), :] = acc[:, h * DH_A:(h + 1) * DH_A]


def _kv_proj(x, w, head_rows, tm_pref=512):
    m, k = x.shape
    n = w.shape[1]
    n_seq, row_off, nh, _ = head_rows.shape
    t_out = row_off + m // n_seq
    tm = _tile(m // n_seq, tm_pref)
    per_seq = m // n_seq // tm
    base = jnp.pad(head_rows.reshape(n_seq, row_off * nh, DH_A), ((0, 0), (0, (t_out - row_off) * nh), (0, 0)))
    o, ob = pl.pallas_call(
        functools.partial(_kv_proj_kernel, tm=tm, nh=nh),
        grid=(m // tm,),
        in_specs=[pl.BlockSpec((tm, k), lambda i: (i, 0)), pl.BlockSpec((k, n), lambda i: (0, 0)),
                  pl.BlockSpec(memory_space=pl.ANY)],
        out_specs=[pl.BlockSpec((pl.Element(1), pl.Element(tm * nh), pl.Element(DH_A)),
                                lambda i: (i // per_seq,
                                           pl.multiple_of((row_off + (i % per_seq) * tm) * nh, SUBLANE), 0)),
                   pl.BlockSpec((tm, n), lambda i: (i, 0))],
        out_shape=[jax.ShapeDtypeStruct((n_seq, t_out * nh, DH_A), f32), jax.ShapeDtypeStruct((m, n), bf16)],
        input_output_aliases={2: 0},
        compiler_params=_cparams(("parallel",)),
        name="kv_proj",
    )(x, w, base)
    return o.reshape(n_seq, t_out, nh, DH_A), ob


def _outproj_ln_kernel(*refs, n_act):
    acts = refs[:n_act]
    w_ref, res_ref, g_ref, b_ref, of_ref, ob_ref = refs[n_act:]
    kw = acts[0].shape[1]
    mix = _dot(acts[0][...], w_ref[0:kw, :])
    for a in range(1, n_act):
        mix = mix + _dot(acts[a][...], w_ref[a * kw:(a + 1) * kw, :])
    y = _layer_norm(ALPHA * res_ref[...] + mix, g_ref[...], b_ref[...])
    of_ref[...] = y
    ob_ref[...] = y.astype(bf16)


def _outproj_ln(acts, w, res, g, b, tm_pref=512):
    m, d = res.shape
    tm = _tile(m, tm_pref)
    n_act = len(acts)
    row = lambda width: pl.BlockSpec((tm, width), lambda i: (i, 0))
    once = lambda shape: pl.BlockSpec(shape, lambda i: (0, 0), pipeline_mode=pl.Buffered(1))
    return pl.pallas_call(
        functools.partial(_outproj_ln_kernel, n_act=n_act),
        grid=(m // tm,),
        in_specs=[row(a.shape[1]) for a in acts] + [once(w.shape), row(d), once((1, d)), once((1, d))],
        out_specs=[row(d), row(d)],
        out_shape=[jax.ShapeDtypeStruct((m, d), f32), jax.ShapeDtypeStruct((m, d), bf16)],
        compiler_params=_cparams(("parallel",)),
        name="outproj_ln",
    )(*acts, w, res, g.reshape(1, d), b.reshape(1, d))


def _ffn_ln_kernel(xb_ref, xf_ref, wg_ref, wu_ref, wd_ref, g_ref, b_ref, of_ref, ob_ref, acc_ref):
    f = pl.program_id(1)

    @pl.when(f == 0)
    def _():
        acc_ref[...] = jnp.zeros_like(acc_ref)

    x = xb_ref[...]
    gate = _dot(x, wg_ref[...])
    up = _dot(x, wu_ref[...])
    h = (gate * _sigmoid(gate) * up).astype(bf16)
    acc_ref[...] += _dot(h, wd_ref[...])

    @pl.when(f == pl.num_programs(1) - 1)
    def _():
        y = _layer_norm(ALPHA * xf_ref[...] + acc_ref[...], g_ref[...], b_ref[...])
        of_ref[...] = y
        ob_ref[...] = y.astype(bf16)


def _ffn_ln(xb, xf, wg, wu, wd, g, b, tm_pref=512, tf_pref=512):
    m, d = xf.shape
    dff = wg.shape[1]
    tm = _tile(m, tm_pref)
    tf = _tile(dff, tf_pref, LANE)
    return pl.pallas_call(
        _ffn_ln_kernel,
        grid=(m // tm, dff // tf),
        in_specs=[pl.BlockSpec((tm, d), lambda i, f: (i, 0)),
                  pl.BlockSpec((tm, d), lambda i, f: (i, 0)),
                  pl.BlockSpec((d, tf), lambda i, f: (0, f)),
                  pl.BlockSpec((d, tf), lambda i, f: (0, f)),
                  pl.BlockSpec((tf, d), lambda i, f: (f, 0)),
                  pl.BlockSpec((1, d), lambda i, f: (0, 0)),
                  pl.BlockSpec((1, d), lambda i, f: (0, 0))],
        out_specs=[pl.BlockSpec((tm, d), lambda i, f: (i, 0)),
                   pl.BlockSpec((tm, d), lambda i, f: (i, 0))],
        out_shape=[jax.ShapeDtypeStruct((m, d), f32), jax.ShapeDtypeStruct((m, d), bf16)],
        scratch_shapes=[pltpu.VMEM((tm, d), f32)],
        compiler_params=_cparams(("parallel", "arbitrary")),
        name="ffn_ln",
    )(xb, xf, wg, wu, wd, g.reshape(1, d), b.reshape(1, d))


FORGET_PIECES = 3


def _cumsum_kernel(x_ref, bias_ref, carry_ref, logf_ref, cum_ref, cols_ref, acc_ref, *, gate):
    c = pl.program_id(1)

    @pl.when(c == 0)
    def _():
        acc_ref[...] = carry_ref[0]

    x = x_ref[0]
    lf = _log_sigmoid(x + bias_ref[...]) if gate else x
    tc = x.shape[0]
    cum = _dot_exact_rhs(_tri_lower(tc), lf) + acc_ref[...]
    logf_ref[0] = lf
    cum_ref[0] = cum
    acc_ref[...] = cum[tc - 1:tc, :]
    r, col = _iota2((LANE, LANE), 0), _iota2((LANE, LANE), 1)
    out = None
    for i, piece in enumerate(_split3(cum * LOG2E)):
        t = _dot(piece, ((col == FORGET_PIECES * r + i) & (r < H_A)).astype(bf16))
        out = t if out is None else out + t
    cols_ref[0] = out.astype(bf16)


def _logf_cumsum(x, bias, carry, gate, tc_pref=512):
    bsz, t, w = x.shape
    tc = _tile(t, tc_pref)
    spec = pl.BlockSpec((1, tc, w), lambda b, c: (b, c, 0))
    return pl.pallas_call(
        functools.partial(_cumsum_kernel, gate=gate),
        grid=(bsz, t // tc),
        in_specs=[spec,
                  pl.BlockSpec((1, w), lambda b, c: (0, 0)),
                  pl.BlockSpec((1, 1, w), lambda b, c: (b, 0, 0))],
        out_specs=[spec, spec, spec],
        out_shape=[jax.ShapeDtypeStruct((bsz, t, w), f32)] * 2 + [jax.ShapeDtypeStruct((bsz, t, w), bf16)],
        scratch_shapes=[pltpu.VMEM((1, w), f32)],
        compiler_params=_cparams(("parallel", "arbitrary")),
        name="logf_cumsum",
    )(x, bias, carry)


def _attn_kernel(*refs, tq, tk, tk2, s1, s1_valid, hps):
    if s1:
        q_ref, k1_ref, v1_ref, f1_ref, k2_ref, v2_ref, f2_ref, o_ref = refs
    else:
        q_ref, k2_ref, v2_ref, f2_ref, o_ref = refs
    qi = pl.program_id(2)
    d = DH_A

    for hh in range(hps):
        cols = slice(hh * d, (hh + 1) * d)
        q = q_ref[0, :, cols]

        def scores(k, q=q):
            return _dot(q, k.astype(bf16), _NT)

        def update(carry, s, v, fk, mask):
            m, l, acc = carry
            s = s - fk * LOG2E
            if mask is not None:
                s = jnp.where(mask, s, NEG)
            m_new = jnp.maximum(m, jnp.max(s, axis=-1, keepdims=True))
            alpha = jnp.exp2(m - m_new)
            p = jnp.exp2(s - m_new)
            l = alpha * l + jnp.sum(p, axis=-1, keepdims=True)
            acc = alpha * acc + _dot(p.astype(bf16), v.astype(bf16))
            return m_new, l, acc

        def step(j, c, mask, row0=0, cols=cols, hh=hh, q=q, update=update):
            off = pl.multiple_of(j * tk, tk)
            s = _dot(q[row0:], k2_ref[0, pl.ds(off, tk), cols].astype(bf16), _NT)
            new = update(tuple(a[row0:] for a in c), s, v2_ref[0, pl.ds(off, tk), cols],
                         f2_ref[0, hh, :, pl.ds(off, tk)], mask)
            if row0 == 0:
                return new
            return tuple(jnp.concatenate([a[:row0], b], axis=0) for a, b in zip(c, new))

        carry = (jnp.full((tq, 1), NEG, f32), jnp.zeros((tq, 1), f32), jnp.zeros((tq, d), f32))
        if s1:
            pmask = None if s1_valid == s1 else (_iota2((tq, s1), 1) < s1_valid)
            carry = update(carry, scores(k1_ref[0, :, cols]), v1_ref[0, :, cols], f1_ref[0, hh], pmask)

        nd = tk2 // tk
        carry = lax.fori_loop(0, qi * nd, lambda j, c, step=step: step(j, c, None), carry)
        for r in range(nd):
            row0 = r * tk if tq == tk2 else 0
            causal = _iota2((tq - row0, tk), 0) + row0 >= _iota2((tq - row0, tk), 1) + r * tk
            carry = step(qi * nd + r, carry, causal, row0)
        m, l, acc = carry
        o_ref[0, :, cols] = (acc / l).astype(o_ref.dtype)


def _fox_attention(q, k2, v2, f2, prefix=None, s1_valid=0, tq=512, tk=512, hps=1):
    bsz, t, hd = q.shape
    t2 = k2.shape[1]
    nq = t // tq
    tk2 = t2 // nq
    w = hps * DH_A
    qspec = pl.BlockSpec((1, tq, w), lambda b, h, i: (b, i, h))
    kvspec = pl.BlockSpec((1, t2, w), lambda b, h, i: (b, 0, h))
    fspec = pl.BlockSpec((1, hps, 1, t2), lambda b, h, i: (b, h, 0, 0))
    args, in_specs = [q], [qspec]
    s1 = 0
    if prefix is not None:
        k1, v1, f1 = prefix
        s1 = k1.shape[1]
        p_spec = pl.BlockSpec((1, s1, w), lambda b, h, i: (b, 0, h))
        args += [k1, v1, f1]
        in_specs += [p_spec, p_spec, pl.BlockSpec((1, hps, 1, s1), lambda b, h, i: (b, h, 0, 0))]
    args += [k2, v2, f2]
    in_specs += [kvspec, kvspec, fspec]
    return pl.pallas_call(
        functools.partial(_attn_kernel, tq=tq, tk=min(tk, tk2), tk2=tk2, s1=s1, s1_valid=s1_valid, hps=hps),
        grid=(bsz, hd // w, nq),
        in_specs=in_specs,
        out_specs=qspec,
        out_shape=jax.ShapeDtypeStruct((bsz, t, hd), bf16),
        compiler_params=_cparams(("parallel", "parallel", "arbitrary")),
        name="fox_attention",
    )(*args)


def _attn_cache_kernel(q_ref, k1_ref, v1_ref, f1_ref, k2_ref, v2_ref, f2_ref, o_ref, m_ref, l_ref, acc_ref,
                       *, tk, nh):
    j = pl.program_id(1)
    d = DH_A
    tq = q_ref.shape[1]

    @pl.when(j == 0)
    def _():
        m_ref[...] = jnp.full(m_ref.shape, NEG, f32)
        l_ref[...] = jnp.zeros(l_ref.shape, f32)
        acc_ref[...] = jnp.zeros(acc_ref.shape, f32)

    def update(h, s, v, fk, mask):
        s = s - fk * LOG2E
        if mask is not None:
            s = jnp.where(mask, s, NEG)
        m = m_ref[h][:, 0:1]
        m_new = jnp.maximum(m, jnp.max(s, axis=-1, keepdims=True))
        alpha = jnp.exp2(m - m_new)
        p = jnp.exp2(s - m_new)
        l_ref[h] = jnp.broadcast_to(alpha * l_ref[h][:, 0:1] + jnp.sum(p, axis=-1, keepdims=True), (tq, LANE))
        acc_ref[h] = alpha * acc_ref[h] + _dot(p.astype(bf16), v)
        m_ref[h] = jnp.broadcast_to(m_new, (tq, LANE))

    qs = [q_ref[0, :, h * d:(h + 1) * d] for h in range(nh)]
    ss = [_dot(qs[h], k1_ref[0, pl.ds(h, tk, stride=nh), :].astype(bf16), _NT) for h in range(nh)]
    for h in range(nh):
        update(h, ss[h], v1_ref[0, pl.ds(h, tk, stride=nh), :].astype(bf16), f1_ref[0, h], None)

    @pl.when(j == pl.num_programs(1) - 1)
    def _():
        t2 = k2_ref.shape[1]
        causal = _iota2((tq, t2), 0) >= _iota2((tq, t2), 1)
        for h in range(nh):
            cols = slice(h * d, (h + 1) * d)
            update(h, _dot(qs[h], k2_ref[0, :, cols], _NT), v2_ref[0, :, cols], f2_ref[0, h], causal)
            o_ref[0, :, cols] = (acc_ref[h] / l_ref[h][:, 0:1]).astype(o_ref.dtype)


def _fox_attention_cache(q, k1, v1, f1, k2, v2, f2, tk=512):
    bsz, tq, hd = q.shape
    s, nh = k1.shape[1], k1.shape[2]
    t2 = k2.shape[1]
    flat = lambda a: a.reshape(bsz, s * nh, DH_A)
    qspec = pl.BlockSpec((1, tq, hd), lambda b, j: (b, 0, 0))
    cspec = pl.BlockSpec((1, tk * nh, DH_A), lambda b, j: (b, j, 0))
    nspec = pl.BlockSpec((1, t2, hd), lambda b, j: (b, 0, 0))
    return pl.pallas_call(
        functools.partial(_attn_cache_kernel, tk=tk, nh=nh),
        grid=(bsz, s // tk),
        in_specs=[qspec, cspec, cspec, pl.BlockSpec((1, nh, 1, tk), lambda b, j: (b, 0, 0, j)),
                  nspec, nspec, pl.BlockSpec((1, nh, 1, t2), lambda b, j: (b, 0, 0, 0))],
        out_specs=qspec,
        out_shape=jax.ShapeDtypeStruct((bsz, tq, hd), bf16),
        scratch_shapes=[pltpu.VMEM((nh, tq, LANE), f32), pltpu.VMEM((nh, tq, LANE), f32),
                        pltpu.VMEM((nh, tq, DH_A), f32)],
        compiler_params=_cparams(("parallel", "arbitrary")),
        name="fox_attention_cache",
    )(q, flat(k1), flat(v1), f1, k2, v2, f2)


def _attn_t_kernel(q_ref, k1_ref, v1_ref, c1_ref, k2_ref, v2_ref, c2_ref, o_ref, *, tq, s1_valid, hps):
    hg = pl.program_id(1)
    qi = pl.program_id(2)
    d = DH_A

    lane = _iota2((tq, LANE), 1)
    q_aug = []
    for hh in range(hps):
        lo = FORGET_PIECES * (hg * hps + hh)
        pick = jnp.where((lane >= lo) & (lane < lo + FORGET_PIECES), -1.0, 0.0).astype(bf16)
        q_aug.append(jnp.concatenate([q_ref[0, :, hh * d:(hh + 1) * d], pick], axis=-1))

    def update(carry, s, v, mask):
        m, l, acc = carry
        if mask is not None:
            s = jnp.where(mask, s, NEG)
        m_new = jnp.maximum(m, jnp.max(s, axis=0, keepdims=True))
        alpha = jnp.exp2(m - m_new)
        p = jnp.exp2(s - m_new)
        l = alpha * l + jnp.sum(p, axis=0, keepdims=True)
        acc = alpha * acc + _dot(v, p.astype(bf16), _TN)
        return m_new, l, acc

    def multi(carries, k_augs, vs, mask):
        ss = [_dot(k_augs[hh], q_aug[hh], _NT) for hh in range(hps)]
        return [update(carries[hh], ss[hh], vs[hh], mask) for hh in range(hps)]

    carries = [(jnp.full((1, tq), NEG, f32), jnp.zeros((1, tq), f32), jnp.zeros((d, tq), f32))
               for _ in range(hps)]
    k1_aug = [jnp.concatenate([k1_ref[0, 0:s1_valid, hh * d:(hh + 1) * d], c1_ref[0, 0:s1_valid, :]], axis=-1)
              for hh in range(hps)]
    carries = multi(carries, k1_aug, [v1_ref[0, 0:s1_valid, hh * d:(hh + 1) * d] for hh in range(hps)], None)

    def blk(j, c, mask):
        off = pl.multiple_of(j * tq, tq)
        cols = c2_ref[0, pl.ds(off, tq), :]
        return multi(c, [jnp.concatenate([k2_ref[0, pl.ds(off, tq), hh * d:(hh + 1) * d], cols], axis=-1)
                         for hh in range(hps)],
                     [v2_ref[0, pl.ds(off, tq), hh * d:(hh + 1) * d] for hh in range(hps)], mask)

    carries = lax.fori_loop(0, qi, lambda j, c: blk(j, c, None), carries)
    carries = blk(qi, carries, _iota2((tq, tq), 0) <= _iota2((tq, tq), 1))
    for hh in range(hps):
        m, l, acc = carries[hh]
        o_ref[0, :, hh * d:(hh + 1) * d] = jnp.transpose(acc / l).astype(o_ref.dtype)


def _fox_attention_main(q, k1, v1, c1, k2, v2, c2, s1_valid, tq=512, hps=4):
    bsz, t, hd = q.shape
    s1 = k1.shape[1]
    w = hps * DH_A
    qspec = pl.BlockSpec((1, tq, w), lambda b, h, i: (b, i, h))
    kvspec = pl.BlockSpec((1, t, w), lambda b, h, i: (b, 0, h))
    pspec = pl.BlockSpec((1, s1, w), lambda b, h, i: (b, 0, h))
    return pl.pallas_call(
        functools.partial(_attn_t_kernel, tq=tq, s1_valid=s1_valid, hps=hps),
        grid=(bsz, hd // w, t // tq),
        in_specs=[qspec, pspec, pspec, pl.BlockSpec((1, s1, LANE), lambda b, h, i: (b, 0, 0)),
                  kvspec, kvspec, pl.BlockSpec((1, t, LANE), lambda b, h, i: (b, 0, 0))],
        out_specs=qspec,
        out_shape=jax.ShapeDtypeStruct((bsz, t, hd), bf16),
        compiler_params=_cparams(("parallel", "parallel", "arbitrary")),
        name="fox_attention_main",
    )(q, k1, v1, c1, k2, v2, c2)


def _ssd_kernel(zx_ref, dt_ref, hist_ref, h0_ref, cw_ref, cb_ref, dtb_ref, alog_ref, alogc_ref,
                dskip_ref, ng_ref, e_ref, y_ref, tail_ref, hout_ref, ext_ref, st_ref, *, lc, t_valid):
    c = pl.program_id(1)

    @pl.when(c == 0)
    def _():
        ext_ref[0:SUBLANE, :] = hist_ref[0]
        st_ref[...] = h0_ref[0]

    ext_ref[SUBLANE:SUBLANE + lc, :] = zx_ref[0, :, D_SSM:]
    conv = cb_ref[...]
    for i in range(CONV_W):
        lo = SUBLANE - (CONV_W - 1) + i
        conv = conv + cw_ref[i:i + 1, :] * ext_ref[lo:lo + lc, :]
    xc = conv * _sigmoid(conv)
    tail = ext_ref[t_valid:t_valid + SUBLANE, :]
    ext_ref[0:SUBLANE, :] = tail

    dt = _softplus(dt_ref[0] + dtb_ref[...])
    if t_valid < lc:
        dt = jnp.where(_iota2((lc, LANE), 0) < t_valid, dt, 0.0)
    a_row = -jnp.exp(alog_ref[...])
    a_col = -jnp.exp(alogc_ref[...])
    a_cum = _dot_exact_rhs(_tri_lower(lc), dt * a_row)
    dt_t = _dot_exact_rhs(_eye(H_B, LANE), dt, _NT)
    a_cum_t = _dot_exact_lhs(dt_t * a_col, _tri_upper(lc))
    a_last = a_cum[lc - 1:lc, :]
    e = e_ref[...]
    to_end_x = _dot_exact_lhs(jnp.exp(a_last - a_cum) * dt, e)
    ea_x = _dot_exact_lhs(jnp.exp(a_cum), e)
    cdec_x = _dot_exact_lhs(jnp.broadcast_to(jnp.exp(a_last), (SUBLANE, LANE)), e)[0:1, :]

    causal = _iota2((lc, lc), 0) >= _iota2((lc, lc), 1)
    left = _iota2((lc, LANE), 1) < P_B
    eye_n = _eye(N_B, N_B)
    gw = D_SSM // G_B
    groups = range(G_B)
    gcols = [slice(g * gw, (g + 1) * gw) for g in groups]
    bg = [xc[:, D_SSM + g * N_B:D_SSM + (g + 1) * N_B].astype(bf16) for g in groups]
    cg = [xc[:, D_SSM + G_B * N_B + g * N_B:D_SSM + G_B * N_B + (g + 1) * N_B].astype(bf16) for g in groups]
    cb = [_dot(cg[g], bg[g], _NT) for g in groups]
    bg_t = [_dot(eye_n, bg[g], _NT).astype(bf16) for g in groups]
    st_in = [st_ref[:, gcols[g]] for g in groups]
    y_off = [_dot(cg[g], st_in[g].astype(bf16)) * ea_x[:, gcols[g]] for g in groups]
    upd = [_dot(bg_t[g], (xc[:, gcols[g]] * to_end_x[:, gcols[g]]).astype(bf16)) for g in groups]
    for g in groups:
        st_ref[:, gcols[g]] = cdec_x[:, gcols[g]] * st_in[g] + upd[g]

    def decay_weights(h):
        seg = a_cum[:, h:h + 1] - a_cum_t[h:h + 1, :]
        dec = jnp.exp(jnp.where(causal, seg, NEG))
        return (cb[h // HG_B] * dec * dt_t[h:h + 1, :]).astype(bf16)

    w_all = [decay_weights(h) for h in range(H_B)]
    y_pair = []
    for j in range(D_SSM // LANE):
        xp = xc[:, j * LANE:(j + 1) * LANE].astype(bf16)
        y_pair.append(jnp.where(left, _dot(w_all[2 * j], xp), _dot(w_all[2 * j + 1], xp)))
    ppg = gw // LANE
    for g in groups:
        xg = xc[:, gcols[g]]
        y = jnp.concatenate(y_pair[g * ppg:(g + 1) * ppg], axis=-1) + y_off[g] + dskip_ref[:, gcols[g]] * xg
        z = zx_ref[0, :, g * gw:(g + 1) * gw]
        y = y * (z * _sigmoid(z))
        y = y * lax.rsqrt(jnp.mean(y * y, axis=-1, keepdims=True) + LN_EPS) * ng_ref[:, g * gw:(g + 1) * gw]
        y_ref[0, :, g * gw:(g + 1) * gw] = y.astype(y_ref.dtype)

    @pl.when(c == pl.num_programs(1) - 1)
    def _():
        tail_ref[0] = tail
        hout_ref[0] = st_ref[...]


def _ssd_mixer(zx, dt_raw, hist, h0_t, p, lc, t_valid):
    bsz, t, wz = zx.shape
    const = lambda shape: pl.BlockSpec(shape, lambda b, c: (0,) * len(shape))
    return pl.pallas_call(
        functools.partial(_ssd_kernel, lc=lc, t_valid=t_valid),
        grid=(bsz, t // lc),
        in_specs=[pl.BlockSpec((1, lc, wz), lambda b, c: (b, c, 0)),
                  pl.BlockSpec((1, lc, LANE), lambda b, c: (b, c, 0)),
                  pl.BlockSpec((1, SUBLANE, CONV_DIM), lambda b, c: (b, 0, 0)),
                  pl.BlockSpec((1, N_B, D_SSM), lambda b, c: (b, 0, 0)),
                  const((SUBLANE, CONV_DIM)), const((1, CONV_DIM)), const((1, LANE)), const((1, LANE)),
                  const((H_B, 1)), const((1, D_SSM)), const((1, D_SSM)), const((LANE, D_SSM))],
        out_specs=[pl.BlockSpec((1, lc, D_SSM), lambda b, c: (b, c, 0)),
                   pl.BlockSpec((1, SUBLANE, CONV_DIM), lambda b, c: (b, 0, 0)),
                   pl.BlockSpec((1, N_B, D_SSM), lambda b, c: (b, 0, 0))],
        out_shape=[jax.ShapeDtypeStruct((bsz, t, D_SSM), bf16),
                   jax.ShapeDtypeStruct((bsz, SUBLANE, CONV_DIM), f32),
                   jax.ShapeDtypeStruct((bsz, N_B, D_SSM), f32)],
        scratch_shapes=[pltpu.VMEM((SUBLANE + lc, CONV_DIM), f32), pltpu.VMEM((N_B, D_SSM), f32)],
        compiler_params=_cparams(("parallel", "arbitrary")),
        name="ssd_mixer",
    )(zx, dt_raw, hist, h0_t, p["conv_w"], p["conv_b"], p["dt_bias"], p["a_log_row"], p["a_log_col"],
      p["d_skip"], p["norm_g"], p["expand"])


def _mlstm_kernel(qkv_ref, o_ref, gates_ref, gb_ref, ng_ref, c0_ref, n0_ref, m0_ref,
                  y_ref, cout_ref, nout_ref, mout_ref, ct_ref, n_ref, m_ref, *, lc, t_valid):
    c = pl.program_id(1)

    @pl.when(c == 0)
    def _():
        ct_ref[...] = c0_ref[0]
        n_ref[...] = n0_ref[0]
        m_ref[...] = m0_ref[0]

    lane = _iota2((lc, LANE), 1)
    g = gates_ref[0] + gb_ref[...]
    is_f = (lane >= H_C) & (lane < 2 * H_C)
    lf = jnp.where(is_f, _log_sigmoid(g), 0.0)
    li = g
    if t_valid < lc:
        valid = _iota2((lc, LANE), 0) < t_valid
        lf = jnp.where(valid, lf, 0.0)
        li = jnp.where(valid, li, NEG)
    b_cum = _dot_exact_rhs(_tri_lower(lc), lf)
    rows = 2 * H_C
    li_t = _dot_exact_rhs(_eye(rows, LANE), li, _NT)
    lf_t = _dot_exact_rhs(_eye(rows, LANE), lf, _NT)
    b_cum_t = _dot_exact_lhs(lf_t, _tri_upper(lc))

    causal = _iota2((lc, lc), 0) >= _iota2((lc, lc), 1)
    eye_k = _eye(DK_C, DK_C)
    m_all = m_ref[...]
    m_out = m_all
    q_of = lambda h: qkv_ref[0, :, h * DK_C:(h + 1) * DK_C]
    k_of = lambda h: qkv_ref[0, :, D_QK_C + h * DK_C:D_QK_C + (h + 1) * DK_C]
    v_of = lambda h: qkv_ref[0, :, 2 * D_QK_C + h * DV_C:2 * D_QK_C + (h + 1) * DV_C]
    heads = range(H_C)
    qk_all = [_dot(q_of(h), k_of(h), _NT) for h in heads]
    qc_all = [_dot(q_of(h), ct_ref[h].astype(bf16)) for h in heads]
    kt_all = [_dot(eye_k, k_of(h), _NT) for h in heads]
    bcol = [b_cum[:, H_C + h:H_C + h + 1] for h in heads]
    brow = [b_cum_t[H_C + h:H_C + h + 1, :] for h in heads]
    lirow = [li_t[h:h + 1, :] for h in heads]
    m_st = [m_all[:, h:h + 1] for h in heads]
    d_mat = [jnp.where(causal, bcol[h] - brow[h] + lirow[h], NEG) for h in heads]
    inter = [bcol[h] + m_st[h] for h in heads]
    m_row = [jnp.maximum(inter[h], jnp.max(d_mat[h], axis=-1, keepdims=True)) for h in heads]
    w = [jnp.exp(d_mat[h] - m_row[h]) * qk_all[h] for h in heads]
    g_inter = [jnp.exp(inter[h] - m_row[h]) for h in heads]
    n_st = [n_ref[h:h + 1, :] for h in heads]
    num = [_dot(w[h].astype(bf16), v_of(h)) + g_inter[h] * qc_all[h] for h in heads]
    den = [jnp.sum(w[h], axis=-1, keepdims=True)
           + g_inter[h] * jnp.sum(q_of(h).astype(f32) * n_st[h], axis=-1, keepdims=True) for h in heads]
    hh = [num[h] / jnp.maximum(jnp.abs(den[h]), jnp.exp(-m_row[h])) for h in heads]
    ms = [jnp.mean(hh[h] * hh[h], axis=-1, keepdims=True) for h in heads]
    for h in heads:
        cols = slice(h * DV_C, (h + 1) * DV_C)
        y = hh[h] * lax.rsqrt(ms[h] + LN_EPS) * ng_ref[:, cols] * _sigmoid(o_ref[0, :, cols])
        y_ref[0, :, cols] = y.astype(y_ref.dtype)
    b_tot = [bcol[h][lc - 1:lc, :] for h in heads]
    d_end = [b_tot[h] - brow[h] + lirow[h] for h in heads]
    m_new = [jnp.maximum(b_tot[h] + m_st[h], jnp.max(d_end[h], axis=-1, keepdims=True)) for h in heads]
    w_end = [jnp.exp(d_end[h] - m_new[h]) for h in heads]
    g_old = [jnp.exp(b_tot[h] + m_st[h] - m_new[h]) for h in heads]
    for h in heads:
        ct_ref[h] = g_old[h] * ct_ref[h] + _dot((kt_all[h] * w_end[h]).astype(bf16), v_of(h))
        n_upd = _dot(jnp.broadcast_to(w_end[h], (SUBLANE, lc)).astype(bf16), k_of(h))[0:1, :]
        n_ref[h:h + 1, :] = g_old[h] * n_st[h] + n_upd
        m_out = jnp.where(lane[0:1, :] == h, m_new[h], m_out)
    m_ref[...] = m_out

    @pl.when(c == pl.num_programs(1) - 1)
    def _():
        cout_ref[0] = ct_ref[...]
        nout_ref[0] = n_ref[...]
        mout_ref[0] = m_ref[...]


def _mlstm_mixer(qkv, o, gates, c0_t, n0, m0, p, lc, t_valid):
    bsz, t, wq = qkv.shape
    const = lambda shape: pl.BlockSpec(shape, lambda b, c: (0,) * len(shape))
    return pl.pallas_call(
        functools.partial(_mlstm_kernel, lc=lc, t_valid=t_valid),
        grid=(bsz, t // lc),
        in_specs=[pl.BlockSpec((1, lc, wq), lambda b, c: (b, c, 0)),
                  pl.BlockSpec((1, lc, D_V_C), lambda b, c: (b, c, 0)),
                  pl.BlockSpec((1, lc, LANE), lambda b, c: (b, c, 0)),
                  const((1, LANE)), const((1, D_V_C)),
                  pl.BlockSpec((1, H_C, DK_C, DV_C), lambda b, c: (b, 0, 0, 0)),
                  pl.BlockSpec((1, H_C, DK_C), lambda b, c: (b, 0, 0)),
                  pl.BlockSpec((1, 1, LANE), lambda b, c: (b, 0, 0))],
        out_specs=[pl.BlockSpec((1, lc, D_V_C), lambda b, c: (b, c, 0)),
                   pl.BlockSpec((1, H_C, DK_C, DV_C), lambda b, c: (b, 0, 0, 0)),
                   pl.BlockSpec((1, H_C, DK_C), lambda b, c: (b, 0, 0)),
                   pl.BlockSpec((1, 1, LANE), lambda b, c: (b, 0, 0))],
        out_shape=[jax.ShapeDtypeStruct((bsz, t, D_V_C), bf16),
                   jax.ShapeDtypeStruct((bsz, H_C, DK_C, DV_C), f32),
                   jax.ShapeDtypeStruct((bsz, H_C, DK_C), f32),
                   jax.ShapeDtypeStruct((bsz, 1, LANE), f32)],
        scratch_shapes=[pltpu.VMEM((H_C, DK_C, DV_C), f32), pltpu.VMEM((H_C, DK_C), f32),
                        pltpu.VMEM((1, LANE), f32)],
        compiler_params=_cparams(("parallel", "arbitrary")),
        name="mlstm_mixer",
    )(qkv, o, gates, p["gate_bias"], p["norm_g"], c0_t, n0, m0)


def _pad_lanes(a, width=LANE):
    return jnp.pad(a, [(0, 0)] * (a.ndim - 1) + [(0, width - a.shape[-1])])


def _pad_seq(a, t_pad=SEQ_PAD):
    return jnp.pad(a, [(0, 0), (0, t_pad - a.shape[1])] + [(0, 0)] * (a.ndim - 2))


def _even_layer(xm_b, xm_f, xa_b, xa_f, n_p, t_main, cache, w):
    k_past, v_past, logf_past, conv_past, h_past = cache
    n_s = k_past.shape[0]
    n_aux = n_p + n_s
    t_aux = xa_f.shape[0] // n_aux
    past = k_past.shape[1]

    seq = lambda a, n: a.reshape(n, a.shape[0] // n, a.shape[1])
    heads = lambda a: a.reshape(a.shape[0], a.shape[1], H_A, DH_A)

    def project(xb, meta_kv):
        q, = _matmul(xb, w["w_q"], [bf16], col_scale=jnp.full((1, D_A), DH_A ** -0.5 * LOG2E, f32))
        if meta_kv is None:
            k_f, k_b = _matmul(xb, w["w_k"], [f32, bf16])
            v_f, v_b = _matmul(xb, w["w_v"], [f32, bf16])
        else:
            k_f, k_b = _kv_proj(xb, w["w_k"], meta_kv[0])
            v_f, v_b = _kv_proj(xb, w["w_v"], meta_kv[1])
        zx, = _matmul(xb, w["w_zx"], [f32])
        sm, = _matmul(xb, w["w_small"], [f32], tn_pref=2 * LANE)
        return q, k_f, k_b, v_f, v_b, zx, sm

    qa, ka_f, ka_b, va_f, va_b, zxa, sma = project(xa_b, None)
    qa, ka_f, va_f = [seq(a, n_aux) for a in (qa, ka_f, va_f)]
    ka_b, va_b, zxa, sma = [_pad_seq(seq(a, n_aux)) for a in (ka_b, va_b, zxa, sma)]
    ka_f, va_f = heads(ka_f), heads(va_f)
    qm, kp_f, km_b, vp_f, vm_b, zxm, smm = project(xm_b, (ka_f[:n_p], va_f[:n_p]))
    qm, km_b, vm_b, zxm, smm = [seq(a, n_p) for a in (qm, km_b, vm_b, zxm, smm)]

    zero_c = jnp.zeros((n_s, 1, LANE), f32)
    _, cum_past, _ = _logf_cumsum(_pad_lanes(logf_past.astype(f32)), w["b_f"], zero_c, gate=False)
    carry_a = jnp.concatenate([jnp.zeros((n_p, 1, LANE), f32), cum_past[:, past - 1:past]], axis=0)
    logf_a, cum_a, cols_a = _logf_cumsum(sma[:, :, LANE:], w["b_f"], carry_a, gate=True)
    logf_m, _, cols_m = _logf_cumsum(smm[:, :, LANE:], w["b_f"], cum_a[:n_p, t_aux - 1:t_aux], gate=True)
    rows = lambda cum: jnp.swapaxes(cum[:, :, :H_A], 1, 2)[:, :, None, :]

    fa = rows(cum_a)
    att_meta = _fox_attention(qa[:n_p], ka_b[:n_p], va_b[:n_p], fa[:n_p], tq=t_aux, hps=2)
    att_s = _fox_attention_cache(qa[n_p:], k_past, v_past, rows(cum_past), ka_b[n_p:], va_b[n_p:], fa[n_p:],
                                 tk=min(past, 512))
    att_m = _fox_attention_main(qm, ka_b[:n_p], va_b[:n_p], cols_a[:n_p], km_b, vm_b, cols_m, t_aux,
                                tq=min(t_main, 512))
    att_a = jnp.concatenate([att_meta, att_s], axis=0)

    hist_a = jnp.concatenate([jnp.zeros((n_p, SUBLANE, CONV_DIM), f32),
                              jnp.pad(conv_past.astype(f32), ((0, 0), (SUBLANE - (CONV_W - 1), 0), (0, 0)))], axis=0)
    to_t = lambda h: jnp.transpose(h.reshape(-1, D_SSM, N_B), (0, 2, 1))
    from_t = lambda h: jnp.transpose(h, (0, 2, 1)).reshape(-1, H_B, P_B, N_B)
    h0_a = jnp.concatenate([jnp.zeros((n_p, N_B, D_SSM), f32), to_t(h_past.astype(f32))], axis=0)
    ya, tail_a, h_a = _ssd_mixer(zxa, sma[:, :, :LANE], hist_a, h0_a, w, SEQ_PAD, t_aux)
    ym, tail_m, h_m = _ssd_mixer(zxm, smm[:, :, :LANE], tail_a[:n_p], h_a[:n_p], w, min(t_main, 128), min(t_main, 128))

    unseq = lambda a: a.reshape(-1, a.shape[-1])
    x1m_f, x1m_b = _outproj_ln([unseq(att_m), unseq(ym)], w["w_out"], xm_f, w["ln_g"], w["ln_b"])
    x1a_f, x1a_b = _outproj_ln([unseq(att_a), unseq(ya[:, :t_aux])], w["w_out"], xa_f, w["ln_g"], w["ln_b"])

    new = dict(
        k_p=kp_f, v_p=vp_f,
        f_p=jnp.concatenate([logf_a[:n_p, :t_aux, :H_A], logf_m[:, :, :H_A]], axis=1),
        conv_p=tail_m[:, SUBLANE - (CONV_W - 1):], h_p=from_t(h_m),
        k_s=ka_f[n_p:], v_s=va_f[n_p:], f_s=logf_a[n_p:, :t_aux, :H_A],
        conv_s=tail_a[n_p:, SUBLANE - (CONV_W - 1):], h_s=from_t(h_a[n_p:]))
    return (x1m_f, x1m_b, x1a_f, x1a_b), new


def _odd_layer(xm_b, xm_f, xa_b, xa_f, n_p, t_main, cache, w):
    c_past, n_past, m_past = cache
    n_s = c_past.shape[0]
    n_aux = n_p + n_s
    t_aux = xa_f.shape[0] // n_aux

    def project(xb):
        qkv, = _matmul(xb, w["w_qkv"], [bf16], col_scale=w["qkv_scale"])
        o, = _matmul(xb, w["w_o"], [f32])
        gt, = _matmul(xb, w["w_gates"], [f32])
        return qkv, o, gt

    qkvm, om, gm = project(xm_b)
    qkva, oa, ga = project(xa_b)
    seq = lambda a, n: a.reshape(n, a.shape[0] // n, a.shape[1])
    qkvm, om, gm = [seq(a, n_p) for a in (qkvm, om, gm)]
    qkva, oa, ga = [_pad_seq(seq(a, n_aux)) for a in (qkva, oa, ga)]

    c0 = jnp.concatenate([jnp.zeros((n_p, H_C, DK_C, DV_C), f32), jnp.swapaxes(c_past.astype(f32), 2, 3)], axis=0)
    n0 = jnp.concatenate([jnp.zeros((n_p, H_C, DK_C), f32), n_past.astype(f32)], axis=0)
    m0 = jnp.concatenate([jnp.zeros((n_p, 1, LANE), f32), _pad_lanes(m_past.astype(f32))[:, None, :]], axis=0)
    ha, c_a, n_a, m_a = _mlstm_mixer(qkva, oa, ga, c0, n0, m0, w, SEQ_PAD, t_aux)
    lc = min(t_main, 256)
    hm, c_m, n_m, m_m = _mlstm_mixer(qkvm, om, gm, c_a[:n_p], n_a[:n_p], m_a[:n_p], w, lc, lc)

    unseq = lambda a: a.reshape(-1, a.shape[-1])
    x1m_f, x1m_b = _outproj_ln([unseq(hm)], w["w_out"], xm_f, w["ln_g"], w["ln_b"])
    x1a_f, x1a_b = _outproj_ln([unseq(ha[:, :t_aux])], w["w_out"], xa_f, w["ln_g"], w["ln_b"])
    new = dict(c_p=jnp.swapaxes(c_m, 2, 3), n_p=n_m, m_p=m_m[:, 0, :H_C],
               c_s=jnp.swapaxes(c_a[n_p:], 2, 3), n_s=n_a[n_p:], m_s=m_a[n_p:, 0, :H_C])
    return (x1m_f, x1m_b, x1a_f, x1a_b), new


def _even_weights(e, w_in_a, b_fgate_a, conv_w, conv_b, dt_bias, a_log, d_skip, ssd_norm_g, w_out_a):
    wi = w_in_a[e]
    o_f = 3 * D_A
    o_z = o_f + H_A
    o_dt = o_z + D_SSM + CONV_DIM
    w_small = jnp.concatenate([_pad_lanes(wi[:, o_dt:o_dt + H_B]), _pad_lanes(wi[:, o_f:o_f + H_A])], axis=1)
    expand = (np.arange(LANE)[:, None] == (np.arange(D_SSM) // P_B)[None, :])
    return dict(
        w_q=wi[:, :D_A].astype(bf16), w_k=wi[:, D_A:2 * D_A].astype(bf16),
        w_v=wi[:, 2 * D_A:3 * D_A].astype(bf16),
        w_zx=wi[:, o_z:o_dt].astype(bf16), w_small=w_small.astype(bf16),
        b_f=_pad_lanes(b_fgate_a[e][None, :]),
        conv_w=jnp.pad(conv_w[e], ((0, SUBLANE - CONV_W), (0, 0))), conv_b=conv_b[e][None, :],
        dt_bias=_pad_lanes(dt_bias[e][None, :]), a_log_row=_pad_lanes(a_log[e][None, :]),
        a_log_col=a_log[e][:, None], d_skip=jnp.repeat(d_skip[e], P_B)[None, :],
        norm_g=ssd_norm_g[e][None, :], expand=jnp.asarray(expand, bf16),
        w_out=w_out_a[e].astype(bf16))


def _odd_weights(o, w_in_c, b_igate_c, b_fgate_c, mlstm_norm_g, w_out_c):
    wi = w_in_c[o]
    o_o = 2 * D_QK_C + D_V_C
    o_g = o_o + D_V_C
    scale = jnp.concatenate([jnp.ones((D_QK_C,), f32), jnp.full((D_QK_C,), DK_C ** -0.5, f32),
                             jnp.ones((D_V_C,), f32)])[None, :]
    return dict(
        w_qkv=wi[:, :o_o].astype(bf16), qkv_scale=scale, w_o=wi[:, o_o:o_g].astype(bf16),
        w_gates=_pad_lanes(wi[:, o_g:]).astype(bf16),
        gate_bias=_pad_lanes(jnp.concatenate([b_igate_c[o], b_fgate_c[o]])[None, :]),
        norm_g=mlstm_norm_g[o][None, :], w_out=w_out_c[o].astype(bf16))


def kernel(x_prompt, x_sample, cache_fox_k, cache_fox_v, cache_fox_logf, state_ssd_conv, state_ssd, state_mlstm_c, state_mlstm_n, state_mlstm_m, meta_tokens, w_in_a, b_fgate_a, conv_w, conv_b, dt_bias, a_log, d_skip, ssd_norm_g, w_out_a, w_in_c, b_igate_c, b_fgate_c, mlstm_norm_g, w_out_c, ln_mix_g, ln_mix_b, ln_ffn_g, ln_ffn_b, w_ffn_gate, w_ffn_up, w_ffn_down):
    n_p, t_main, d = x_prompt.shape
    n_s, t_s, _ = x_sample.shape
    assert t_s == N_META, "aux rows hold equal-length meta and running-stream sequences"
    xm_f = x_prompt.reshape(n_p * t_main, d)
    xa_f = jnp.concatenate([jnp.broadcast_to(meta_tokens.astype(x_prompt.dtype)[None], (n_p, N_META, d)),
                            x_sample], axis=0).reshape((n_p + n_s) * t_s, d)
    xm_b, xa_b = xm_f.astype(bf16), xa_f.astype(bf16)

    even_out, odd_out = [], []
    for layer in range(DEPTH):
        if layer % 2 == 0:
            e = layer // 2
            w = _even_weights(e, w_in_a, b_fgate_a, conv_w, conv_b, dt_bias, a_log, d_skip, ssd_norm_g, w_out_a)
            cache = (cache_fox_k[e], cache_fox_v[e], cache_fox_logf[e], state_ssd_conv[e], state_ssd[e])
            w["ln_g"], w["ln_b"] = ln_mix_g[layer], ln_mix_b[layer]
            (x1m_f, x1m_b, x1a_f, x1a_b), new = _even_layer(xm_b, xm_f, xa_b, xa_f, n_p, t_main, cache, w)
            even_out.append(new)
        else:
            o = layer // 2
            w = _odd_weights(o, w_in_c, b_igate_c, b_fgate_c, mlstm_norm_g, w_out_c)
            cache = (state_mlstm_c[o], state_mlstm_n[o], state_mlstm_m[o])
            w["ln_g"], w["ln_b"] = ln_mix_g[layer], ln_mix_b[layer]
            (x1m_f, x1m_b, x1a_f, x1a_b), new = _odd_layer(xm_b, xm_f, xa_b, xa_f, n_p, t_main, cache, w)
            odd_out.append(new)
        wg, wu, wd = w_ffn_gate[layer].astype(bf16), w_ffn_up[layer].astype(bf16), w_ffn_down[layer].astype(bf16)
        xm_f, xm_b = _ffn_ln(x1m_b, x1m_f, wg, wu, wd, ln_ffn_g[layer], ln_ffn_b[layer])
        xa_f, xa_b = _ffn_ln(x1a_b, x1a_f, wg, wu, wd, ln_ffn_g[layer], ln_ffn_b[layer])

    y_prompt = xm_f.reshape(n_p, t_main, d)
    y_sample = xa_f.reshape(n_p + n_s, t_s, d)[n_p:]
    st = lambda outs, key: jnp.stack([o[key] for o in outs])
    return (y_prompt, y_sample,
            st(even_out, "k_p"), st(even_out, "v_p"), st(even_out, "f_p"), st(even_out, "conv_p"), st(even_out, "h_p"),
            st(odd_out, "c_p"), st(odd_out, "n_p"), st(odd_out, "m_p"),
            st(even_out, "k_s"), st(even_out, "v_s"), st(even_out, "f_s"), st(even_out, "conv_s"), st(even_out, "h_s"),
            st(odd_out, "c_s"), st(odd_out, "n_s"), st(odd_out, "m_s"))
```

```python
import functools

import jax
import jax.numpy as jnp
import numpy as np
from jax import lax
from jax.experimental import pallas as pl
from jax.experimental.pallas import tpu as pltpu

f32 = jnp.float32
bf16 = jnp.bfloat16

D_MODEL = 2048
DEPTH = 2
N_META = 16
H_A, DH_A = 16, 128
D_A = H_A * DH_A
H_B, P_B, G_B, N_B = 32, 64, 4, 128
HG_B = H_B // G_B
D_SSM = H_B * P_B
CONV_W = 4
CONV_DIM = D_SSM + 2 * G_B * N_B
H_C, DK_C, DV_C = 8, 128, 256
D_QK_C = H_C * DK_C
D_V_C = H_C * DV_C
ALPHA = (2 * DEPTH) ** 0.25
NEG = -1e30
LN_EPS = 1e-5
LOG2E = 1.4426950408889634

LANE = 128
SUBLANE = 8
SEQ_PAD = 64
VMEM_LIMIT = 56 * 1024 * 1024
ATTN_BLOCK = 512
CACHE_BLOCK = 1024
SSD_CHUNK = 128
MLSTM_CHUNK = 256


def _cparams(sem):
    return pltpu.CompilerParams(dimension_semantics=sem, vmem_limit_bytes=VMEM_LIMIT)


def _tile(n, pref, mult=16):
    if n <= pref:
        return n
    t = (pref // mult) * mult
    while t >= mult:
        if n % t == 0:
            return t
        t -= mult
    return n


def _softplus(x):
    return jnp.maximum(x, 0.0) + jnp.log1p(jnp.exp(-jnp.abs(x)))


def _log_sigmoid(x):
    return -_softplus(-x)


def _sigmoid(x):
    return 1.0 / (1.0 + jnp.exp(-x))


def _split3(a):
    a1 = a.astype(bf16)
    r1 = a - a1.astype(f32)
    a2 = r1.astype(bf16)
    a3 = (r1 - a2.astype(f32)).astype(bf16)
    return a1, a2, a3


_NN = (((1,), (0,)), ((), ()))
_NT = (((1,), (1,)), ((), ()))
_TN = (((0,), (0,)), ((), ()))


def _dot(a, b, dims=_NN):
    return lax.dot_general(a, b, dims, preferred_element_type=f32)


def _dot_exact_rhs(a_bf16_exact, b_f32, dims=_NN):
    b1, b2, b3 = _split3(b_f32)
    return _dot(a_bf16_exact, b1, dims) + _dot(a_bf16_exact, b2, dims) + _dot(a_bf16_exact, b3, dims)


def _dot_exact_lhs(a_f32, b_bf16_exact, dims=_NN):
    a1, a2, a3 = _split3(a_f32)
    return _dot(a1, b_bf16_exact, dims) + _dot(a2, b_bf16_exact, dims) + _dot(a3, b_bf16_exact, dims)


def _iota2(shape, axis):
    return lax.broadcasted_iota(jnp.int32, shape, axis)


def _tri_lower(n):
    return (_iota2((n, n), 0) >= _iota2((n, n), 1)).astype(bf16)


def _tri_upper(n):
    return (_iota2((n, n), 0) <= _iota2((n, n), 1)).astype(bf16)


def _eye(rows, cols):
    return (_iota2((rows, cols), 0) == _iota2((rows, cols), 1)).astype(bf16)


def _layer_norm(x, g, b):
    mu = jnp.mean(x, axis=-1, keepdims=True)
    xc = x - mu
    var = jnp.mean(xc * xc, axis=-1, keepdims=True)
    return xc * lax.rsqrt(var + LN_EPS) * g + b


def _mm_kernel(x_ref, w_ref, s_ref, *out_refs):
    acc = _dot(x_ref[...], w_ref[...]) * s_ref[...]
    for o in out_refs:
        o[...] = acc.astype(o.dtype)


def _matmul(x, w, out_dtypes, col_scale=None, tm_pref=1024, tn_pref=1024):
    m, k = x.shape
    n = w.shape[1]
    tm = _tile(m, tm_pref)
    tn = _tile(n, tn_pref, LANE)
    if col_scale is None:
        col_scale = jnp.ones((1, n), f32)
    return pl.pallas_call(
        _mm_kernel,
        grid=(m // tm, n // tn),
        in_specs=[pl.BlockSpec((tm, k), lambda i, j: (i, 0)),
                  pl.BlockSpec((k, tn), lambda i, j: (0, j)),
                  pl.BlockSpec((1, tn), lambda i, j: (0, j))],
        out_specs=[pl.BlockSpec((tm, tn), lambda i, j: (i, j)) for _ in out_dtypes],
        out_shape=[jax.ShapeDtypeStruct((m, n), dt) for dt in out_dtypes],
        compiler_params=_cparams(("parallel", "arbitrary")),
        name="proj_matmul",
    )(x, w, col_scale)


def _kv_proj_kernel(x_ref, w_ref, base_ref, o_ref, ob_ref, *, tm, nh):
    del base_ref
    acc = _dot(x_ref[...], w_ref[...])
    ob_ref[...] = acc.astype(bf16)
    for h in range(nh):
        o_ref[0, pl.ds(h, tm, stride=nh), :] = acc[:, h * DH_A:(h + 1) * DH_A]


def _kv_proj(x, w, head_rows, tm_pref=512):
    m, k = x.shape
    n = w.shape[1]
    n_seq, row_off, nh, _ = head_rows.shape
    t_out = row_off + m // n_seq
    tm = _tile(m // n_seq, tm_pref)
    per_seq = m // n_seq // tm
    base = jnp.pad(head_rows.reshape(n_seq, row_off * nh, DH_A), ((0, 0), (0, (t_out - row_off) * nh), (0, 0)))
    o, ob = pl.pallas_call(
        functools.partial(_kv_proj_kernel, tm=tm, nh=nh),
        grid=(m // tm,),
        in_specs=[pl.BlockSpec((tm, k), lambda i: (i, 0)), pl.BlockSpec((k, n), lambda i: (0, 0)),
                  pl.BlockSpec(memory_space=pl.ANY)],
        out_specs=[pl.BlockSpec((pl.Element(1), pl.Element(tm * nh), pl.Element(DH_A)),
                                lambda i: (i // per_seq,
                                           pl.multiple_of((row_off + (i % per_seq) * tm) * nh, SUBLANE), 0)),
                   pl.BlockSpec((tm, n), lambda i: (i, 0))],
        out_shape=[jax.ShapeDtypeStruct((n_seq, t_out * nh, DH_A), f32), jax.ShapeDtypeStruct((m, n), bf16)],
        input_output_aliases={2: 0},
        compiler_params=_cparams(("parallel",)),
        name="kv_proj",
    )(x, w, base)
    return o.reshape(n_seq, t_out, nh, DH_A), ob


def _outproj_ln_kernel(*refs, n_act):
    acts = refs[:n_act]
    w_ref, res_ref, g_ref, b_ref, of_ref, ob_ref = refs[n_act:]
    kw = acts[0].shape[1]
    mix = _dot(acts[0][...], w_ref[0:kw, :])
    for a in range(1, n_act):
        mix = mix + _dot(acts[a][...], w_ref[a * kw:(a + 1) * kw, :])
    y = _layer_norm(ALPHA * res_ref[...] + mix, g_ref[...], b_ref[...])
    of_ref[...] = y
    ob_ref[...] = y.astype(bf16)


def _outproj_ln(acts, w, res, g, b, tm_pref=512):
    m, d = res.shape
    tm = _tile(m, tm_pref)
    n_act = len(acts)
    row = lambda width: pl.BlockSpec((tm, width), lambda i: (i, 0))
    once = lambda shape: pl.BlockSpec(shape, lambda i: (0, 0), pipeline_mode=pl.Buffered(1))
    return pl.pallas_call(
        functools.partial(_outproj_ln_kernel, n_act=n_act),
        grid=(m // tm,),
        in_specs=[row(a.shape[1]) for a in acts] + [once(w.shape), row(d), once((1, d)), once((1, d))],
        out_specs=[row(d), row(d)],
        out_shape=[jax.ShapeDtypeStruct((m, d), f32), jax.ShapeDtypeStruct((m, d), bf16)],
        compiler_params=_cparams(("parallel",)),
        name="outproj_ln",
    )(*acts, w, res, g.reshape(1, d), b.reshape(1, d))


def _ffn_ln_kernel(xb_ref, xf_ref, wg_ref, wu_ref, wd_ref, g_ref, b_ref, of_ref, ob_ref, acc_ref):
    f = pl.program_id(1)

    @pl.when(f == 0)
    def _():
        acc_ref[...] = jnp.zeros_like(acc_ref)

    x = xb_ref[...]
    gate = _dot(x, wg_ref[0])
    up = _dot(x, wu_ref[0])
    h = (gate * _sigmoid(gate) * up).astype(bf16)
    acc_ref[...] += _dot(h, wd_ref[0])

    @pl.when(f == pl.num_programs(1) - 1)
    def _():
        y = _layer_norm(ALPHA * xf_ref[...] + acc_ref[...], g_ref[...], b_ref[...])
        of_ref[...] = y
        ob_ref[...] = y.astype(bf16)


def _ffn_ln(xb, xf, wg, wu, wd, layer, g, b, tm_pref=512, tf_pref=512):
    m, d = xf.shape
    dff = wg.shape[2]
    tm = _tile(m, tm_pref)
    tf = _tile(dff, tf_pref, LANE)
    return pl.pallas_call(
        _ffn_ln_kernel,
        grid=(m // tm, dff // tf),
        in_specs=[pl.BlockSpec((tm, d), lambda i, f: (i, 0)),
                  pl.BlockSpec((tm, d), lambda i, f: (i, 0)),
                  pl.BlockSpec((1, d, tf), lambda i, f: (layer, 0, f)),
                  pl.BlockSpec((1, d, tf), lambda i, f: (layer, 0, f)),
                  pl.BlockSpec((1, tf, d), lambda i, f: (layer, f, 0)),
                  pl.BlockSpec((1, d), lambda i, f: (0, 0)),
                  pl.BlockSpec((1, d), lambda i, f: (0, 0))],
        out_specs=[pl.BlockSpec((tm, d), lambda i, f: (i, 0)),
                   pl.BlockSpec((tm, d), lambda i, f: (i, 0))],
        out_shape=[jax.ShapeDtypeStruct((m, d), f32), jax.ShapeDtypeStruct((m, d), bf16)],
        scratch_shapes=[pltpu.VMEM((tm, d), f32)],
        compiler_params=_cparams(("parallel", "arbitrary")),
        name="ffn_ln",
    )(xb, xf, wg, wu, wd, g.reshape(1, d), b.reshape(1, d))


FORGET_PIECES = 3


def _cumsum_kernel(x_ref, bias_ref, carry_ref, *refs, gate):
    acc_ref = refs[-1]
    c = pl.program_id(1)

    @pl.when(c == 0)
    def _():
        acc_ref[...] = carry_ref[0]

    x = x_ref[0]
    lf = _log_sigmoid(x + bias_ref[...]) if gate else x
    tc = x.shape[0]
    cum = _dot_exact_rhs(_tri_lower(tc), lf) + acc_ref[...]
    acc_ref[...] = cum[tc - 1:tc, :]
    if not gate:
        refs[0][0] = cum
        return
    logf_ref, cum_ref, cols_ref = refs[:3]
    logf_ref[0] = lf
    cum_ref[0] = cum
    r, col = _iota2((LANE, LANE), 0), _iota2((LANE, LANE), 1)
    out = None
    for i, piece in enumerate(_split3(cum * LOG2E)):
        t = _dot(piece, ((col == FORGET_PIECES * r + i) & (r < H_A)).astype(bf16))
        out = t if out is None else out + t
    cols_ref[0] = out.astype(bf16)


def _logf_cumsum(x, bias, carry, gate, tc_pref=512):
    bsz, t, w = x.shape
    tc = _tile(t, tc_pref)
    spec = pl.BlockSpec((1, tc, w), lambda b, c: (b, c, 0))
    out_dtypes = [f32, f32, bf16] if gate else [f32]
    return pl.pallas_call(
        functools.partial(_cumsum_kernel, gate=gate),
        grid=(bsz, t // tc),
        in_specs=[spec,
                  pl.BlockSpec((1, w), lambda b, c: (0, 0)),
                  pl.BlockSpec((1, 1, w), lambda b, c: (b, 0, 0))],
        out_specs=[spec for _ in out_dtypes],
        out_shape=[jax.ShapeDtypeStruct((bsz, t, w), dt) for dt in out_dtypes],
        scratch_shapes=[pltpu.VMEM((1, w), f32)],
        compiler_params=_cparams(("parallel", "arbitrary")),
        name="logf_cumsum",
    )(x, bias, carry)


def _attn_self_kernel(q_ref, k_ref, v_ref, f_ref, o_ref, *, nh):
    d = DH_A
    tq, t2 = q_ref.shape[1], k_ref.shape[1]
    causal = _iota2((tq, t2), 0) >= _iota2((tq, t2), 1)
    for h in range(nh):
        cols = slice(h * d, (h + 1) * d)
        s = _dot(q_ref[0, :, cols], k_ref[0, :, cols], _NT) - f_ref[0, h] * LOG2E
        s = jnp.where(causal, s, NEG)
        p = jnp.exp2(s - jnp.max(s, axis=-1, keepdims=True))
        acc = _dot(p.astype(bf16), v_ref[0, :, cols])
        o_ref[0, :, cols] = (acc / jnp.sum(p, axis=-1, keepdims=True)).astype(o_ref.dtype)


def _fox_attention_self(q, k, v, f):
    bsz, tq, hd = q.shape
    t2 = k.shape[1]
    nh = hd // DH_A
    qspec = pl.BlockSpec((1, tq, hd), lambda b: (b, 0, 0))
    kvspec = pl.BlockSpec((1, t2, hd), lambda b: (b, 0, 0))
    return pl.pallas_call(
        functools.partial(_attn_self_kernel, nh=nh),
        grid=(bsz,),
        in_specs=[qspec, kvspec, kvspec, pl.BlockSpec((1, nh, 1, t2), lambda b: (b, 0, 0, 0))],
        out_specs=qspec,
        out_shape=jax.ShapeDtypeStruct((bsz, tq, hd), bf16),
        compiler_params=_cparams(("parallel",)),
        name="fox_attention_self",
    )(q, k, v, f)


def _attn_cache_kernel(q_ref, k1_ref, v1_ref, f1_ref, k2_ref, v2_ref, f2_ref, o_ref, m_ref, l_ref, acc_ref,
                       *, tk, nh):
    j = pl.program_id(1)
    d = DH_A
    tq = q_ref.shape[1]

    @pl.when(j == 0)
    def _():
        m_ref[...] = jnp.full(m_ref.shape, NEG, f32)
        l_ref[...] = jnp.zeros(l_ref.shape, f32)
        acc_ref[...] = jnp.zeros(acc_ref.shape, f32)

    def update(h, s, v, fk, mask):
        s = s - fk * LOG2E
        if mask is not None:
            s = jnp.where(mask, s, NEG)
        m = m_ref[h][:, 0:1]
        m_new = jnp.maximum(m, jnp.max(s, axis=-1, keepdims=True))
        alpha = jnp.exp2(m - m_new)
        p = jnp.exp2(s - m_new)
        l_ref[h] = jnp.broadcast_to(alpha * l_ref[h][:, 0:1] + jnp.sum(p, axis=-1, keepdims=True), (tq, LANE))
        acc_ref[h] = alpha * acc_ref[h] + _dot(p.astype(bf16), v)
        m_ref[h] = jnp.broadcast_to(m_new, (tq, LANE))

    qs = [q_ref[0, :, h * d:(h + 1) * d] for h in range(nh)]
    ss = [_dot(qs[h], k1_ref[0, pl.ds(h, tk, stride=nh), :].astype(bf16), _NT) for h in range(nh)]
    for h in range(nh):
        update(h, ss[h], v1_ref[0, pl.ds(h, tk, stride=nh), :].astype(bf16), f1_ref[0, h], None)

    @pl.when(j == pl.num_programs(1) - 1)
    def _():
        t2 = k2_ref.shape[1]
        causal = _iota2((tq, t2), 0) >= _iota2((tq, t2), 1)
        for h in range(nh):
            cols = slice(h * d, (h + 1) * d)
            update(h, _dot(qs[h], k2_ref[0, :, cols], _NT), v2_ref[0, :, cols], f2_ref[0, h], causal)
            o_ref[0, :, cols] = (acc_ref[h] / l_ref[h][:, 0:1]).astype(o_ref.dtype)


def _fox_attention_cache(q, k1, v1, f1, k2, v2, f2, tk=512):
    bsz, tq, hd = q.shape
    s, nh = k1.shape[1], k1.shape[2]
    t2 = k2.shape[1]
    flat = lambda a: a.reshape(bsz, s * nh, DH_A)
    qspec = pl.BlockSpec((1, tq, hd), lambda b, j: (b, 0, 0))
    cspec = pl.BlockSpec((1, tk * nh, DH_A), lambda b, j: (b, j, 0))
    nspec = pl.BlockSpec((1, t2, hd), lambda b, j: (b, 0, 0))
    return pl.pallas_call(
        functools.partial(_attn_cache_kernel, tk=tk, nh=nh),
        grid=(bsz, s // tk),
        in_specs=[qspec, cspec, cspec, pl.BlockSpec((1, nh, 1, tk), lambda b, j: (b, 0, 0, j)),
                  nspec, nspec, pl.BlockSpec((1, nh, 1, t2), lambda b, j: (b, 0, 0, 0))],
        out_specs=qspec,
        out_shape=jax.ShapeDtypeStruct((bsz, tq, hd), bf16),
        scratch_shapes=[pltpu.VMEM((nh, tq, LANE), f32), pltpu.VMEM((nh, tq, LANE), f32),
                        pltpu.VMEM((nh, tq, DH_A), f32)],
        compiler_params=_cparams(("parallel", "arbitrary")),
        name="fox_attention_cache",
    )(q, flat(k1), flat(v1), f1, k2, v2, f2)


def _attn_t_kernel(q_ref, k1_ref, v1_ref, c1_ref, k2_ref, v2_ref, c2_ref, o_ref, *, tq, s1_valid, hps):
    hg = pl.program_id(1)
    qi = pl.program_id(2)
    d = DH_A

    lane = _iota2((tq, LANE), 1)
    q_aug = []
    for hh in range(hps):
        lo = FORGET_PIECES * (hg * hps + hh)
        pick = jnp.where((lane >= lo) & (lane < lo + FORGET_PIECES), -1.0, 0.0).astype(bf16)
        q_aug.append(jnp.concatenate([q_ref[0, :, hh * d:(hh + 1) * d], pick], axis=-1))

    def update(carry, s, v, mask):
        m, l, acc = carry
        if mask is not None:
            s = jnp.where(mask, s, NEG)
        m_new = jnp.maximum(m, jnp.max(s, axis=0, keepdims=True))
        alpha = jnp.exp2(m - m_new)
        p = jnp.exp2(s - m_new)
        l = alpha * l + jnp.sum(p, axis=0, keepdims=True)
        acc = alpha * acc + _dot(v, p.astype(bf16), _TN)
        return m_new, l, acc

    def multi(carries, k_augs, vs, mask):
        ss = [_dot(k_augs[hh], q_aug[hh], _NT) for hh in range(hps)]
        return [update(carries[hh], ss[hh], vs[hh], mask) for hh in range(hps)]

    carries = [(jnp.full((1, tq), NEG, f32), jnp.zeros((1, tq), f32), jnp.zeros((d, tq), f32))
               for _ in range(hps)]
    k1_aug = [jnp.concatenate([k1_ref[0, 0:s1_valid, hh * d:(hh + 1) * d], c1_ref[0, 0:s1_valid, :]], axis=-1)
              for hh in range(hps)]
    carries = multi(carries, k1_aug, [v1_ref[0, 0:s1_valid, hh * d:(hh + 1) * d] for hh in range(hps)], None)

    def blk(j, c, mask):
        off = pl.multiple_of(j * tq, tq)
        cols = c2_ref[0, pl.ds(off, tq), :]
        return multi(c, [jnp.concatenate([k2_ref[0, pl.ds(off, tq), hh * d:(hh + 1) * d], cols], axis=-1)
                         for hh in range(hps)],
                     [v2_ref[0, pl.ds(off, tq), hh * d:(hh + 1) * d] for hh in range(hps)], mask)

    carries = lax.fori_loop(0, qi, lambda j, c: blk(j, c, None), carries)
    carries = blk(qi, carries, _iota2((tq, tq), 0) <= _iota2((tq, tq), 1))
    for hh in range(hps):
        m, l, acc = carries[hh]
        o_ref[0, :, hh * d:(hh + 1) * d] = jnp.transpose(acc / l).astype(o_ref.dtype)


def _fox_attention_main(q, k1, v1, c1, k2, v2, c2, s1_valid, tq=512, hps=4):
    bsz, t, hd = q.shape
    s1 = k1.shape[1]
    w = hps * DH_A
    qspec = pl.BlockSpec((1, tq, w), lambda b, h, i: (b, i, h))
    kvspec = pl.BlockSpec((1, t, w), lambda b, h, i: (b, 0, h))
    pspec = pl.BlockSpec((1, s1, w), lambda b, h, i: (b, 0, h))
    return pl.pallas_call(
        functools.partial(_attn_t_kernel, tq=tq, s1_valid=s1_valid, hps=hps),
        grid=(bsz, hd // w, t // tq),
        in_specs=[qspec, pspec, pspec, pl.BlockSpec((1, s1, LANE), lambda b, h, i: (b, 0, 0)),
                  kvspec, kvspec, pl.BlockSpec((1, t, LANE), lambda b, h, i: (b, 0, 0))],
        out_specs=qspec,
        out_shape=jax.ShapeDtypeStruct((bsz, t, hd), bf16),
        compiler_params=_cparams(("parallel", "parallel", "arbitrary")),
        name="fox_attention_main",
    )(q, k1, v1, c1, k2, v2, c2)


def _ssd_kernel(zx_ref, dt_ref, hist_ref, h0_ref, cw_ref, cb_ref, dtb_ref, alog_ref, alogc_ref,
                dskip_ref, ng_ref, e_ref, y_ref, tail_ref, hout_ref, ext_ref, st_ref, *, lc, t_valid):
    c = pl.program_id(1)

    @pl.when(c == 0)
    def _():
        ext_ref[0:SUBLANE, :] = hist_ref[0]
        st_ref[...] = h0_ref[0]

    x_raw = zx_ref[0, :, D_SSM:]
    ext_ref[SUBLANE:SUBLANE + lc, :] = x_raw
    hist = ext_ref[0:SUBLANE, :]
    row8 = _iota2((SUBLANE, CONV_DIM), 0)
    conv = cb_ref[...] + cw_ref[CONV_W - 1:CONV_W, :] * x_raw
    for s in range(1, CONV_W):
        rolled = pltpu.roll(x_raw, s, axis=0)
        first = jnp.where(row8 < s, pltpu.roll(hist, s, axis=0), rolled[0:SUBLANE])
        conv = conv + cw_ref[CONV_W - 1 - s:CONV_W - s, :] * jnp.concatenate([first, rolled[SUBLANE:]], axis=0)
    xc = conv * _sigmoid(conv)
    tail = ext_ref[t_valid:t_valid + SUBLANE, :]
    ext_ref[0:SUBLANE, :] = tail

    dt = _softplus(dt_ref[0] + dtb_ref[...])
    if t_valid < lc:
        dt = jnp.where(_iota2((lc, LANE), 0) < t_valid, dt, 0.0)
    a_row = -jnp.exp(alog_ref[...])
    a_col = -jnp.exp(alogc_ref[...])
    a_cum = _dot_exact_rhs(_tri_lower(lc), dt * a_row)
    dt_t = _dot_exact_rhs(_eye(H_B, LANE), dt, _NT)
    a_cum_t = _dot_exact_lhs(dt_t * a_col, _tri_upper(lc))
    a_last = a_cum[lc - 1:lc, :]
    e = e_ref[...]
    to_end_x = _dot_exact_lhs(jnp.exp(a_last - a_cum) * dt, e)
    ea_x = _dot_exact_lhs(jnp.exp(a_cum), e)
    cdec_x = _dot_exact_lhs(jnp.broadcast_to(jnp.exp(a_last), (SUBLANE, LANE)), e)[0:1, :]

    causal = _iota2((lc, lc), 0) >= _iota2((lc, lc), 1)
    left = _iota2((lc, LANE), 1) < P_B
    eye_n = _eye(N_B, N_B)
    gw = D_SSM // G_B
    groups = range(G_B)
    gcols = [slice(g * gw, (g + 1) * gw) for g in groups]
    bg = [xc[:, D_SSM + g * N_B:D_SSM + (g + 1) * N_B].astype(bf16) for g in groups]
    cg = [xc[:, D_SSM + G_B * N_B + g * N_B:D_SSM + G_B * N_B + (g + 1) * N_B].astype(bf16) for g in groups]
    cb = [_dot(cg[g], bg[g], _NT) for g in groups]
    bg_t = [_dot(eye_n, bg[g], _NT).astype(bf16) for g in groups]
    st_in = [st_ref[:, gcols[g]] for g in groups]
    y_off = [_dot(cg[g], st_in[g].astype(bf16)) * ea_x[:, gcols[g]] for g in groups]
    upd = [_dot(bg_t[g], (xc[:, gcols[g]] * to_end_x[:, gcols[g]]).astype(bf16)) for g in groups]
    for g in groups:
        st_ref[:, gcols[g]] = cdec_x[:, gcols[g]] * st_in[g] + upd[g]

    def decay_weights(h):
        seg = a_cum[:, h:h + 1] - a_cum_t[h:h + 1, :]
        dec = jnp.exp(jnp.where(causal, seg, NEG))
        return (cb[h // HG_B] * dec * dt_t[h:h + 1, :]).astype(bf16)

    w_all = [decay_weights(h) for h in range(H_B)]
    y_pair = []
    for j in range(D_SSM // LANE):
        xp = xc[:, j * LANE:(j + 1) * LANE].astype(bf16)
        y_pair.append(jnp.where(left, _dot(w_all[2 * j], xp), _dot(w_all[2 * j + 1], xp)))
    ppg = gw // LANE
    for g in groups:
        xg = xc[:, gcols[g]]
        y = jnp.concatenate(y_pair[g * ppg:(g + 1) * ppg], axis=-1) + y_off[g] + dskip_ref[:, gcols[g]] * xg
        z = zx_ref[0, :, g * gw:(g + 1) * gw]
        y = y * (z * _sigmoid(z))
        y = y * lax.rsqrt(jnp.mean(y * y, axis=-1, keepdims=True) + LN_EPS) * ng_ref[:, g * gw:(g + 1) * gw]
        y_ref[0, :, g * gw:(g + 1) * gw] = y.astype(y_ref.dtype)

    @pl.when(c == pl.num_programs(1) - 1)
    def _():
        tail_ref[0] = tail
        hout_ref[0] = st_ref[...]


def _ssd_mixer(zx, dt_raw, hist, h0_t, p, lc, t_valid):
    bsz, t, wz = zx.shape
    const = lambda shape: pl.BlockSpec(shape, lambda b, c: (0,) * len(shape))
    return pl.pallas_call(
        functools.partial(_ssd_kernel, lc=lc, t_valid=t_valid),
        grid=(bsz, t // lc),
        in_specs=[pl.BlockSpec((1, lc, wz), lambda b, c: (b, c, 0)),
                  pl.BlockSpec((1, lc, LANE), lambda b, c: (b, c, 0)),
                  pl.BlockSpec((1, SUBLANE, CONV_DIM), lambda b, c: (b, 0, 0)),
                  pl.BlockSpec((1, N_B, D_SSM), lambda b, c: (b, 0, 0)),
                  const((SUBLANE, CONV_DIM)), const((1, CONV_DIM)), const((1, LANE)), const((1, LANE)),
                  const((H_B, 1)), const((1, D_SSM)), const((1, D_SSM)), const((LANE, D_SSM))],
        out_specs=[pl.BlockSpec((1, lc, D_SSM), lambda b, c: (b, c, 0)),
                   pl.BlockSpec((1, SUBLANE, CONV_DIM), lambda b, c: (b, 0, 0)),
                   pl.BlockSpec((1, N_B, D_SSM), lambda b, c: (b, 0, 0))],
        out_shape=[jax.ShapeDtypeStruct((bsz, t, D_SSM), bf16),
                   jax.ShapeDtypeStruct((bsz, SUBLANE, CONV_DIM), f32),
                   jax.ShapeDtypeStruct((bsz, N_B, D_SSM), f32)],
        scratch_shapes=[pltpu.VMEM((SUBLANE + lc, CONV_DIM), f32), pltpu.VMEM((N_B, D_SSM), f32)],
        compiler_params=_cparams(("parallel", "arbitrary")),
        name="ssd_mixer",
    )(zx, dt_raw, hist, h0_t, p["conv_w"], p["conv_b"], p["dt_bias"], p["a_log_row"], p["a_log_col"],
      p["d_skip"], p["norm_g"], p["expand"])


def _mlstm_kernel(qkv_ref, o_ref, gates_ref, gb_ref, ng_ref, c0_ref, n0_ref, m0_ref,
                  y_ref, cout_ref, nout_ref, mout_ref, ct_ref, n_ref, m_ref, *, lc, t_valid):
    c = pl.program_id(1)

    @pl.when(c == 0)
    def _():
        ct_ref[...] = c0_ref[0]
        n_ref[...] = n0_ref[0]
        m_ref[...] = m0_ref[0]

    lane = _iota2((lc, LANE), 1)
    g = gates_ref[0] + gb_ref[...]
    is_f = (lane >= H_C) & (lane < 2 * H_C)
    lf = jnp.where(is_f, _log_sigmoid(g), 0.0)
    li = g
    if t_valid < lc:
        valid = _iota2((lc, LANE), 0) < t_valid
        lf = jnp.where(valid, lf, 0.0)
        li = jnp.where(valid, li, NEG)
    b_cum = _dot_exact_rhs(_tri_lower(lc), lf)
    rows = 2 * H_C
    li_t = _dot_exact_rhs(_eye(rows, LANE), li, _NT)
    lf_t = _dot_exact_rhs(_eye(rows, LANE), lf, _NT)
    b_cum_t = _dot_exact_lhs(lf_t, _tri_upper(lc))

    causal = _iota2((lc, lc), 0) >= _iota2((lc, lc), 1)
    eye_k = _eye(DK_C, DK_C)
    m_all = m_ref[...]
    m_out = m_all
    q_of = lambda h: qkv_ref[0, :, h * DK_C:(h + 1) * DK_C]
    k_of = lambda h: qkv_ref[0, :, D_QK_C + h * DK_C:D_QK_C + (h + 1) * DK_C]
    v_of = lambda h: qkv_ref[0, :, 2 * D_QK_C + h * DV_C:2 * D_QK_C + (h + 1) * DV_C]
    heads = range(H_C)
    qk_all = [_dot(q_of(h), k_of(h), _NT) for h in heads]
    qc_all = [_dot(q_of(h), ct_ref[h].astype(bf16)) for h in heads]
    kt_all = [_dot(eye_k, k_of(h), _NT) for h in heads]
    bcol = [b_cum[:, H_C + h:H_C + h + 1] for h in heads]
    brow = [b_cum_t[H_C + h:H_C + h + 1, :] for h in heads]
    lirow = [li_t[h:h + 1, :] for h in heads]
    m_st = [m_all[:, h:h + 1] for h in heads]
    d_mat = [jnp.where(causal, bcol[h] - brow[h] + lirow[h], NEG) for h in heads]
    inter = [bcol[h] + m_st[h] for h in heads]
    m_row = [jnp.maximum(inter[h], jnp.max(d_mat[h], axis=-1, keepdims=True)) for h in heads]
    w = [jnp.exp(d_mat[h] - m_row[h]) * qk_all[h] for h in heads]
    g_inter = [jnp.exp(inter[h] - m_row[h]) for h in heads]
    n_st = [n_ref[h:h + 1, :] for h in heads]
    num = [_dot(w[h].astype(bf16), v_of(h)) + g_inter[h] * qc_all[h] for h in heads]
    den = [jnp.sum(w[h], axis=-1, keepdims=True)
           + g_inter[h] * jnp.sum(q_of(h).astype(f32) * n_st[h], axis=-1, keepdims=True) for h in heads]
    hh = [num[h] / jnp.maximum(jnp.abs(den[h]), jnp.exp(-m_row[h])) for h in heads]
    ms = [jnp.mean(hh[h] * hh[h], axis=-1, keepdims=True) for h in heads]
    for h in heads:
        cols = slice(h * DV_C, (h + 1) * DV_C)
        y = hh[h] * lax.rsqrt(ms[h] + LN_EPS) * ng_ref[:, cols] * _sigmoid(o_ref[0, :, cols])
        y_ref[0, :, cols] = y.astype(y_ref.dtype)
    b_tot = [bcol[h][lc - 1:lc, :] for h in heads]
    d_end = [b_tot[h] - brow[h] + lirow[h] for h in heads]
    m_new = [jnp.maximum(b_tot[h] + m_st[h], jnp.max(d_end[h], axis=-1, keepdims=True)) for h in heads]
    w_end = [jnp.exp(d_end[h] - m_new[h]) for h in heads]
    g_old = [jnp.exp(b_tot[h] + m_st[h] - m_new[h]) for h in heads]
    for h in heads:
        ct_ref[h] = g_old[h] * ct_ref[h] + _dot((kt_all[h] * w_end[h]).astype(bf16), v_of(h))
        n_upd = _dot(jnp.broadcast_to(w_end[h], (SUBLANE, lc)).astype(bf16), k_of(h))[0:1, :]
        n_ref[h:h + 1, :] = g_old[h] * n_st[h] + n_upd
        m_out = jnp.where(lane[0:1, :] == h, m_new[h], m_out)
    m_ref[...] = m_out

    @pl.when(c == pl.num_programs(1) - 1)
    def _():
        cout_ref[0] = ct_ref[...]
        nout_ref[0] = n_ref[...]
        mout_ref[0] = m_ref[...]


def _mlstm_mixer(qkv, o, gates, c0_t, n0, m0, p, lc, t_valid):
    bsz, t, wq = qkv.shape
    const = lambda shape: pl.BlockSpec(shape, lambda b, c: (0,) * len(shape))
    return pl.pallas_call(
        functools.partial(_mlstm_kernel, lc=lc, t_valid=t_valid),
        grid=(bsz, t // lc),
        in_specs=[pl.BlockSpec((1, lc, wq), lambda b, c: (b, c, 0)),
                  pl.BlockSpec((1, lc, D_V_C), lambda b, c: (b, c, 0)),
                  pl.BlockSpec((1, lc, LANE), lambda b, c: (b, c, 0)),
                  const((1, LANE)), const((1, D_V_C)),
                  pl.BlockSpec((1, H_C, DK_C, DV_C), lambda b, c: (b, 0, 0, 0)),
                  pl.BlockSpec((1, H_C, DK_C), lambda b, c: (b, 0, 0)),
                  pl.BlockSpec((1, 1, LANE), lambda b, c: (b, 0, 0))],
        out_specs=[pl.BlockSpec((1, lc, D_V_C), lambda b, c: (b, c, 0)),
                   pl.BlockSpec((1, H_C, DK_C, DV_C), lambda b, c: (b, 0, 0, 0)),
                   pl.BlockSpec((1, H_C, DK_C), lambda b, c: (b, 0, 0)),
                   pl.BlockSpec((1, 1, LANE), lambda b, c: (b, 0, 0))],
        out_shape=[jax.ShapeDtypeStruct((bsz, t, D_V_C), bf16),
                   jax.ShapeDtypeStruct((bsz, H_C, DK_C, DV_C), f32),
                   jax.ShapeDtypeStruct((bsz, H_C, DK_C), f32),
                   jax.ShapeDtypeStruct((bsz, 1, LANE), f32)],
        scratch_shapes=[pltpu.VMEM((H_C, DK_C, DV_C), f32), pltpu.VMEM((H_C, DK_C), f32),
                        pltpu.VMEM((1, LANE), f32)],
        compiler_params=_cparams(("parallel", "arbitrary")),
        name="mlstm_mixer",
    )(qkv, o, gates, p["gate_bias"], p["norm_g"], c0_t, n0, m0)


def _pad_lanes(a, width=LANE):
    return jnp.pad(a, [(0, 0)] * (a.ndim - 1) + [(0, width - a.shape[-1])])


def _pad_seq(a, t_pad=SEQ_PAD):
    return jnp.pad(a, [(0, 0), (0, t_pad - a.shape[1])] + [(0, 0)] * (a.ndim - 2))


def _even_layer(xm_b, xm_f, xa_b, xa_f, n_p, t_main, cache, w):
    k_past, v_past, logf_past, conv_past, h_past = cache
    n_s = k_past.shape[0]
    n_aux = n_p + n_s
    t_aux = xa_f.shape[0] // n_aux
    past = k_past.shape[1]

    seq = lambda a, n: a.reshape(n, a.shape[0] // n, a.shape[1])
    heads = lambda a: a.reshape(a.shape[0], a.shape[1], H_A, DH_A)

    def project(xb, meta_kv):
        q, = _matmul(xb, w["w_q"], [bf16], col_scale=jnp.full((1, D_A), DH_A ** -0.5 * LOG2E, f32))
        if meta_kv is None:
            k_f, k_b = _matmul(xb, w["w_k"], [f32, bf16])
            v_f, v_b = _matmul(xb, w["w_v"], [f32, bf16])
        else:
            k_f, k_b = _kv_proj(xb, w["w_k"], meta_kv[0])
            v_f, v_b = _kv_proj(xb, w["w_v"], meta_kv[1])
        zx, = _matmul(xb, w["w_zx"], [f32])
        sm, = _matmul(xb, w["w_small"], [f32], tn_pref=2 * LANE)
        return q, k_f, k_b, v_f, v_b, zx, sm

    qa, ka_f, ka_b, va_f, va_b, zxa, sma = project(xa_b, None)
    qa, ka_f, va_f = [seq(a, n_aux) for a in (qa, ka_f, va_f)]
    ka_b, va_b, zxa, sma = [_pad_seq(seq(a, n_aux)) for a in (ka_b, va_b, zxa, sma)]
    ka_f, va_f = heads(ka_f), heads(va_f)
    qm, kp_f, km_b, vp_f, vm_b, zxm, smm = project(xm_b, (ka_f[:n_p], va_f[:n_p]))
    qm, km_b, vm_b, zxm, smm = [seq(a, n_p) for a in (qm, km_b, vm_b, zxm, smm)]

    zero_c = jnp.zeros((n_s, 1, LANE), f32)
    cum_past, = _logf_cumsum(_pad_lanes(logf_past.astype(f32)), w["b_f"], zero_c, gate=False)
    carry_a = jnp.concatenate([jnp.zeros((n_p, 1, LANE), f32), cum_past[:, past - 1:past]], axis=0)
    logf_a, cum_a, cols_a = _logf_cumsum(sma[:, :, LANE:], w["b_f"], carry_a, gate=True)
    logf_m, _, cols_m = _logf_cumsum(smm[:, :, LANE:], w["b_f"], cum_a[:n_p, t_aux - 1:t_aux], gate=True)
    rows = lambda cum: jnp.swapaxes(cum[:, :, :H_A], 1, 2)[:, :, None, :]

    fa = rows(cum_a)
    att_meta = _fox_attention_self(qa[:n_p], ka_b[:n_p], va_b[:n_p], fa[:n_p])
    att_s = _fox_attention_cache(qa[n_p:], k_past, v_past, rows(cum_past), ka_b[n_p:], va_b[n_p:], fa[n_p:],
                                 tk=min(past, CACHE_BLOCK))
    att_m = _fox_attention_main(qm, ka_b[:n_p], va_b[:n_p], cols_a[:n_p], km_b, vm_b, cols_m, t_aux,
                                tq=min(t_main, ATTN_BLOCK))
    att_a = jnp.concatenate([att_meta, att_s], axis=0)

    hist_a = jnp.concatenate([jnp.zeros((n_p, SUBLANE, CONV_DIM), f32),
                              jnp.pad(conv_past.astype(f32), ((0, 0), (SUBLANE - (CONV_W - 1), 0), (0, 0)))], axis=0)
    to_t = lambda h: jnp.transpose(h.reshape(-1, D_SSM, N_B), (0, 2, 1))
    from_t = lambda h: jnp.transpose(h, (0, 2, 1)).reshape(-1, H_B, P_B, N_B)
    h0_a = jnp.concatenate([jnp.zeros((n_p, N_B, D_SSM), f32), to_t(h_past.astype(f32))], axis=0)
    ya, tail_a, h_a = _ssd_mixer(zxa, sma[:, :, :LANE], hist_a, h0_a, w, SEQ_PAD, t_aux)
    ym, tail_m, h_m = _ssd_mixer(zxm, smm[:, :, :LANE], tail_a[:n_p], h_a[:n_p], w, min(t_main, SSD_CHUNK), min(t_main, SSD_CHUNK))

    unseq = lambda a: a.reshape(-1, a.shape[-1])
    x1m_f, x1m_b = _outproj_ln([unseq(att_m), unseq(ym)], w["w_out"], xm_f, w["ln_g"], w["ln_b"])
    x1a_f, x1a_b = _outproj_ln([unseq(att_a), unseq(ya[:, :t_aux])], w["w_out"], xa_f, w["ln_g"], w["ln_b"])

    new = dict(
        k_p=kp_f, v_p=vp_f,
        f_p=jnp.concatenate([logf_a[:n_p, :t_aux, :H_A], logf_m[:, :, :H_A]], axis=1),
        conv_p=tail_m[:, SUBLANE - (CONV_W - 1):], h_p=from_t(h_m),
        k_s=ka_f[n_p:], v_s=va_f[n_p:], f_s=logf_a[n_p:, :t_aux, :H_A],
        conv_s=tail_a[n_p:, SUBLANE - (CONV_W - 1):], h_s=from_t(h_a[n_p:]))
    return (x1m_f, x1m_b, x1a_f, x1a_b), new


def _odd_layer(xm_b, xm_f, xa_b, xa_f, n_p, t_main, cache, w):
    c_past, n_past, m_past = cache
    n_s = c_past.shape[0]
    n_aux = n_p + n_s
    t_aux = xa_f.shape[0] // n_aux

    def project(xb):
        qkv, = _matmul(xb, w["w_qkv"], [bf16], col_scale=w["qkv_scale"])
        o, = _matmul(xb, w["w_o"], [f32])
        gt, = _matmul(xb, w["w_gates"], [f32])
        return qkv, o, gt

    qkvm, om, gm = project(xm_b)
    qkva, oa, ga = project(xa_b)
    seq = lambda a, n: a.reshape(n, a.shape[0] // n, a.shape[1])
    qkvm, om, gm = [seq(a, n_p) for a in (qkvm, om, gm)]
    qkva, oa, ga = [_pad_seq(seq(a, n_aux)) for a in (qkva, oa, ga)]

    c0 = jnp.concatenate([jnp.zeros((n_p, H_C, DK_C, DV_C), f32), jnp.swapaxes(c_past.astype(f32), 2, 3)], axis=0)
    n0 = jnp.concatenate([jnp.zeros((n_p, H_C, DK_C), f32), n_past.astype(f32)], axis=0)
    m0 = jnp.concatenate([jnp.zeros((n_p, 1, LANE), f32), _pad_lanes(m_past.astype(f32))[:, None, :]], axis=0)
    ha, c_a, n_a, m_a = _mlstm_mixer(qkva, oa, ga, c0, n0, m0, w, SEQ_PAD, t_aux)
    lc = min(t_main, MLSTM_CHUNK)
    hm, c_m, n_m, m_m = _mlstm_mixer(qkvm, om, gm, c_a[:n_p], n_a[:n_p], m_a[:n_p], w, lc, lc)

    unseq = lambda a: a.reshape(-1, a.shape[-1])
    x1m_f, x1m_b = _outproj_ln([unseq(hm)], w["w_out"], xm_f, w["ln_g"], w["ln_b"])
    x1a_f, x1a_b = _outproj_ln([unseq(ha[:, :t_aux])], w["w_out"], xa_f, w["ln_g"], w["ln_b"])
    new = dict(c_p=jnp.swapaxes(c_m, 2, 3), n_p=n_m, m_p=m_m[:, 0, :H_C],
               c_s=jnp.swapaxes(c_a[n_p:], 2, 3), n_s=n_a[n_p:], m_s=m_a[n_p:, 0, :H_C])
    return (x1m_f, x1m_b, x1a_f, x1a_b), new


def _even_weights(e, w_in_a, b_fgate_a, conv_w, conv_b, dt_bias, a_log, d_skip, ssd_norm_g, w_out_a):
    wi = w_in_a[e]
    o_f = 3 * D_A
    o_z = o_f + H_A
    o_dt = o_z + D_SSM + CONV_DIM
    w_small = jnp.concatenate([_pad_lanes(wi[:, o_dt:o_dt + H_B]), _pad_lanes(wi[:, o_f:o_f + H_A])], axis=1)
    expand = (np.arange(LANE)[:, None] == (np.arange(D_SSM) // P_B)[None, :])
    return dict(
        w_q=wi[:, :D_A].astype(bf16), w_k=wi[:, D_A:2 * D_A].astype(bf16),
        w_v=wi[:, 2 * D_A:3 * D_A].astype(bf16),
        w_zx=wi[:, o_z:o_dt].astype(bf16), w_small=w_small.astype(bf16),
        b_f=_pad_lanes(b_fgate_a[e][None, :]),
        conv_w=jnp.pad(conv_w[e], ((0, SUBLANE - CONV_W), (0, 0))), conv_b=conv_b[e][None, :],
        dt_bias=_pad_lanes(dt_bias[e][None, :]), a_log_row=_pad_lanes(a_log[e][None, :]),
        a_log_col=a_log[e][:, None], d_skip=jnp.repeat(d_skip[e], P_B)[None, :],
        norm_g=ssd_norm_g[e][None, :], expand=jnp.asarray(expand, bf16),
        w_out=w_out_a[e].astype(bf16))


def _odd_weights(o, w_in_c, b_igate_c, b_fgate_c, mlstm_norm_g, w_out_c):
    wi = w_in_c[o]
    o_o = 2 * D_QK_C + D_V_C
    o_g = o_o + D_V_C
    scale = jnp.concatenate([jnp.ones((D_QK_C,), f32), jnp.full((D_QK_C,), DK_C ** -0.5, f32),
                             jnp.ones((D_V_C,), f32)])[None, :]
    return dict(
        w_qkv=wi[:, :o_o].astype(bf16), qkv_scale=scale, w_o=wi[:, o_o:o_g].astype(bf16),
        w_gates=_pad_lanes(wi[:, o_g:]).astype(bf16),
        gate_bias=_pad_lanes(jnp.concatenate([b_igate_c[o], b_fgate_c[o]])[None, :]),
        norm_g=mlstm_norm_g[o][None, :], w_out=w_out_c[o].astype(bf16))


def kernel(x_prompt, x_sample, cache_fox_k, cache_fox_v, cache_fox_logf, state_ssd_conv, state_ssd, state_mlstm_c, state_mlstm_n, state_mlstm_m, meta_tokens, w_in_a, b_fgate_a, conv_w, conv_b, dt_bias, a_log, d_skip, ssd_norm_g, w_out_a, w_in_c, b_igate_c, b_fgate_c, mlstm_norm_g, w_out_c, ln_mix_g, ln_mix_b, ln_ffn_g, ln_ffn_b, w_ffn_gate, w_ffn_up, w_ffn_down):
    n_p, t_main, d = x_prompt.shape
    n_s, t_s, _ = x_sample.shape
    assert t_s == N_META, "aux rows hold equal-length meta and running-stream sequences"
    xm_f = x_prompt.reshape(n_p * t_main, d)
    xa_f = jnp.concatenate([jnp.broadcast_to(meta_tokens.astype(x_prompt.dtype)[None], (n_p, N_META, d)),
                            x_sample], axis=0).reshape((n_p + n_s) * t_s, d)
    xm_b, xa_b = xm_f.astype(bf16), xa_f.astype(bf16)

    ffn_w = (w_ffn_gate.astype(bf16), w_ffn_up.astype(bf16), w_ffn_down.astype(bf16))
    even_out, odd_out = [], []
    for layer in range(DEPTH):
        if layer % 2 == 0:
            e = layer // 2
            w = _even_weights(e, w_in_a, b_fgate_a, conv_w, conv_b, dt_bias, a_log, d_skip, ssd_norm_g, w_out_a)
            cache = (cache_fox_k[e], cache_fox_v[e], cache_fox_logf[e], state_ssd_conv[e], state_ssd[e])
            w["ln_g"], w["ln_b"] = ln_mix_g[layer], ln_mix_b[layer]
            (x1m_f, x1m_b, x1a_f, x1a_b), new = _even_layer(xm_b, xm_f, xa_b, xa_f, n_p, t_main, cache, w)
            even_out.append(new)
        else:
            o = layer // 2
            w = _odd_weights(o, w_in_c, b_igate_c, b_fgate_c, mlstm_norm_g, w_out_c)
            cache = (state_mlstm_c[o], state_mlstm_n[o], state_mlstm_m[o])
            w["ln_g"], w["ln_b"] = ln_mix_g[layer], ln_mix_b[layer]
            (x1m_f, x1m_b, x1a_f, x1a_b), new = _odd_layer(xm_b, xm_f, xa_b, xa_f, n_p, t_main, cache, w)
            odd_out.append(new)
        xm_f, xm_b = _ffn_ln(x1m_b, x1m_f, *ffn_w, layer, ln_ffn_g[layer], ln_ffn_b[layer])
        xa_f, xa_b = _ffn_ln(x1a_b, x1a_f, *ffn_w, layer, ln_ffn_g[layer], ln_ffn_b[layer])

    y_prompt = xm_f.reshape(n_p, t_main, d)
    y_sample = xa_f.reshape(n_p + n_s, t_s, d)[n_p:]
    st = lambda outs, key: jnp.stack([o[key] for o in outs])
    return (y_prompt, y_sample,
            st(even_out, "k_p"), st(even_out, "v_p"), st(even_out, "f_p"), st(even_out, "conv_p"), st(even_out, "h_p"),
            st(odd_out, "c_p"), st(odd_out, "n_p"), st(odd_out, "m_p"),
            st(even_out, "k_s"), st(even_out, "v_s"), st(even_out, "f_s"), st(even_out, "conv_s"), st(even_out, "h_s"),
            st(odd_out, "c_s"), st(odd_out, "n_s"), st(odd_out, "m_s"))
```

```python
import functools

import jax
import jax.numpy as jnp
import numpy as np
from jax import lax
from jax.experimental import pallas as pl
from jax.experimental.pallas import tpu as pltpu

f32 = jnp.float32
bf16 = jnp.bfloat16

D_MODEL = 2048
DEPTH = 2
N_META = 16
H_A, DH_A = 16, 128
D_A = H_A * DH_A
H_B, P_B, G_B, N_B = 32, 64, 4, 128
HG_B = H_B // G_B
D_SSM = H_B * P_B
CONV_W = 4
CONV_DIM = D_SSM + 2 * G_B * N_B
H_C, DK_C, DV_C = 8, 128, 256
D_QK_C = H_C * DK_C
D_V_C = H_C * DV_C
ALPHA = (2 * DEPTH) ** 0.25
NEG = -1e30
LN_EPS = 1e-5
LOG2E = 1.4426950408889634

LANE = 128
SUBLANE = 8
SEQ_PAD = 64
VMEM_LIMIT = 56 * 1024 * 1024
ATTN_BLOCK = 512
CACHE_BLOCK = 1024
SSD_CHUNK = 128
MLSTM_CHUNK = 256


def _cparams(sem):
    return pltpu.CompilerParams(dimension_semantics=sem, vmem_limit_bytes=VMEM_LIMIT)


def _tile(n, pref, mult=16):
    if n <= pref:
        return n
    t = (pref // mult) * mult
    while t >= mult:
        if n % t == 0:
            return t
        t -= mult
    return n


def _softplus(x):
    return jnp.maximum(x, 0.0) + jnp.log1p(jnp.exp(-jnp.abs(x)))


def _log_sigmoid(x):
    return -_softplus(-x)


def _sigmoid(x):
    return 1.0 / (1.0 + jnp.exp(-x))


def _split3(a):
    a1 = a.astype(bf16)
    r1 = a - a1.astype(f32)
    a2 = r1.astype(bf16)
    a3 = (r1 - a2.astype(f32)).astype(bf16)
    return a1, a2, a3


_NN = (((1,), (0,)), ((), ()))
_NT = (((1,), (1,)), ((), ()))
_TN = (((0,), (0,)), ((), ()))


def _dot(a, b, dims=_NN):
    return lax.dot_general(a, b, dims, preferred_element_type=f32)


def _dot_exact_rhs(a_bf16_exact, b_f32, dims=_NN):
    b1, b2, b3 = _split3(b_f32)
    return _dot(a_bf16_exact, b1, dims) + _dot(a_bf16_exact, b2, dims) + _dot(a_bf16_exact, b3, dims)


def _dot_exact_lhs(a_f32, b_bf16_exact, dims=_NN):
    a1, a2, a3 = _split3(a_f32)
    return _dot(a1, b_bf16_exact, dims) + _dot(a2, b_bf16_exact, dims) + _dot(a3, b_bf16_exact, dims)


def _iota2(shape, axis):
    return lax.broadcasted_iota(jnp.int32, shape, axis)


def _tri_lower(n):
    return (_iota2((n, n), 0) >= _iota2((n, n), 1)).astype(bf16)


def _tri_upper(n):
    return (_iota2((n, n), 0) <= _iota2((n, n), 1)).astype(bf16)


def _eye(rows, cols):
    return (_iota2((rows, cols), 0) == _iota2((rows, cols), 1)).astype(bf16)


def _layer_norm(x, g, b):
    mu = jnp.mean(x, axis=-1, keepdims=True)
    xc = x - mu
    var = jnp.mean(xc * xc, axis=-1, keepdims=True)
    return xc * lax.rsqrt(var + LN_EPS) * g + b


def _mm_kernel(x_ref, w_ref, s_ref, *out_refs):
    acc = _dot(x_ref[...], w_ref[...]) * s_ref[...]
    for o in out_refs:
        o[...] = acc.astype(o.dtype)


def _matmul(x, w, out_dtypes, col_scale=None, tm_pref=1024, tn_pref=1024):
    m, k = x.shape
    n = w.shape[1]
    tm = _tile(m, tm_pref)
    tn = _tile(n, tn_pref, LANE)
    if col_scale is None:
        col_scale = jnp.ones((1, n), f32)
    return pl.pallas_call(
        _mm_kernel,
        grid=(m // tm, n // tn),
        in_specs=[pl.BlockSpec((tm, k), lambda i, j: (i, 0)),
                  pl.BlockSpec((k, tn), lambda i, j: (0, j)),
                  pl.BlockSpec((1, tn), lambda i, j: (0, j))],
        out_specs=[pl.BlockSpec((tm, tn), lambda i, j: (i, j)) for _ in out_dtypes],
        out_shape=[jax.ShapeDtypeStruct((m, n), dt) for dt in out_dtypes],
        compiler_params=_cparams(("parallel", "arbitrary")),
        name="proj_matmul",
    )(x, w, col_scale)


def _kv_proj_kernel(x_ref, w_ref, base_ref, o_ref, ob_ref, *, tm, nh):
    del base_ref
    acc = _dot(x_ref[...], w_ref[...])
    ob_ref[...] = acc.astype(bf16)
    for h in range(nh):
        o_ref[0, pl.ds(h, tm, stride=nh), :] = acc[:, h * DH_A:(h + 1) * DH_A]


def _kv_proj(x, w, head_rows, tm_pref=512):
    m, k = x.shape
    n = w.shape[1]
    n_seq, row_off, nh, _ = head_rows.shape
    t_out = row_off + m // n_seq
    tm = _tile(m // n_seq, tm_pref)
    per_seq = m // n_seq // tm
    base = jnp.pad(head_rows.reshape(n_seq, row_off * nh, DH_A), ((0, 0), (0, (t_out - row_off) * nh), (0, 0)))
    o, ob = pl.pallas_call(
        functools.partial(_kv_proj_kernel, tm=tm, nh=nh),
        grid=(m // tm,),
        in_specs=[pl.BlockSpec((tm, k), lambda i: (i, 0)), pl.BlockSpec((k, n), lambda i: (0, 0)),
                  pl.BlockSpec(memory_space=pl.ANY)],
        out_specs=[pl.BlockSpec((pl.Element(1), pl.Element(tm * nh), pl.Element(DH_A)),
                                lambda i: (i // per_seq,
                                           pl.multiple_of((row_off + (i % per_seq) * tm) * nh, SUBLANE), 0)),
                   pl.BlockSpec((tm, n), lambda i: (i, 0))],
        out_shape=[jax.ShapeDtypeStruct((n_seq, t_out * nh, DH_A), f32), jax.ShapeDtypeStruct((m, n), bf16)],
        input_output_aliases={2: 0},
        compiler_params=_cparams(("parallel",)),
        name="kv_proj",
    )(x, w, base)
    return o.reshape(n_seq, t_out, nh, DH_A), ob


def _outproj_ln_kernel(*refs, n_act):
    acts = refs[:n_act]
    w_ref, res_ref, g_ref, b_ref, of_ref, ob_ref = refs[n_act:]
    kw = acts[0].shape[1]
    mix = _dot(acts[0][...], w_ref[0:kw, :])
    for a in range(1, n_act):
        mix = mix + _dot(acts[a][...], w_ref[a * kw:(a + 1) * kw, :])
    y = _layer_norm(ALPHA * res_ref[...] + mix, g_ref[...], b_ref[...])
    of_ref[...] = y
    ob_ref[...] = y.astype(bf16)


def _outproj_ln(acts, w, res, g, b, tm_pref=512):
    m, d = res.shape
    tm = _tile(m, tm_pref)
    n_act = len(acts)
    row = lambda width: pl.BlockSpec((tm, width), lambda i: (i, 0))
    once = lambda shape: pl.BlockSpec(shape, lambda i: (0, 0), pipeline_mode=pl.Buffered(1))
    return pl.pallas_call(
        functools.partial(_outproj_ln_kernel, n_act=n_act),
        grid=(m // tm,),
        in_specs=[row(a.shape[1]) for a in acts] + [once(w.shape), row(d), once((1, d)), once((1, d))],
        out_specs=[row(d), row(d)],
        out_shape=[jax.ShapeDtypeStruct((m, d), f32), jax.ShapeDtypeStruct((m, d), bf16)],
        compiler_params=_cparams(("parallel",)),
        name="outproj_ln",
    )(*acts, w, res, g.reshape(1, d), b.reshape(1, d))


def _ffn_ln_kernel(xb_ref, xf_ref, wg_ref, wu_ref, wd_ref, g_ref, b_ref, of_ref, ob_ref, acc_ref):
    f = pl.program_id(1)

    @pl.when(f == 0)
    def _():
        acc_ref[...] = jnp.zeros_like(acc_ref)

    x = xb_ref[...]
    gate = _dot(x, wg_ref[0])
    up = _dot(x, wu_ref[0])
    h = (gate * _sigmoid(gate) * up).astype(bf16)
    acc_ref[...] += _dot(h, wd_ref[0])

    @pl.when(f == pl.num_programs(1) - 1)
    def _():
        y = _layer_norm(ALPHA * xf_ref[...] + acc_ref[...], g_ref[...], b_ref[...])
        of_ref[...] = y
        ob_ref[...] = y.astype(bf16)


def _ffn_ln(xb, xf, wg, wu, wd, layer, g, b, tm_pref=512, tf_pref=512):
    m, d = xf.shape
    dff = wg.shape[2]
    tm = _tile(m, tm_pref)
    tf = _tile(dff, tf_pref, LANE)
    return pl.pallas_call(
        _ffn_ln_kernel,
        grid=(m // tm, dff // tf),
        in_specs=[pl.BlockSpec((tm, d), lambda i, f: (i, 0)),
                  pl.BlockSpec((tm, d), lambda i, f: (i, 0)),
                  pl.BlockSpec((1, d, tf), lambda i, f: (layer, 0, f)),
                  pl.BlockSpec((1, d, tf), lambda i, f: (layer, 0, f)),
                  pl.BlockSpec((1, tf, d), lambda i, f: (layer, f, 0)),
                  pl.BlockSpec((1, d), lambda i, f: (0, 0)),
                  pl.BlockSpec((1, d), lambda i, f: (0, 0))],
        out_specs=[pl.BlockSpec((tm, d), lambda i, f: (i, 0)),
                   pl.BlockSpec((tm, d), lambda i, f: (i, 0))],
        out_shape=[jax.ShapeDtypeStruct((m, d), f32), jax.ShapeDtypeStruct((m, d), bf16)],
        scratch_shapes=[pltpu.VMEM((tm, d), f32)],
        compiler_params=_cparams(("parallel", "arbitrary")),
        name="ffn_ln",
    )(xb, xf, wg, wu, wd, g.reshape(1, d), b.reshape(1, d))


FORGET_PIECES = 3


def _cumsum_kernel(x_ref, bias_ref, carry_ref, *refs, gate):
    acc_ref = refs[-1]
    c = pl.program_id(1)

    @pl.when(c == 0)
    def _():
        acc_ref[...] = carry_ref[0]

    x = x_ref[0]
    lf = _log_sigmoid(x + bias_ref[...]) if gate else x
    tc = x.shape[0]
    cum = _dot_exact_rhs(_tri_lower(tc), lf) + acc_ref[...]
    acc_ref[...] = cum[tc - 1:tc, :]
    if not gate:
        refs[0][0] = cum
        return
    logf_ref, cum_ref, cols_ref = refs[:3]
    logf_ref[0] = lf
    cum_ref[0] = cum
    r, col = _iota2((LANE, LANE), 0), _iota2((LANE, LANE), 1)
    out = None
    for i, piece in enumerate(_split3(cum * LOG2E)):
        t = _dot(piece, ((col == FORGET_PIECES * r + i) & (r < H_A)).astype(bf16))
        out = t if out is None else out + t
    cols_ref[0] = out.astype(bf16)


def _logf_cumsum(x, bias, carry, gate, tc_pref=512):
    bsz, t, w = x.shape
    tc = _tile(t, tc_pref)
    spec = pl.BlockSpec((1, tc, w), lambda b, c: (b, c, 0))
    out_dtypes = [f32, f32, bf16] if gate else [f32]
    return pl.pallas_call(
        functools.partial(_cumsum_kernel, gate=gate),
        grid=(bsz, t // tc),
        in_specs=[spec,
                  pl.BlockSpec((1, w), lambda b, c: (0, 0)),
                  pl.BlockSpec((1, 1, w), lambda b, c: (b, 0, 0))],
        out_specs=[spec for _ in out_dtypes],
        out_shape=[jax.ShapeDtypeStruct((bsz, t, w), dt) for dt in out_dtypes],
        scratch_shapes=[pltpu.VMEM((1, w), f32)],
        compiler_params=_cparams(("parallel", "arbitrary")),
        name="logf_cumsum",
    )(x, bias, carry)


def _attn_self_kernel(q_ref, k_ref, v_ref, f_ref, o_ref, *, nh):
    d = DH_A
    tq, t2 = q_ref.shape[1], k_ref.shape[1]
    causal = _iota2((tq, t2), 0) >= _iota2((tq, t2), 1)
    for h in range(nh):
        cols = slice(h * d, (h + 1) * d)
        s = _dot(q_ref[0, :, cols], k_ref[0, :, cols], _NT) - f_ref[0, h] * LOG2E
        s = jnp.where(causal, s, NEG)
        p = jnp.exp2(s - jnp.max(s, axis=-1, keepdims=True))
        acc = _dot(p.astype(bf16), v_ref[0, :, cols])
        o_ref[0, :, cols] = (acc / jnp.sum(p, axis=-1, keepdims=True)).astype(o_ref.dtype)


def _fox_attention_self(q, k, v, f):
    bsz, tq, hd = q.shape
    t2 = k.shape[1]
    nh = hd // DH_A
    qspec = pl.BlockSpec((1, tq, hd), lambda b: (b, 0, 0))
    kvspec = pl.BlockSpec((1, t2, hd), lambda b: (b, 0, 0))
    return pl.pallas_call(
        functools.partial(_attn_self_kernel, nh=nh),
        grid=(bsz,),
        in_specs=[qspec, kvspec, kvspec, pl.BlockSpec((1, nh, 1, t2), lambda b: (b, 0, 0, 0))],
        out_specs=qspec,
        out_shape=jax.ShapeDtypeStruct((bsz, tq, hd), bf16),
        compiler_params=_cparams(("parallel",)),
        name="fox_attention_self",
    )(q, k, v, f)


def _attn_cache_kernel(q_ref, k1_ref, v1_ref, f1_ref, k2_ref, v2_ref, f2_ref, o_ref, m_ref, l_ref, acc_ref,
                       *, tk, nh):
    j = pl.program_id(1)
    d = DH_A
    tq = q_ref.shape[1]

    @pl.when(j == 0)
    def _():
        m_ref[...] = jnp.full(m_ref.shape, NEG, f32)
        l_ref[...] = jnp.zeros(l_ref.shape, f32)
        acc_ref[...] = jnp.zeros(acc_ref.shape, f32)

    def update(h, s, v, fk, mask):
        s = s - fk * LOG2E
        if mask is not None:
            s = jnp.where(mask, s, NEG)
        m = m_ref[h][:, 0:1]
        m_new = jnp.maximum(m, jnp.max(s, axis=-1, keepdims=True))
        alpha = jnp.exp2(m - m_new)
        p = jnp.exp2(s - m_new)
        l_ref[h] = jnp.broadcast_to(alpha * l_ref[h][:, 0:1] + jnp.sum(p, axis=-1, keepdims=True), (tq, LANE))
        acc_ref[h] = alpha * acc_ref[h] + _dot(p.astype(bf16), v)
        m_ref[h] = jnp.broadcast_to(m_new, (tq, LANE))

    qs = [q_ref[0, :, h * d:(h + 1) * d] for h in range(nh)]
    ss = [_dot(qs[h], k1_ref[0, pl.ds(h, tk, stride=nh), :].astype(bf16), _NT) for h in range(nh)]
    for h in range(nh):
        update(h, ss[h], v1_ref[0, pl.ds(h, tk, stride=nh), :].astype(bf16), f1_ref[0, h], None)

    @pl.when(j == pl.num_programs(1) - 1)
    def _():
        t2 = k2_ref.shape[1]
        causal = _iota2((tq, t2), 0) >= _iota2((tq, t2), 1)
        for h in range(nh):
            cols = slice(h * d, (h + 1) * d)
            update(h, _dot(qs[h], k2_ref[0, :, cols], _NT), v2_ref[0, :, cols], f2_ref[0, h], causal)
            o_ref[0, :, cols] = (acc_ref[h] / l_ref[h][:, 0:1]).astype(o_ref.dtype)


def _fox_attention_cache(q, k1, v1, f1, k2, v2, f2, tk=512):
    bsz, tq, hd = q.shape
    s, nh = k1.shape[1], k1.shape[2]
    t2 = k2.shape[1]
    flat = lambda a: a.reshape(bsz, s * nh, DH_A)
    qspec = pl.BlockSpec((1, tq, hd), lambda b, j: (b, 0, 0))
    cspec = pl.BlockSpec((1, tk * nh, DH_A), lambda b, j: (b, j, 0))
    nspec = pl.BlockSpec((1, t2, hd), lambda b, j: (b, 0, 0))
    return pl.pallas_call(
        functools.partial(_attn_cache_kernel, tk=tk, nh=nh),
        grid=(bsz, s // tk),
        in_specs=[qspec, cspec, cspec, pl.BlockSpec((1, nh, 1, tk), lambda b, j: (b, 0, 0, j)),
                  nspec, nspec, pl.BlockSpec((1, nh, 1, t2), lambda b, j: (b, 0, 0, 0))],
        out_specs=qspec,
        out_shape=jax.ShapeDtypeStruct((bsz, tq, hd), bf16),
        scratch_shapes=[pltpu.VMEM((nh, tq, LANE), f32), pltpu.VMEM((nh, tq, LANE), f32),
                        pltpu.VMEM((nh, tq, DH_A), f32)],
        compiler_params=_cparams(("parallel", "arbitrary")),
        name="fox_attention_cache",
    )(q, flat(k1), flat(v1), f1, k2, v2, f2)


def _attn_t_kernel(q_ref, k1_ref, v1_ref, c1_ref, k2_ref, v2_ref, c2_ref, o_ref, *, tq, s1_valid, hps):
    hg = pl.program_id(1)
    qi = pl.program_id(2)
    d = DH_A

    lane = _iota2((tq, LANE), 1)
    q_aug = []
    for hh in range(hps):
        lo = FORGET_PIECES * (hg * hps + hh)
        pick = jnp.where((lane >= lo) & (lane < lo + FORGET_PIECES), -1.0, 0.0).astype(bf16)
        q_aug.append(jnp.concatenate([q_ref[0, :, hh * d:(hh + 1) * d], pick], axis=-1))

    def update(carry, s, v, mask):
        m, l, acc = carry
        if mask is not None:
            s = jnp.where(mask, s, NEG)
        m_new = jnp.maximum(m, jnp.max(s, axis=0, keepdims=True))
        alpha = jnp.exp2(m - m_new)
        p = jnp.exp2(s - m_new)
        l = alpha * l + jnp.sum(p, axis=0, keepdims=True)
        acc = alpha * acc + _dot(v, p.astype(bf16), _TN)
        return m_new, l, acc

    def multi(carries, k_augs, vs, mask):
        ss = [_dot(k_augs[hh], q_aug[hh], _NT) for hh in range(hps)]
        return [update(carries[hh], ss[hh], vs[hh], mask) for hh in range(hps)]

    carries = [(jnp.full((1, tq), NEG, f32), jnp.zeros((1, tq), f32), jnp.zeros((d, tq), f32))
               for _ in range(hps)]
    k1_aug = [jnp.concatenate([k1_ref[0, 0:s1_valid, hh * d:(hh + 1) * d], c1_ref[0, 0:s1_valid, :]], axis=-1)
              for hh in range(hps)]
    carries = multi(carries, k1_aug, [v1_ref[0, 0:s1_valid, hh * d:(hh + 1) * d] for hh in range(hps)], None)

    def blk(j, c, mask):
        off = pl.multiple_of(j * tq, tq)
        cols = c2_ref[0, pl.ds(off, tq), :]
        return multi(c, [jnp.concatenate([k2_ref[0, pl.ds(off, tq), hh * d:(hh + 1) * d], cols], axis=-1)
                         for hh in range(hps)],
                     [v2_ref[0, pl.ds(off, tq), hh * d:(hh + 1) * d] for hh in range(hps)], mask)

    carries = lax.fori_loop(0, qi, lambda j, c: blk(j, c, None), carries)
    carries = blk(qi, carries, _iota2((tq, tq), 0) <= _iota2((tq, tq), 1))
    for hh in range(hps):
        m, l, acc = carries[hh]
        o_ref[0, :, hh * d:(hh + 1) * d] = jnp.transpose(acc / l).astype(o_ref.dtype)


def _fox_attention_main(q, k1, v1, c1, k2, v2, c2, s1_valid, tq=512, hps=8):
    bsz, t, hd = q.shape
    s1 = k1.shape[1]
    w = hps * DH_A
    qspec = pl.BlockSpec((1, tq, w), lambda b, h, i: (b, i, h))
    kvspec = pl.BlockSpec((1, t, w), lambda b, h, i: (b, 0, h))
    pspec = pl.BlockSpec((1, s1, w), lambda b, h, i: (b, 0, h))
    return pl.pallas_call(
        functools.partial(_attn_t_kernel, tq=tq, s1_valid=s1_valid, hps=hps),
        grid=(bsz, hd // w, t // tq),
        in_specs=[qspec, pspec, pspec, pl.BlockSpec((1, s1, LANE), lambda b, h, i: (b, 0, 0)),
                  kvspec, kvspec, pl.BlockSpec((1, t, LANE), lambda b, h, i: (b, 0, 0))],
        out_specs=qspec,
        out_shape=jax.ShapeDtypeStruct((bsz, t, hd), bf16),
        compiler_params=_cparams(("parallel", "parallel", "arbitrary")),
        name="fox_attention_main",
    )(q, k1, v1, c1, k2, v2, c2)


def _ssd_kernel(zx_ref, dt_ref, hist_ref, h0_ref, cw_ref, cb_ref, dtb_ref, alog_ref, alogc_ref,
                dskip_ref, ng_ref, e_ref, y_ref, tail_ref, hout_ref, ext_ref, st_ref, *, lc, t_valid):
    c = pl.program_id(1)

    @pl.when(c == 0)
    def _():
        ext_ref[0:SUBLANE, :] = hist_ref[0]
        st_ref[...] = h0_ref[0]

    x_raw = zx_ref[0, :, D_SSM:]
    ext_ref[SUBLANE:SUBLANE + lc, :] = x_raw
    hist = ext_ref[0:SUBLANE, :]
    row8 = _iota2((SUBLANE, CONV_DIM), 0)
    conv = cb_ref[...] + cw_ref[CONV_W - 1:CONV_W, :] * x_raw
    for s in range(1, CONV_W):
        rolled = pltpu.roll(x_raw, s, axis=0)
        first = jnp.where(row8 < s, pltpu.roll(hist, s, axis=0), rolled[0:SUBLANE])
        conv = conv + cw_ref[CONV_W - 1 - s:CONV_W - s, :] * jnp.concatenate([first, rolled[SUBLANE:]], axis=0)
    xc = conv * _sigmoid(conv)
    tail = ext_ref[t_valid:t_valid + SUBLANE, :]
    ext_ref[0:SUBLANE, :] = tail

    dt = _softplus(dt_ref[0] + dtb_ref[...])
    if t_valid < lc:
        dt = jnp.where(_iota2((lc, LANE), 0) < t_valid, dt, 0.0)
    a_row = -jnp.exp(alog_ref[...])
    a_col = -jnp.exp(alogc_ref[...])
    a_cum = _dot_exact_rhs(_tri_lower(lc), dt * a_row)
    dt_t = _dot_exact_rhs(_eye(H_B, LANE), dt, _NT)
    a_cum_t = _dot_exact_lhs(dt_t * a_col, _tri_upper(lc))
    a_last = a_cum[lc - 1:lc, :]
    e = e_ref[...]
    to_end_x = _dot_exact_lhs(jnp.exp(a_last - a_cum) * dt, e)
    ea_x = _dot_exact_lhs(jnp.exp(a_cum), e)
    cdec_x = _dot_exact_lhs(jnp.broadcast_to(jnp.exp(a_last), (SUBLANE, LANE)), e)[0:1, :]

    causal = _iota2((lc, lc), 0) >= _iota2((lc, lc), 1)
    left = _iota2((lc, LANE), 1) < P_B
    eye_n = _eye(N_B, N_B)
    gw = D_SSM // G_B
    groups = range(G_B)
    gcols = [slice(g * gw, (g + 1) * gw) for g in groups]
    bg = [xc[:, D_SSM + g * N_B:D_SSM + (g + 1) * N_B].astype(bf16) for g in groups]
    cg = [xc[:, D_SSM + G_B * N_B + g * N_B:D_SSM + G_B * N_B + (g + 1) * N_B].astype(bf16) for g in groups]
    cb = [_dot(cg[g], bg[g], _NT) for g in groups]
    bg_t = [_dot(eye_n, bg[g], _NT).astype(bf16) for g in groups]
    st_in = [st_ref[:, gcols[g]] for g in groups]
    y_off = [_dot(cg[g], st_in[g].astype(bf16)) * ea_x[:, gcols[g]] for g in groups]
    upd = [_dot(bg_t[g], (xc[:, gcols[g]] * to_end_x[:, gcols[g]]).astype(bf16)) for g in groups]
    for g in groups:
        st_ref[:, gcols[g]] = cdec_x[:, gcols[g]] * st_in[g] + upd[g]

    def decay_weights(h):
        seg = a_cum[:, h:h + 1] - a_cum_t[h:h + 1, :]
        dec = jnp.exp(jnp.where(causal, seg, NEG))
        return (cb[h // HG_B] * dec * dt_t[h:h + 1, :]).astype(bf16)

    w_all = [decay_weights(h) for h in range(H_B)]
    y_pair = []
    for j in range(D_SSM // LANE):
        xp = xc[:, j * LANE:(j + 1) * LANE].astype(bf16)
        y_pair.append(jnp.where(left, _dot(w_all[2 * j], xp), _dot(w_all[2 * j + 1], xp)))
    ppg = gw // LANE
    for g in groups:
        xg = xc[:, gcols[g]]
        y = jnp.concatenate(y_pair[g * ppg:(g + 1) * ppg], axis=-1) + y_off[g] + dskip_ref[:, gcols[g]] * xg
        z = zx_ref[0, :, g * gw:(g + 1) * gw]
        y = y * (z * _sigmoid(z))
        y = y * lax.rsqrt(jnp.mean(y * y, axis=-1, keepdims=True) + LN_EPS) * ng_ref[:, g * gw:(g + 1) * gw]
        y_ref[0, :, g * gw:(g + 1) * gw] = y.astype(y_ref.dtype)

    @pl.when(c == pl.num_programs(1) - 1)
    def _():
        tail_ref[0] = tail
        hout_ref[0] = st_ref[...]


def _ssd_mixer(zx, dt_raw, hist, h0_t, p, lc, t_valid):
    bsz, t, wz = zx.shape
    const = lambda shape: pl.BlockSpec(shape, lambda b, c: (0,) * len(shape))
    return pl.pallas_call(
        functools.partial(_ssd_kernel, lc=lc, t_valid=t_valid),
        grid=(bsz, t // lc),
        in_specs=[pl.BlockSpec((1, lc, wz), lambda b, c: (b, c, 0)),
                  pl.BlockSpec((1, lc, LANE), lambda b, c: (b, c, 0)),
                  pl.BlockSpec((1, SUBLANE, CONV_DIM), lambda b, c: (b, 0, 0)),
                  pl.BlockSpec((1, N_B, D_SSM), lambda b, c: (b, 0, 0)),
                  const((SUBLANE, CONV_DIM)), const((1, CONV_DIM)), const((1, LANE)), const((1, LANE)),
                  const((H_B, 1)), const((1, D_SSM)), const((1, D_SSM)), const((LANE, D_SSM))],
        out_specs=[pl.BlockSpec((1, lc, D_SSM), lambda b, c: (b, c, 0)),
                   pl.BlockSpec((1, SUBLANE, CONV_DIM), lambda b, c: (b, 0, 0)),
                   pl.BlockSpec((1, N_B, D_SSM), lambda b, c: (b, 0, 0))],
        out_shape=[jax.ShapeDtypeStruct((bsz, t, D_SSM), bf16),
                   jax.ShapeDtypeStruct((bsz, SUBLANE, CONV_DIM), f32),
                   jax.ShapeDtypeStruct((bsz, N_B, D_SSM), f32)],
        scratch_shapes=[pltpu.VMEM((SUBLANE + lc, CONV_DIM), f32), pltpu.VMEM((N_B, D_SSM), f32)],
        compiler_params=_cparams(("parallel", "arbitrary")),
        name="ssd_mixer",
    )(zx, dt_raw, hist, h0_t, p["conv_w"], p["conv_b"], p["dt_bias"], p["a_log_row"], p["a_log_col"],
      p["d_skip"], p["norm_g"], p["expand"])


def _mlstm_kernel(qkv_ref, o_ref, gates_ref, gb_ref, ng_ref, c0_ref, n0_ref, m0_ref,
                  y_ref, cout_ref, nout_ref, mout_ref, ct_ref, n_ref, m_ref, *, lc, t_valid):
    c = pl.program_id(1)

    @pl.when(c == 0)
    def _():
        ct_ref[...] = c0_ref[0]
        n_ref[...] = n0_ref[0]
        m_ref[...] = m0_ref[0]

    lane = _iota2((lc, LANE), 1)
    g = gates_ref[0] + gb_ref[...]
    is_f = (lane >= H_C) & (lane < 2 * H_C)
    lf = jnp.where(is_f, _log_sigmoid(g), 0.0)
    li = g
    if t_valid < lc:
        valid = _iota2((lc, LANE), 0) < t_valid
        lf = jnp.where(valid, lf, 0.0)
        li = jnp.where(valid, li, NEG)
    b_cum = _dot_exact_rhs(_tri_lower(lc), lf)
    rows = 2 * H_C
    li_t = _dot_exact_rhs(_eye(rows, LANE), li, _NT)
    lf_t = _dot_exact_rhs(_eye(rows, LANE), lf, _NT)
    b_cum_t = _dot_exact_lhs(lf_t, _tri_upper(lc))

    causal = _iota2((lc, lc), 0) >= _iota2((lc, lc), 1)
    eye_k = _eye(DK_C, DK_C)
    m_all = m_ref[...]
    m_out = m_all
    q_of = lambda h: qkv_ref[0, :, h * DK_C:(h + 1) * DK_C]
    k_of = lambda h: qkv_ref[0, :, D_QK_C + h * DK_C:D_QK_C + (h + 1) * DK_C]
    v_of = lambda h: qkv_ref[0, :, 2 * D_QK_C + h * DV_C:2 * D_QK_C + (h + 1) * DV_C]
    heads = range(H_C)
    qk_all = [_dot(q_of(h), k_of(h), _NT) for h in heads]
    qc_all = [_dot(q_of(h), ct_ref[h].astype(bf16)) for h in heads]
    kt_all = [_dot(eye_k, k_of(h), _NT) for h in heads]
    bcol = [b_cum[:, H_C + h:H_C + h + 1] for h in heads]
    brow = [b_cum_t[H_C + h:H_C + h + 1, :] for h in heads]
    lirow = [li_t[h:h + 1, :] for h in heads]
    m_st = [m_all[:, h:h + 1] for h in heads]
    d_mat = [jnp.where(causal, bcol[h] - brow[h] + lirow[h], NEG) for h in heads]
    inter = [bcol[h] + m_st[h] for h in heads]
    m_row = [jnp.maximum(inter[h], jnp.max(d_mat[h], axis=-1, keepdims=True)) for h in heads]
    w = [jnp.exp(d_mat[h] - m_row[h]) * qk_all[h] for h in heads]
    g_inter = [jnp.exp(inter[h] - m_row[h]) for h in heads]
    n_st = [n_ref[h:h + 1, :] for h in heads]
    num = [_dot(w[h].astype(bf16), v_of(h)) + g_inter[h] * qc_all[h] for h in heads]
    den = [jnp.sum(w[h], axis=-1, keepdims=True)
           + g_inter[h] * jnp.sum(q_of(h).astype(f32) * n_st[h], axis=-1, keepdims=True) for h in heads]
    hh = [num[h] / jnp.maximum(jnp.abs(den[h]), jnp.exp(-m_row[h])) for h in heads]
    ms = [jnp.mean(hh[h] * hh[h], axis=-1, keepdims=True) for h in heads]
    for h in heads:
        cols = slice(h * DV_C, (h + 1) * DV_C)
        y = hh[h] * lax.rsqrt(ms[h] + LN_EPS) * ng_ref[:, cols] * _sigmoid(o_ref[0, :, cols])
        y_ref[0, :, cols] = y.astype(y_ref.dtype)
    b_tot = [bcol[h][lc - 1:lc, :] for h in heads]
    d_end = [b_tot[h] - brow[h] + lirow[h] for h in heads]
    m_new = [jnp.maximum(b_tot[h] + m_st[h], jnp.max(d_end[h], axis=-1, keepdims=True)) for h in heads]
    w_end = [jnp.exp(d_end[h] - m_new[h]) for h in heads]
    g_old = [jnp.exp(b_tot[h] + m_st[h] - m_new[h]) for h in heads]
    for h in heads:
        ct_ref[h] = g_old[h] * ct_ref[h] + _dot((kt_all[h] * w_end[h]).astype(bf16), v_of(h))
        n_upd = _dot(jnp.broadcast_to(w_end[h], (SUBLANE, lc)).astype(bf16), k_of(h))[0:1, :]
        n_ref[h:h + 1, :] = g_old[h] * n_st[h] + n_upd
        m_out = jnp.where(lane[0:1, :] == h, m_new[h], m_out)
    m_ref[...] = m_out

    @pl.when(c == pl.num_programs(1) - 1)
    def _():
        cout_ref[0] = ct_ref[...]
        nout_ref[0] = n_ref[...]
        mout_ref[0] = m_ref[...]


def _mlstm_mixer(qkv, o, gates, c0_t, n0, m0, p, lc, t_valid):
    bsz, t, wq = qkv.shape
    const = lambda shape: pl.BlockSpec(shape, lambda b, c: (0,) * len(shape))
    return pl.pallas_call(
        functools.partial(_mlstm_kernel, lc=lc, t_valid=t_valid),
        grid=(bsz, t // lc),
        in_specs=[pl.BlockSpec((1, lc, wq), lambda b, c: (b, c, 0)),
                  pl.BlockSpec((1, lc, D_V_C), lambda b, c: (b, c, 0)),
                  pl.BlockSpec((1, lc, LANE), lambda b, c: (b, c, 0)),
                  const((1, LANE)), const((1, D_V_C)),
                  pl.BlockSpec((1, H_C, DK_C, DV_C), lambda b, c: (b, 0, 0, 0)),
                  pl.BlockSpec((1, H_C, DK_C), lambda b, c: (b, 0, 0)),
                  pl.BlockSpec((1, 1, LANE), lambda b, c: (b, 0, 0))],
        out_specs=[pl.BlockSpec((1, lc, D_V_C), lambda b, c: (b, c, 0)),
                   pl.BlockSpec((1, H_C, DK_C, DV_C), lambda b, c: (b, 0, 0, 0)),
                   pl.BlockSpec((1, H_C, DK_C), lambda b, c: (b, 0, 0)),
                   pl.BlockSpec((1, 1, LANE), lambda b, c: (b, 0, 0))],
        out_shape=[jax.ShapeDtypeStruct((bsz, t, D_V_C), bf16),
                   jax.ShapeDtypeStruct((bsz, H_C, DK_C, DV_C), f32),
                   jax.ShapeDtypeStruct((bsz, H_C, DK_C), f32),
                   jax.ShapeDtypeStruct((bsz, 1, LANE), f32)],
        scratch_shapes=[pltpu.VMEM((H_C, DK_C, DV_C), f32), pltpu.VMEM((H_C, DK_C), f32),
                        pltpu.VMEM((1, LANE), f32)],
        compiler_params=_cparams(("parallel", "arbitrary")),
        name="mlstm_mixer",
    )(qkv, o, gates, p["gate_bias"], p["norm_g"], c0_t, n0, m0)


def _pad_lanes(a, width=LANE):
    return jnp.pad(a, [(0, 0)] * (a.ndim - 1) + [(0, width - a.shape[-1])])


def _pad_seq(a, t_pad=SEQ_PAD):
    return jnp.pad(a, [(0, 0), (0, t_pad - a.shape[1])] + [(0, 0)] * (a.ndim - 2))


def _even_layer(xm_b, xm_f, xa_b, xa_f, n_p, t_main, cache, w):
    k_past, v_past, logf_past, conv_past, h_past = cache
    n_s = k_past.shape[0]
    n_aux = n_p + n_s
    t_aux = xa_f.shape[0] // n_aux
    past = k_past.shape[1]

    seq = lambda a, n: a.reshape(n, a.shape[0] // n, a.shape[1])
    heads = lambda a: a.reshape(a.shape[0], a.shape[1], H_A, DH_A)

    def project(xb, meta_kv):
        q, = _matmul(xb, w["w_q"], [bf16], col_scale=jnp.full((1, D_A), DH_A ** -0.5 * LOG2E, f32))
        if meta_kv is None:
            k_f, k_b = _matmul(xb, w["w_k"], [f32, bf16])
            v_f, v_b = _matmul(xb, w["w_v"], [f32, bf16])
        else:
            k_f, k_b = _kv_proj(xb, w["w_k"], meta_kv[0])
            v_f, v_b = _kv_proj(xb, w["w_v"], meta_kv[1])
        zx, = _matmul(xb, w["w_zx"], [f32])
        sm, = _matmul(xb, w["w_small"], [f32], tn_pref=2 * LANE)
        return q, k_f, k_b, v_f, v_b, zx, sm

    qa, ka_f, ka_b, va_f, va_b, zxa, sma = project(xa_b, None)
    qa, ka_f, va_f = [seq(a, n_aux) for a in (qa, ka_f, va_f)]
    ka_b, va_b, zxa, sma = [_pad_seq(seq(a, n_aux)) for a in (ka_b, va_b, zxa, sma)]
    ka_f, va_f = heads(ka_f), heads(va_f)
    qm, kp_f, km_b, vp_f, vm_b, zxm, smm = project(xm_b, (ka_f[:n_p], va_f[:n_p]))
    qm, km_b, vm_b, zxm, smm = [seq(a, n_p) for a in (qm, km_b, vm_b, zxm, smm)]

    zero_c = jnp.zeros((n_s, 1, LANE), f32)
    cum_past, = _logf_cumsum(_pad_lanes(logf_past.astype(f32)), w["b_f"], zero_c, gate=False)
    carry_a = jnp.concatenate([jnp.zeros((n_p, 1, LANE), f32), cum_past[:, past - 1:past]], axis=0)
    logf_a, cum_a, cols_a = _logf_cumsum(sma[:, :, LANE:], w["b_f"], carry_a, gate=True)
    logf_m, _, cols_m = _logf_cumsum(smm[:, :, LANE:], w["b_f"], cum_a[:n_p, t_aux - 1:t_aux], gate=True)
    rows = lambda cum: jnp.swapaxes(cum[:, :, :H_A], 1, 2)[:, :, None, :]

    fa = rows(cum_a)
    att_meta = _fox_attention_self(qa[:n_p], ka_b[:n_p], va_b[:n_p], fa[:n_p])
    att_s = _fox_attention_cache(qa[n_p:], k_past, v_past, rows(cum_past), ka_b[n_p:], va_b[n_p:], fa[n_p:],
                                 tk=min(past, CACHE_BLOCK))
    att_m = _fox_attention_main(qm, ka_b[:n_p], va_b[:n_p], cols_a[:n_p], km_b, vm_b, cols_m, t_aux,
                                tq=min(t_main, ATTN_BLOCK))
    att_a = jnp.concatenate([att_meta, att_s], axis=0)

    hist_a = jnp.concatenate([jnp.zeros((n_p, SUBLANE, CONV_DIM), f32),
                              jnp.pad(conv_past.astype(f32), ((0, 0), (SUBLANE - (CONV_W - 1), 0), (0, 0)))], axis=0)
    to_t = lambda h: jnp.transpose(h.reshape(-1, D_SSM, N_B), (0, 2, 1))
    from_t = lambda h: jnp.transpose(h, (0, 2, 1)).reshape(-1, H_B, P_B, N_B)
    h0_a = jnp.concatenate([jnp.zeros((n_p, N_B, D_SSM), f32), to_t(h_past.astype(f32))], axis=0)
    ya, tail_a, h_a = _ssd_mixer(zxa, sma[:, :, :LANE], hist_a, h0_a, w, SEQ_PAD, t_aux)
    ym, tail_m, h_m = _ssd_mixer(zxm, smm[:, :, :LANE], tail_a[:n_p], h_a[:n_p], w, min(t_main, SSD_CHUNK), min(t_main, SSD_CHUNK))

    unseq = lambda a: a.reshape(-1, a.shape[-1])
    x1m_f, x1m_b = _outproj_ln([unseq(att_m), unseq(ym)], w["w_out"], xm_f, w["ln_g"], w["ln_b"])
    x1a_f, x1a_b = _outproj_ln([unseq(att_a), unseq(ya[:, :t_aux])], w["w_out"], xa_f, w["ln_g"], w["ln_b"])

    new = dict(
        k_p=kp_f, v_p=vp_f,
        f_p=jnp.concatenate([logf_a[:n_p, :t_aux, :H_A], logf_m[:, :, :H_A]], axis=1),
        conv_p=tail_m[:, SUBLANE - (CONV_W - 1):], h_p=from_t(h_m),
        k_s=ka_f[n_p:], v_s=va_f[n_p:], f_s=logf_a[n_p:, :t_aux, :H_A],
        conv_s=tail_a[n_p:, SUBLANE - (CONV_W - 1):], h_s=from_t(h_a[n_p:]))
    return (x1m_f, x1m_b, x1a_f, x1a_b), new


def _odd_layer(xm_b, xm_f, xa_b, xa_f, n_p, t_main, cache, w):
    c_past, n_past, m_past = cache
    n_s = c_past.shape[0]
    n_aux = n_p + n_s
    t_aux = xa_f.shape[0] // n_aux

    def project(xb):
        qkv, = _matmul(xb, w["w_qkv"], [bf16], col_scale=w["qkv_scale"])
        o, = _matmul(xb, w["w_o"], [f32])
        gt, = _matmul(xb, w["w_gates"], [f32])
        return qkv, o, gt

    qkvm, om, gm = project(xm_b)
    qkva, oa, ga = project(xa_b)
    seq = lambda a, n: a.reshape(n, a.shape[0] // n, a.shape[1])
    qkvm, om, gm = [seq(a, n_p) for a in (qkvm, om, gm)]
    qkva, oa, ga = [_pad_seq(seq(a, n_aux)) for a in (qkva, oa, ga)]

    c0 = jnp.concatenate([jnp.zeros((n_p, H_C, DK_C, DV_C), f32), jnp.swapaxes(c_past.astype(f32), 2, 3)], axis=0)
    n0 = jnp.concatenate([jnp.zeros((n_p, H_C, DK_C), f32), n_past.astype(f32)], axis=0)
    m0 = jnp.concatenate([jnp.zeros((n_p, 1, LANE), f32), _pad_lanes(m_past.astype(f32))[:, None, :]], axis=0)
    ha, c_a, n_a, m_a = _mlstm_mixer(qkva, oa, ga, c0, n0, m0, w, SEQ_PAD, t_aux)
    lc = min(t_main, MLSTM_CHUNK)
    hm, c_m, n_m, m_m = _mlstm_mixer(qkvm, om, gm, c_a[:n_p], n_a[:n_p], m_a[:n_p], w, lc, lc)

    unseq = lambda a: a.reshape(-1, a.shape[-1])
    x1m_f, x1m_b = _outproj_ln([unseq(hm)], w["w_out"], xm_f, w["ln_g"], w["ln_b"])
    x1a_f, x1a_b = _outproj_ln([unseq(ha[:, :t_aux])], w["w_out"], xa_f, w["ln_g"], w["ln_b"])
    new = dict(c_p=jnp.swapaxes(c_m, 2, 3), n_p=n_m, m_p=m_m[:, 0, :H_C],
               c_s=jnp.swapaxes(c_a[n_p:], 2, 3), n_s=n_a[n_p:], m_s=m_a[n_p:, 0, :H_C])
    return (x1m_f, x1m_b, x1a_f, x1a_b), new


def _even_weights(e, w_in_a, b_fgate_a, conv_w, conv_b, dt_bias, a_log, d_skip, ssd_norm_g, w_out_a):
    wi = w_in_a[e]
    o_f = 3 * D_A
    o_z = o_f + H_A
    o_dt = o_z + D_SSM + CONV_DIM
    w_small = jnp.concatenate([_pad_lanes(wi[:, o_dt:o_dt + H_B]), _pad_lanes(wi[:, o_f:o_f + H_A])], axis=1)
    expand = (np.arange(LANE)[:, None] == (np.arange(D_SSM) // P_B)[None, :])
    return dict(
        w_q=wi[:, :D_A].astype(bf16), w_k=wi[:, D_A:2 * D_A].astype(bf16),
        w_v=wi[:, 2 * D_A:3 * D_A].astype(bf16),
        w_zx=wi[:, o_z:o_dt].astype(bf16), w_small=w_small.astype(bf16),
        b_f=_pad_lanes(b_fgate_a[e][None, :]),
        conv_w=jnp.pad(conv_w[e], ((0, SUBLANE - CONV_W), (0, 0))), conv_b=conv_b[e][None, :],
        dt_bias=_pad_lanes(dt_bias[e][None, :]), a_log_row=_pad_lanes(a_log[e][None, :]),
        a_log_col=a_log[e][:, None], d_skip=jnp.repeat(d_skip[e], P_B)[None, :],
        norm_g=ssd_norm_g[e][None, :], expand=jnp.asarray(expand, bf16),
        w_out=w_out_a[e].astype(bf16))


def _odd_weights(o, w_in_c, b_igate_c, b_fgate_c, mlstm_norm_g, w_out_c):
    wi = w_in_c[o]
    o_o = 2 * D_QK_C + D_V_C
    o_g = o_o + D_V_C
    scale = jnp.concatenate([jnp.ones((D_QK_C,), f32), jnp.full((D_QK_C,), DK_C ** -0.5, f32),
                             jnp.ones((D_V_C,), f32)])[None, :]
    return dict(
        w_qkv=wi[:, :o_o].astype(bf16), qkv_scale=scale, w_o=wi[:, o_o:o_g].astype(bf16),
        w_gates=_pad_lanes(wi[:, o_g:]).astype(bf16),
        gate_bias=_pad_lanes(jnp.concatenate([b_igate_c[o], b_fgate_c[o]])[None, :]),
        norm_g=mlstm_norm_g[o][None, :], w_out=w_out_c[o].astype(bf16))


def kernel(x_prompt, x_sample, cache_fox_k, cache_fox_v, cache_fox_logf, state_ssd_conv, state_ssd, state_mlstm_c, state_mlstm_n, state_mlstm_m, meta_tokens, w_in_a, b_fgate_a, conv_w, conv_b, dt_bias, a_log, d_skip, ssd_norm_g, w_out_a, w_in_c, b_igate_c, b_fgate_c, mlstm_norm_g, w_out_c, ln_mix_g, ln_mix_b, ln_ffn_g, ln_ffn_b, w_ffn_gate, w_ffn_up, w_ffn_down):
    n_p, t_main, d = x_prompt.shape
    n_s, t_s, _ = x_sample.shape
    assert t_s == N_META, "aux rows hold equal-length meta and running-stream sequences"
    xm_f = x_prompt.reshape(n_p * t_main, d)
    xa_f = jnp.concatenate([jnp.broadcast_to(meta_tokens.astype(x_prompt.dtype)[None], (n_p, N_META, d)),
                            x_sample], axis=0).reshape((n_p + n_s) * t_s, d)
    xm_b, xa_b = xm_f.astype(bf16), xa_f.astype(bf16)

    ffn_w = (w_ffn_gate.astype(bf16), w_ffn_up.astype(bf16), w_ffn_down.astype(bf16))
    even_out, odd_out = [], []
    for layer in range(DEPTH):
        if layer % 2 == 0:
            e = layer // 2
            w = _even_weights(e, w_in_a, b_fgate_a, conv_w, conv_b, dt_bias, a_log, d_skip, ssd_norm_g, w_out_a)
            cache = (cache_fox_k[e], cache_fox_v[e], cache_fox_logf[e], state_ssd_conv[e], state_ssd[e])
            w["ln_g"], w["ln_b"] = ln_mix_g[layer], ln_mix_b[layer]
            (x1m_f, x1m_b, x1a_f, x1a_b), new = _even_layer(xm_b, xm_f, xa_b, xa_f, n_p, t_main, cache, w)
            even_out.append(new)
        else:
            o = layer // 2
            w = _odd_weights(o, w_in_c, b_igate_c, b_fgate_c, mlstm_norm_g, w_out_c)
            cache = (state_mlstm_c[o], state_mlstm_n[o], state_mlstm_m[o])
            w["ln_g"], w["ln_b"] = ln_mix_g[layer], ln_mix_b[layer]
            (x1m_f, x1m_b, x1a_f, x1a_b), new = _odd_layer(xm_b, xm_f, xa_b, xa_f, n_p, t_main, cache, w)
            odd_out.append(new)
        xm_f, xm_b = _ffn_ln(x1m_b, x1m_f, *ffn_w, layer, ln_ffn_g[layer], ln_ffn_b[layer])
        xa_f, xa_b = _ffn_ln(x1a_b, x1a_f, *ffn_w, layer, ln_ffn_g[layer], ln_ffn_b[layer])

    y_prompt = xm_f.reshape(n_p, t_main, d)
    y_sample = xa_f.reshape(n_p + n_s, t_s, d)[n_p:]
    st = lambda outs, key: jnp.stack([o[key] for o in outs])
    return (y_prompt, y_sample,
            st(even_out, "k_p"), st(even_out, "v_p"), st(even_out, "f_p"), st(even_out, "conv_p"), st(even_out, "h_p"),
            st(odd_out, "c_p"), st(odd_out, "n_p"), st(odd_out, "m_p"),
            st(even_out, "k_s"), st(even_out, "v_s"), st(even_out, "f_s"), st(even_out, "conv_s"), st(even_out, "h_s"),
            st(odd_out, "c_s"), st(odd_out, "n_s"), st(odd_out, "m_s"))
```

```python
import functools

import jax
import jax.numpy as jnp
import numpy as np
from jax import lax
from jax.experimental import pallas as pl
from jax.experimental.pallas import tpu as pltpu

f32 = jnp.float32
bf16 = jnp.bfloat16

D_MODEL = 2048
DEPTH = 2
N_META = 16
H_A, DH_A = 16, 128
D_A = H_A * DH_A
H_B, P_B, G_B, N_B = 32, 64, 4, 128
HG_B = H_B // G_B
D_SSM = H_B * P_B
CONV_W = 4
CONV_DIM = D_SSM + 2 * G_B * N_B
H_C, DK_C, DV_C = 8, 128, 256
D_QK_C = H_C * DK_C
D_V_C = H_C * DV_C
ALPHA = (2 * DEPTH) ** 0.25
NEG = -1e30
LN_EPS = 1e-5
LOG2E = 1.4426950408889634

LANE = 128
SUBLANE = 8
SEQ_PAD = 64
VMEM_LIMIT = 56 * 1024 * 1024
ATTN_BLOCK = 512
CACHE_BLOCK = 1024
SSD_CHUNK = 256
MLSTM_CHUNK = 256


def _cparams(sem):
    return pltpu.CompilerParams(dimension_semantics=sem, vmem_limit_bytes=VMEM_LIMIT)


def _tile(n, pref, mult=16):
    if n <= pref:
        return n
    t = (pref // mult) * mult
    while t >= mult:
        if n % t == 0:
            return t
        t -= mult
    return n


def _softplus(x):
    return jnp.maximum(x, 0.0) + jnp.log1p(jnp.exp(-jnp.abs(x)))


def _log_sigmoid(x):
    return -_softplus(-x)


def _sigmoid(x):
    return 1.0 / (1.0 + jnp.exp(-x))


def _split3(a):
    a1 = a.astype(bf16)
    r1 = a - a1.astype(f32)
    a2 = r1.astype(bf16)
    a3 = (r1 - a2.astype(f32)).astype(bf16)
    return a1, a2, a3


_NN = (((1,), (0,)), ((), ()))
_NT = (((1,), (1,)), ((), ()))
_TN = (((0,), (0,)), ((), ()))


def _dot(a, b, dims=_NN):
    return lax.dot_general(a, b, dims, preferred_element_type=f32)


def _dot_exact_rhs(a_bf16_exact, b_f32, dims=_NN):
    b1, b2, b3 = _split3(b_f32)
    return _dot(a_bf16_exact, b1, dims) + _dot(a_bf16_exact, b2, dims) + _dot(a_bf16_exact, b3, dims)


def _dot_exact_lhs(a_f32, b_bf16_exact, dims=_NN):
    a1, a2, a3 = _split3(a_f32)
    return _dot(a1, b_bf16_exact, dims) + _dot(a2, b_bf16_exact, dims) + _dot(a3, b_bf16_exact, dims)


def _iota2(shape, axis):
    return lax.broadcasted_iota(jnp.int32, shape, axis)


def _tri_lower(n):
    return (_iota2((n, n), 0) >= _iota2((n, n), 1)).astype(bf16)


def _tri_upper(n):
    return (_iota2((n, n), 0) <= _iota2((n, n), 1)).astype(bf16)


def _eye(rows, cols):
    return (_iota2((rows, cols), 0) == _iota2((rows, cols), 1)).astype(bf16)


def _layer_norm(x, g, b):
    mu = jnp.mean(x, axis=-1, keepdims=True)
    xc = x - mu
    var = jnp.mean(xc * xc, axis=-1, keepdims=True)
    return xc * lax.rsqrt(var + LN_EPS) * g + b


def _mm_kernel(x_ref, w_ref, s_ref, *out_refs):
    acc = _dot(x_ref[...], w_ref[...]) * s_ref[...]
    for o in out_refs:
        o[...] = acc.astype(o.dtype)


def _matmul(x, w, out_dtypes, col_scale=None, tm_pref=1024, tn_pref=1024):
    m, k = x.shape
    n = w.shape[1]
    tm = _tile(m, tm_pref)
    tn = _tile(n, tn_pref, LANE)
    if col_scale is None:
        col_scale = jnp.ones((1, n), f32)
    return pl.pallas_call(
        _mm_kernel,
        grid=(m // tm, n // tn),
        in_specs=[pl.BlockSpec((tm, k), lambda i, j: (i, 0)),
                  pl.BlockSpec((k, tn), lambda i, j: (0, j)),
                  pl.BlockSpec((1, tn), lambda i, j: (0, j))],
        out_specs=[pl.BlockSpec((tm, tn), lambda i, j: (i, j)) for _ in out_dtypes],
        out_shape=[jax.ShapeDtypeStruct((m, n), dt) for dt in out_dtypes],
        compiler_params=_cparams(("parallel", "arbitrary")),
        name="proj_matmul",
    )(x, w, col_scale)


def _kv_proj_kernel(x_ref, w_ref, base_ref, o_ref, ob_ref, *, tm, nh):
    del base_ref
    acc = _dot(x_ref[...], w_ref[...])
    ob_ref[...] = acc.astype(bf16)
    for h in range(nh):
        o_ref[0, pl.ds(h, tm, stride=nh), :] = acc[:, h * DH_A:(h + 1) * DH_A]


def _kv_proj(x, w, head_rows, tm_pref=512):
    m, k = x.shape
    n = w.shape[1]
    n_seq, row_off, nh, _ = head_rows.shape
    t_out = row_off + m // n_seq
    tm = _tile(m // n_seq, tm_pref)
    per_seq = m // n_seq // tm
    base = jnp.pad(head_rows.reshape(n_seq, row_off * nh, DH_A), ((0, 0), (0, (t_out - row_off) * nh), (0, 0)))
    o, ob = pl.pallas_call(
        functools.partial(_kv_proj_kernel, tm=tm, nh=nh),
        grid=(m // tm,),
        in_specs=[pl.BlockSpec((tm, k), lambda i: (i, 0)), pl.BlockSpec((k, n), lambda i: (0, 0)),
                  pl.BlockSpec(memory_space=pl.ANY)],
        out_specs=[pl.BlockSpec((pl.Element(1), pl.Element(tm * nh), pl.Element(DH_A)),
                                lambda i: (i // per_seq,
                                           pl.multiple_of((row_off + (i % per_seq) * tm) * nh, SUBLANE), 0)),
                   pl.BlockSpec((tm, n), lambda i: (i, 0))],
        out_shape=[jax.ShapeDtypeStruct((n_seq, t_out * nh, DH_A), f32), jax.ShapeDtypeStruct((m, n), bf16)],
        input_output_aliases={2: 0},
        compiler_params=_cparams(("parallel",)),
        name="kv_proj",
    )(x, w, base)
    return o.reshape(n_seq, t_out, nh, DH_A), ob


def _outproj_ln_kernel(*refs, n_act):
    acts = refs[:n_act]
    w_ref, res_ref, g_ref, b_ref, of_ref, ob_ref = refs[n_act:]
    kw = acts[0].shape[1]
    mix = _dot(acts[0][...], w_ref[0:kw, :])
    for a in range(1, n_act):
        mix = mix + _dot(acts[a][...], w_ref[a * kw:(a + 1) * kw, :])
    y = _layer_norm(ALPHA * res_ref[...] + mix, g_ref[...], b_ref[...])
    of_ref[...] = y
    ob_ref[...] = y.astype(bf16)


def _outproj_ln(acts, w, res, g, b, tm_pref=512):
    m, d = res.shape
    tm = _tile(m, tm_pref)
    n_act = len(acts)
    row = lambda width: pl.BlockSpec((tm, width), lambda i: (i, 0))
    once = lambda shape: pl.BlockSpec(shape, lambda i: (0, 0), pipeline_mode=pl.Buffered(1))
    return pl.pallas_call(
        functools.partial(_outproj_ln_kernel, n_act=n_act),
        grid=(m // tm,),
        in_specs=[row(a.shape[1]) for a in acts] + [once(w.shape), row(d), once((1, d)), once((1, d))],
        out_specs=[row(d), row(d)],
        out_shape=[jax.ShapeDtypeStruct((m, d), f32), jax.ShapeDtypeStruct((m, d), bf16)],
        compiler_params=_cparams(("parallel",)),
        name="outproj_ln",
    )(*acts, w, res, g.reshape(1, d), b.reshape(1, d))


def _ffn_ln_kernel(xb_ref, xf_ref, wg_ref, wu_ref, wd_ref, g_ref, b_ref, of_ref, ob_ref, acc_ref):
    f = pl.program_id(1)

    @pl.when(f == 0)
    def _():
        acc_ref[...] = jnp.zeros_like(acc_ref)

    x = xb_ref[...]
    gate = _dot(x, wg_ref[0])
    up = _dot(x, wu_ref[0])
    h = (gate * _sigmoid(gate) * up).astype(bf16)
    acc_ref[...] += _dot(h, wd_ref[0])

    @pl.when(f == pl.num_programs(1) - 1)
    def _():
        y = _layer_norm(ALPHA * xf_ref[...] + acc_ref[...], g_ref[...], b_ref[...])
        of_ref[...] = y
        ob_ref[...] = y.astype(bf16)


def _ffn_ln(xb, xf, wg, wu, wd, layer, g, b, tm_pref=512, tf_pref=512):
    m, d = xf.shape
    dff = wg.shape[2]
    tm = _tile(m, tm_pref)
    tf = _tile(dff, tf_pref, LANE)
    return pl.pallas_call(
        _ffn_ln_kernel,
        grid=(m // tm, dff // tf),
        in_specs=[pl.BlockSpec((tm, d), lambda i, f: (i, 0)),
                  pl.BlockSpec((tm, d), lambda i, f: (i, 0)),
                  pl.BlockSpec((1, d, tf), lambda i, f: (layer, 0, f)),
                  pl.BlockSpec((1, d, tf), lambda i, f: (layer, 0, f)),
                  pl.BlockSpec((1, tf, d), lambda i, f: (layer, f, 0)),
                  pl.BlockSpec((1, d), lambda i, f: (0, 0)),
                  pl.BlockSpec((1, d), lambda i, f: (0, 0))],
        out_specs=[pl.BlockSpec((tm, d), lambda i, f: (i, 0)),
                   pl.BlockSpec((tm, d), lambda i, f: (i, 0))],
        out_shape=[jax.ShapeDtypeStruct((m, d), f32), jax.ShapeDtypeStruct((m, d), bf16)],
        scratch_shapes=[pltpu.VMEM((tm, d), f32)],
        compiler_params=_cparams(("parallel", "arbitrary")),
        name="ffn_ln",
    )(xb, xf, wg, wu, wd, g.reshape(1, d), b.reshape(1, d))


FORGET_PIECES = 3


def _cumsum_kernel(x_ref, bias_ref, carry_ref, *refs, gate):
    acc_ref = refs[-1]
    c = pl.program_id(1)

    @pl.when(c == 0)
    def _():
        acc_ref[...] = carry_ref[0]

    x = x_ref[0]
    lf = _log_sigmoid(x + bias_ref[...]) if gate else x
    tc = x.shape[0]
    cum = _dot_exact_rhs(_tri_lower(tc), lf) + acc_ref[...]
    acc_ref[...] = cum[tc - 1:tc, :]
    if not gate:
        refs[0][0] = cum
        return
    logf_ref, cum_ref, cols_ref = refs[:3]
    logf_ref[0] = lf
    cum_ref[0] = cum
    r, col = _iota2((LANE, LANE), 0), _iota2((LANE, LANE), 1)
    out = None
    for i, piece in enumerate(_split3(cum * LOG2E)):
        t = _dot(piece, ((col == FORGET_PIECES * r + i) & (r < H_A)).astype(bf16))
        out = t if out is None else out + t
    cols_ref[0] = out.astype(bf16)


def _logf_cumsum(x, bias, carry, gate, tc_pref=512):
    bsz, t, w = x.shape
    tc = _tile(t, tc_pref)
    spec = pl.BlockSpec((1, tc, w), lambda b, c: (b, c, 0))
    out_dtypes = [f32, f32, bf16] if gate else [f32]
    return pl.pallas_call(
        functools.partial(_cumsum_kernel, gate=gate),
        grid=(bsz, t // tc),
        in_specs=[spec,
                  pl.BlockSpec((1, w), lambda b, c: (0, 0)),
                  pl.BlockSpec((1, 1, w), lambda b, c: (b, 0, 0))],
        out_specs=[spec for _ in out_dtypes],
        out_shape=[jax.ShapeDtypeStruct((bsz, t, w), dt) for dt in out_dtypes],
        scratch_shapes=[pltpu.VMEM((1, w), f32)],
        compiler_params=_cparams(("parallel", "arbitrary")),
        name="logf_cumsum",
    )(x, bias, carry)


def _attn_self_kernel(q_ref, k_ref, v_ref, f_ref, o_ref, *, nh):
    d = DH_A
    tq, t2 = q_ref.shape[1], k_ref.shape[1]
    causal = _iota2((tq, t2), 0) >= _iota2((tq, t2), 1)
    for h in range(nh):
        cols = slice(h * d, (h + 1) * d)
        s = _dot(q_ref[0, :, cols], k_ref[0, :, cols], _NT) - f_ref[0, h] * LOG2E
        s = jnp.where(causal, s, NEG)
        p = jnp.exp2(s - jnp.max(s, axis=-1, keepdims=True))
        acc = _dot(p.astype(bf16), v_ref[0, :, cols])
        o_ref[0, :, cols] = (acc / jnp.sum(p, axis=-1, keepdims=True)).astype(o_ref.dtype)


def _fox_attention_self(q, k, v, f):
    bsz, tq, hd = q.shape
    t2 = k.shape[1]
    nh = hd // DH_A
    qspec = pl.BlockSpec((1, tq, hd), lambda b: (b, 0, 0))
    kvspec = pl.BlockSpec((1, t2, hd), lambda b: (b, 0, 0))
    return pl.pallas_call(
        functools.partial(_attn_self_kernel, nh=nh),
        grid=(bsz,),
        in_specs=[qspec, kvspec, kvspec, pl.BlockSpec((1, nh, 1, t2), lambda b: (b, 0, 0, 0))],
        out_specs=qspec,
        out_shape=jax.ShapeDtypeStruct((bsz, tq, hd), bf16),
        compiler_params=_cparams(("parallel",)),
        name="fox_attention_self",
    )(q, k, v, f)


def _attn_cache_kernel(q_ref, k1_ref, v1_ref, f1_ref, k2_ref, v2_ref, f2_ref, o_ref, m_ref, l_ref, acc_ref,
                       *, tk, nh):
    j = pl.program_id(1)
    d = DH_A
    tq = q_ref.shape[1]

    @pl.when(j == 0)
    def _():
        m_ref[...] = jnp.full(m_ref.shape, NEG, f32)
        l_ref[...] = jnp.zeros(l_ref.shape, f32)
        acc_ref[...] = jnp.zeros(acc_ref.shape, f32)

    def update(h, s, v, fk, mask):
        s = s - fk * LOG2E
        if mask is not None:
            s = jnp.where(mask, s, NEG)
        m = m_ref[h][:, 0:1]
        m_new = jnp.maximum(m, jnp.max(s, axis=-1, keepdims=True))
        alpha = jnp.exp2(m - m_new)
        p = jnp.exp2(s - m_new)
        l_ref[h] = jnp.broadcast_to(alpha * l_ref[h][:, 0:1] + jnp.sum(p, axis=-1, keepdims=True), (tq, LANE))
        acc_ref[h] = alpha * acc_ref[h] + _dot(p.astype(bf16), v)
        m_ref[h] = jnp.broadcast_to(m_new, (tq, LANE))

    qs = [q_ref[0, :, h * d:(h + 1) * d] for h in range(nh)]
    ss = [_dot(qs[h], k1_ref[0, pl.ds(h, tk, stride=nh), :].astype(bf16), _NT) for h in range(nh)]
    for h in range(nh):
        update(h, ss[h], v1_ref[0, pl.ds(h, tk, stride=nh), :].astype(bf16), f1_ref[0, h], None)

    @pl.when(j == pl.num_programs(1) - 1)
    def _():
        t2 = k2_ref.shape[1]
        causal = _iota2((tq, t2), 0) >= _iota2((tq, t2), 1)
        for h in range(nh):
            cols = slice(h * d, (h + 1) * d)
            update(h, _dot(qs[h], k2_ref[0, :, cols], _NT), v2_ref[0, :, cols], f2_ref[0, h], causal)
            o_ref[0, :, cols] = (acc_ref[h] / l_ref[h][:, 0:1]).astype(o_ref.dtype)


def _fox_attention_cache(q, k1, v1, f1, k2, v2, f2, tk=512):
    bsz, tq, hd = q.shape
    s, nh = k1.shape[1], k1.shape[2]
    t2 = k2.shape[1]
    flat = lambda a: a.reshape(bsz, s * nh, DH_A)
    qspec = pl.BlockSpec((1, tq, hd), lambda b, j: (b, 0, 0))
    cspec = pl.BlockSpec((1, tk * nh, DH_A), lambda b, j: (b, j, 0))
    nspec = pl.BlockSpec((1, t2, hd), lambda b, j: (b, 0, 0))
    return pl.pallas_call(
        functools.partial(_attn_cache_kernel, tk=tk, nh=nh),
        grid=(bsz, s // tk),
        in_specs=[qspec, cspec, cspec, pl.BlockSpec((1, nh, 1, tk), lambda b, j: (b, 0, 0, j)),
                  nspec, nspec, pl.BlockSpec((1, nh, 1, t2), lambda b, j: (b, 0, 0, 0))],
        out_specs=qspec,
        out_shape=jax.ShapeDtypeStruct((bsz, tq, hd), bf16),
        scratch_shapes=[pltpu.VMEM((nh, tq, LANE), f32), pltpu.VMEM((nh, tq, LANE), f32),
                        pltpu.VMEM((nh, tq, DH_A), f32)],
        compiler_params=_cparams(("parallel", "arbitrary")),
        name="fox_attention_cache",
    )(q, flat(k1), flat(v1), f1, k2, v2, f2)


def _attn_t_kernel(q_ref, k1_ref, v1_ref, c1_ref, k2_ref, v2_ref, c2_ref, o_ref, *, tq, s1_valid, hps):
    hg = pl.program_id(1)
    qi = pl.program_id(2)
    d = DH_A

    lane = _iota2((tq, LANE), 1)
    q_aug = []
    for hh in range(hps):
        lo = FORGET_PIECES * (hg * hps + hh)
        pick = jnp.where((lane >= lo) & (lane < lo + FORGET_PIECES), -1.0, 0.0).astype(bf16)
        q_aug.append(jnp.concatenate([q_ref[0, :, hh * d:(hh + 1) * d], pick], axis=-1))

    def update(carry, s, v, mask):
        m, l, acc = carry
        if mask is not None:
            s = jnp.where(mask, s, NEG)
        m_new = jnp.maximum(m, jnp.max(s, axis=0, keepdims=True))
        alpha = jnp.exp2(m - m_new)
        p = jnp.exp2(s - m_new)
        l = alpha * l + jnp.sum(p, axis=0, keepdims=True)
        acc = alpha * acc + _dot(v, p.astype(bf16), _TN)
        return m_new, l, acc

    def multi(carries, k_augs, vs, mask):
        ss = [_dot(k_augs[hh], q_aug[hh], _NT) for hh in range(hps)]
        return [update(carries[hh], ss[hh], vs[hh], mask) for hh in range(hps)]

    carries = [(jnp.full((1, tq), NEG, f32), jnp.zeros((1, tq), f32), jnp.zeros((d, tq), f32))
               for _ in range(hps)]
    k1_aug = [jnp.concatenate([k1_ref[0, 0:s1_valid, hh * d:(hh + 1) * d], c1_ref[0, 0:s1_valid, :]], axis=-1)
              for hh in range(hps)]
    carries = multi(carries, k1_aug, [v1_ref[0, 0:s1_valid, hh * d:(hh + 1) * d] for hh in range(hps)], None)

    def blk(j, c, mask):
        off = pl.multiple_of(j * tq, tq)
        cols = c2_ref[0, pl.ds(off, tq), :]
        return multi(c, [jnp.concatenate([k2_ref[0, pl.ds(off, tq), hh * d:(hh + 1) * d], cols], axis=-1)
                         for hh in range(hps)],
                     [v2_ref[0, pl.ds(off, tq), hh * d:(hh + 1) * d] for hh in range(hps)], mask)

    carries = lax.fori_loop(0, qi, lambda j, c: blk(j, c, None), carries)
    carries = blk(qi, carries, _iota2((tq, tq), 0) <= _iota2((tq, tq), 1))
    for hh in range(hps):
        m, l, acc = carries[hh]
        o_ref[0, :, hh * d:(hh + 1) * d] = jnp.transpose(acc / l).astype(o_ref.dtype)


def _fox_attention_main(q, k1, v1, c1, k2, v2, c2, s1_valid, tq=512, hps=8):
    bsz, t, hd = q.shape
    s1 = k1.shape[1]
    w = hps * DH_A
    qspec = pl.BlockSpec((1, tq, w), lambda b, h, i: (b, i, h))
    kvspec = pl.BlockSpec((1, t, w), lambda b, h, i: (b, 0, h))
    pspec = pl.BlockSpec((1, s1, w), lambda b, h, i: (b, 0, h))
    return pl.pallas_call(
        functools.partial(_attn_t_kernel, tq=tq, s1_valid=s1_valid, hps=hps),
        grid=(bsz, hd // w, t // tq),
        in_specs=[qspec, pspec, pspec, pl.BlockSpec((1, s1, LANE), lambda b, h, i: (b, 0, 0)),
                  kvspec, kvspec, pl.BlockSpec((1, t, LANE), lambda b, h, i: (b, 0, 0))],
        out_specs=qspec,
        out_shape=jax.ShapeDtypeStruct((bsz, t, hd), bf16),
        compiler_params=_cparams(("parallel", "parallel", "arbitrary")),
        name="fox_attention_main",
    )(q, k1, v1, c1, k2, v2, c2)


def _ssd_kernel(zx_ref, dt_ref, hist_ref, h0_ref, cw_ref, cb_ref, dtb_ref, alog_ref, alogc_ref,
                dskip_ref, ng_ref, e_ref, y_ref, tail_ref, hout_ref, ext_ref, st_ref, *, lc, t_valid):
    c = pl.program_id(1)

    @pl.when(c == 0)
    def _():
        ext_ref[0:SUBLANE, :] = hist_ref[0]
        st_ref[...] = h0_ref[0]

    x_raw = zx_ref[0, :, D_SSM:]
    ext_ref[SUBLANE:SUBLANE + lc, :] = x_raw
    hist = ext_ref[0:SUBLANE, :]
    row8 = _iota2((SUBLANE, CONV_DIM), 0)
    conv = cb_ref[...] + cw_ref[CONV_W - 1:CONV_W, :] * x_raw
    for s in range(1, CONV_W):
        rolled = pltpu.roll(x_raw, s, axis=0)
        first = jnp.where(row8 < s, pltpu.roll(hist, s, axis=0), rolled[0:SUBLANE])
        conv = conv + cw_ref[CONV_W - 1 - s:CONV_W - s, :] * jnp.concatenate([first, rolled[SUBLANE:]], axis=0)
    xc = conv * _sigmoid(conv)
    tail = ext_ref[t_valid:t_valid + SUBLANE, :]
    ext_ref[0:SUBLANE, :] = tail

    dt = _softplus(dt_ref[0] + dtb_ref[...])
    if t_valid < lc:
        dt = jnp.where(_iota2((lc, LANE), 0) < t_valid, dt, 0.0)
    a_row = -jnp.exp(alog_ref[...])
    a_col = -jnp.exp(alogc_ref[...])
    a_cum = _dot_exact_rhs(_tri_lower(lc), dt * a_row)
    dt_t = _dot_exact_rhs(_eye(H_B, LANE), dt, _NT)
    a_cum_t = _dot_exact_lhs(dt_t * a_col, _tri_upper(lc))
    a_last = a_cum[lc - 1:lc, :]
    e = e_ref[...]
    to_end_x = _dot_exact_lhs(jnp.exp(a_last - a_cum) * dt, e)
    ea_x = _dot_exact_lhs(jnp.exp(a_cum), e)
    cdec_x = _dot_exact_lhs(jnp.broadcast_to(jnp.exp(a_last), (SUBLANE, LANE)), e)[0:1, :]

    causal = _iota2((lc, lc), 0) >= _iota2((lc, lc), 1)
    left = _iota2((lc, LANE), 1) < P_B
    eye_n = _eye(N_B, N_B)
    gw = D_SSM // G_B
    groups = range(G_B)
    gcols = [slice(g * gw, (g + 1) * gw) for g in groups]
    bg = [xc[:, D_SSM + g * N_B:D_SSM + (g + 1) * N_B].astype(bf16) for g in groups]
    cg = [xc[:, D_SSM + G_B * N_B + g * N_B:D_SSM + G_B * N_B + (g + 1) * N_B].astype(bf16) for g in groups]
    cb = [_dot(cg[g], bg[g], _NT) for g in groups]
    bg_t = [_dot(eye_n, bg[g], _NT).astype(bf16) for g in groups]
    st_in = [st_ref[:, gcols[g]] for g in groups]
    y_off = [_dot(cg[g], st_in[g].astype(bf16)) * ea_x[:, gcols[g]] for g in groups]
    upd = [_dot(bg_t[g], (xc[:, gcols[g]] * to_end_x[:, gcols[g]]).astype(bf16)) for g in groups]
    for g in groups:
        st_ref[:, gcols[g]] = cdec_x[:, gcols[g]] * st_in[g] + upd[g]

    def decay_weights(h):
        seg = a_cum[:, h:h + 1] - a_cum_t[h:h + 1, :]
        dec = jnp.exp(jnp.where(causal, seg, NEG))
        return (cb[h // HG_B] * dec * dt_t[h:h + 1, :]).astype(bf16)

    w_all = [decay_weights(h) for h in range(H_B)]
    y_pair = []
    for j in range(D_SSM // LANE):
        xp = xc[:, j * LANE:(j + 1) * LANE].astype(bf16)
        y_pair.append(jnp.where(left, _dot(w_all[2 * j], xp), _dot(w_all[2 * j + 1], xp)))
    ppg = gw // LANE
    for g in groups:
        xg = xc[:, gcols[g]]
        y = jnp.concatenate(y_pair[g * ppg:(g + 1) * ppg], axis=-1) + y_off[g] + dskip_ref[:, gcols[g]] * xg
        z = zx_ref[0, :, g * gw:(g + 1) * gw]
        y = y * (z * _sigmoid(z))
        y = y * lax.rsqrt(jnp.mean(y * y, axis=-1, keepdims=True) + LN_EPS) * ng_ref[:, g * gw:(g + 1) * gw]
        y_ref[0, :, g * gw:(g + 1) * gw] = y.astype(y_ref.dtype)

    @pl.when(c == pl.num_programs(1) - 1)
    def _():
        tail_ref[0] = tail
        hout_ref[0] = st_ref[...]


def _ssd_mixer(zx, dt_raw, hist, h0_t, p, lc, t_valid):
    bsz, t, wz = zx.shape
    const = lambda shape: pl.BlockSpec(shape, lambda b, c: (0,) * len(shape))
    return pl.pallas_call(
        functools.partial(_ssd_kernel, lc=lc, t_valid=t_valid),
        grid=(bsz, t // lc),
        in_specs=[pl.BlockSpec((1, lc, wz), lambda b, c: (b, c, 0)),
                  pl.BlockSpec((1, lc, LANE), lambda b, c: (b, c, 0)),
                  pl.BlockSpec((1, SUBLANE, CONV_DIM), lambda b, c: (b, 0, 0)),
                  pl.BlockSpec((1, N_B, D_SSM), lambda b, c: (b, 0, 0)),
                  const((SUBLANE, CONV_DIM)), const((1, CONV_DIM)), const((1, LANE)), const((1, LANE)),
                  const((H_B, 1)), const((1, D_SSM)), const((1, D_SSM)), const((LANE, D_SSM))],
        out_specs=[pl.BlockSpec((1, lc, D_SSM), lambda b, c: (b, c, 0)),
                   pl.BlockSpec((1, SUBLANE, CONV_DIM), lambda b, c: (b, 0, 0)),
                   pl.BlockSpec((1, N_B, D_SSM), lambda b, c: (b, 0, 0))],
        out_shape=[jax.ShapeDtypeStruct((bsz, t, D_SSM), bf16),
                   jax.ShapeDtypeStruct((bsz, SUBLANE, CONV_DIM), f32),
                   jax.ShapeDtypeStruct((bsz, N_B, D_SSM), f32)],
        scratch_shapes=[pltpu.VMEM((SUBLANE + lc, CONV_DIM), f32), pltpu.VMEM((N_B, D_SSM), f32)],
        compiler_params=_cparams(("parallel", "arbitrary")),
        name="ssd_mixer",
    )(zx, dt_raw, hist, h0_t, p["conv_w"], p["conv_b"], p["dt_bias"], p["a_log_row"], p["a_log_col"],
      p["d_skip"], p["norm_g"], p["expand"])


def _mlstm_kernel(qkv_ref, o_ref, gates_ref, gb_ref, ng_ref, c0_ref, n0_ref, m0_ref,
                  y_ref, cout_ref, nout_ref, mout_ref, ct_ref, n_ref, m_ref, *, lc, t_valid):
    c = pl.program_id(1)

    @pl.when(c == 0)
    def _():
        ct_ref[...] = c0_ref[0]
        n_ref[...] = n0_ref[0]
        m_ref[...] = m0_ref[0]

    lane = _iota2((lc, LANE), 1)
    g = gates_ref[0] + gb_ref[...]
    is_f = (lane >= H_C) & (lane < 2 * H_C)
    lf = jnp.where(is_f, _log_sigmoid(g), 0.0)
    li = g
    if t_valid < lc:
        valid = _iota2((lc, LANE), 0) < t_valid
        lf = jnp.where(valid, lf, 0.0)
        li = jnp.where(valid, li, NEG)
    b_cum = _dot_exact_rhs(_tri_lower(lc), lf)
    rows = 2 * H_C
    li_t = _dot_exact_rhs(_eye(rows, LANE), li, _NT)
    lf_t = _dot_exact_rhs(_eye(rows, LANE), lf, _NT)
    b_cum_t = _dot_exact_lhs(lf_t, _tri_upper(lc))

    causal = _iota2((lc, lc), 0) >= _iota2((lc, lc), 1)
    eye_k = _eye(DK_C, DK_C)
    m_all = m_ref[...]
    m_out = m_all
    q_of = lambda h: qkv_ref[0, :, h * DK_C:(h + 1) * DK_C]
    k_of = lambda h: qkv_ref[0, :, D_QK_C + h * DK_C:D_QK_C + (h + 1) * DK_C]
    v_of = lambda h: qkv_ref[0, :, 2 * D_QK_C + h * DV_C:2 * D_QK_C + (h + 1) * DV_C]
    heads = range(H_C)
    qk_all = [_dot(q_of(h), k_of(h), _NT) for h in heads]
    qc_all = [_dot(q_of(h), ct_ref[h].astype(bf16)) for h in heads]
    kt_all = [_dot(eye_k, k_of(h), _NT) for h in heads]
    bcol = [b_cum[:, H_C + h:H_C + h + 1] for h in heads]
    brow = [b_cum_t[H_C + h:H_C + h + 1, :] for h in heads]
    lirow = [li_t[h:h + 1, :] for h in heads]
    m_st = [m_all[:, h:h + 1] for h in heads]
    d_mat = [jnp.where(causal, bcol[h] - brow[h] + lirow[h], NEG) for h in heads]
    inter = [bcol[h] + m_st[h] for h in heads]
    m_row = [jnp.maximum(inter[h], jnp.max(d_mat[h], axis=-1, keepdims=True)) for h in heads]
    w = [jnp.exp(d_mat[h] - m_row[h]) * qk_all[h] for h in heads]
    g_inter = [jnp.exp(inter[h] - m_row[h]) for h in heads]
    n_st = [n_ref[h:h + 1, :] for h in heads]
    num = [_dot(w[h].astype(bf16), v_of(h)) + g_inter[h] * qc_all[h] for h in heads]
    den = [jnp.sum(w[h], axis=-1, keepdims=True)
           + g_inter[h] * jnp.sum(q_of(h).astype(f32) * n_st[h], axis=-1, keepdims=True) for h in heads]
    hh = [num[h] / jnp.maximum(jnp.abs(den[h]), jnp.exp(-m_row[h])) for h in heads]
    ms = [jnp.mean(hh[h] * hh[h], axis=-1, keepdims=True) for h in heads]
    for h in heads:
        cols = slice(h * DV_C, (h + 1) * DV_C)
        y = hh[h] * lax.rsqrt(ms[h] + LN_EPS) * ng_ref[:, cols] * _sigmoid(o_ref[0, :, cols])
        y_ref[0, :, cols] = y.astype(y_ref.dtype)
    b_tot = [bcol[h][lc - 1:lc, :] for h in heads]
    d_end = [b_tot[h] - brow[h] + lirow[h] for h in heads]
    m_new = [jnp.maximum(b_tot[h] + m_st[h], jnp.max(d_end[h], axis=-1, keepdims=True)) for h in heads]
    w_end = [jnp.exp(d_end[h] - m_new[h]) for h in heads]
    g_old = [jnp.exp(b_tot[h] + m_st[h] - m_new[h]) for h in heads]
    for h in heads:
        ct_ref[h] = g_old[h] * ct_ref[h] + _dot((kt_all[h] * w_end[h]).astype(bf16), v_of(h))
        n_upd = _dot(jnp.broadcast_to(w_end[h], (SUBLANE, lc)).astype(bf16), k_of(h))[0:1, :]
        n_ref[h:h + 1, :] = g_old[h] * n_st[h] + n_upd
        m_out = jnp.where(lane[0:1, :] == h, m_new[h], m_out)
    m_ref[...] = m_out

    @pl.when(c == pl.num_programs(1) - 1)
    def _():
        cout_ref[0] = ct_ref[...]
        nout_ref[0] = n_ref[...]
        mout_ref[0] = m_ref[...]


def _mlstm_mixer(qkv, o, gates, c0_t, n0, m0, p, lc, t_valid):
    bsz, t, wq = qkv.shape
    const = lambda shape: pl.BlockSpec(shape, lambda b, c: (0,) * len(shape))
    return pl.pallas_call(
        functools.partial(_mlstm_kernel, lc=lc, t_valid=t_valid),
        grid=(bsz, t // lc),
        in_specs=[pl.BlockSpec((1, lc, wq), lambda b, c: (b, c, 0)),
                  pl.BlockSpec((1, lc, D_V_C), lambda b, c: (b, c, 0)),
                  pl.BlockSpec((1, lc, LANE), lambda b, c: (b, c, 0)),
                  const((1, LANE)), const((1, D_V_C)),
                  pl.BlockSpec((1, H_C, DK_C, DV_C), lambda b, c: (b, 0, 0, 0)),
                  pl.BlockSpec((1, H_C, DK_C), lambda b, c: (b, 0, 0)),
                  pl.BlockSpec((1, 1, LANE), lambda b, c: (b, 0, 0))],
        out_specs=[pl.BlockSpec((1, lc, D_V_C), lambda b, c: (b, c, 0)),
                   pl.BlockSpec((1, H_C, DK_C, DV_C), lambda b, c: (b, 0, 0, 0)),
                   pl.BlockSpec((1, H_C, DK_C), lambda b, c: (b, 0, 0)),
                   pl.BlockSpec((1, 1, LANE), lambda b, c: (b, 0, 0))],
        out_shape=[jax.ShapeDtypeStruct((bsz, t, D_V_C), bf16),
                   jax.ShapeDtypeStruct((bsz, H_C, DK_C, DV_C), f32),
                   jax.ShapeDtypeStruct((bsz, H_C, DK_C), f32),
                   jax.ShapeDtypeStruct((bsz, 1, LANE), f32)],
        scratch_shapes=[pltpu.VMEM((H_C, DK_C, DV_C), f32), pltpu.VMEM((H_C, DK_C), f32),
                        pltpu.VMEM((1, LANE), f32)],
        compiler_params=_cparams(("parallel", "arbitrary")),
        name="mlstm_mixer",
    )(qkv, o, gates, p["gate_bias"], p["norm_g"], c0_t, n0, m0)


def _pad_lanes(a, width=LANE):
    return jnp.pad(a, [(0, 0)] * (a.ndim - 1) + [(0, width - a.shape[-1])])


def _pad_seq(a, t_pad=SEQ_PAD):
    return jnp.pad(a, [(0, 0), (0, t_pad - a.shape[1])] + [(0, 0)] * (a.ndim - 2))


def _even_layer(xm_b, xm_f, xa_b, xa_f, n_p, t_main, cache, w):
    k_past, v_past, logf_past, conv_past, h_past = cache
    n_s = k_past.shape[0]
    n_aux = n_p + n_s
    t_aux = xa_f.shape[0] // n_aux
    past = k_past.shape[1]

    seq = lambda a, n: a.reshape(n, a.shape[0] // n, a.shape[1])
    heads = lambda a: a.reshape(a.shape[0], a.shape[1], H_A, DH_A)

    def project(xb, meta_kv):
        q, = _matmul(xb, w["w_q"], [bf16], col_scale=jnp.full((1, D_A), DH_A ** -0.5 * LOG2E, f32))
        if meta_kv is None:
            k_f, k_b = _matmul(xb, w["w_k"], [f32, bf16])
            v_f, v_b = _matmul(xb, w["w_v"], [f32, bf16])
        else:
            k_f, k_b = _kv_proj(xb, w["w_k"], meta_kv[0])
            v_f, v_b = _kv_proj(xb, w["w_v"], meta_kv[1])
        zx, = _matmul(xb, w["w_zx"], [f32])
        sm, = _matmul(xb, w["w_small"], [f32], tn_pref=2 * LANE)
        return q, k_f, k_b, v_f, v_b, zx, sm

    qa, ka_f, ka_b, va_f, va_b, zxa, sma = project(xa_b, None)
    qa, ka_f, va_f = [seq(a, n_aux) for a in (qa, ka_f, va_f)]
    ka_b, va_b, zxa, sma = [_pad_seq(seq(a, n_aux)) for a in (ka_b, va_b, zxa, sma)]
    ka_f, va_f = heads(ka_f), heads(va_f)
    qm, kp_f, km_b, vp_f, vm_b, zxm, smm = project(xm_b, (ka_f[:n_p], va_f[:n_p]))
    qm, km_b, vm_b, zxm, smm = [seq(a, n_p) for a in (qm, km_b, vm_b, zxm, smm)]

    zero_c = jnp.zeros((n_s, 1, LANE), f32)
    cum_past, = _logf_cumsum(_pad_lanes(logf_past.astype(f32)), w["b_f"], zero_c, gate=False)
    carry_a = jnp.concatenate([jnp.zeros((n_p, 1, LANE), f32), cum_past[:, past - 1:past]], axis=0)
    logf_a, cum_a, cols_a = _logf_cumsum(sma[:, :, LANE:], w["b_f"], carry_a, gate=True)
    logf_m, _, cols_m = _logf_cumsum(smm[:, :, LANE:], w["b_f"], cum_a[:n_p, t_aux - 1:t_aux], gate=True)
    rows = lambda cum: jnp.swapaxes(cum[:, :, :H_A], 1, 2)[:, :, None, :]

    fa = rows(cum_a)
    att_meta = _fox_attention_self(qa[:n_p], ka_b[:n_p], va_b[:n_p], fa[:n_p])
    att_s = _fox_attention_cache(qa[n_p:], k_past, v_past, rows(cum_past), ka_b[n_p:], va_b[n_p:], fa[n_p:],
                                 tk=min(past, CACHE_BLOCK))
    att_m = _fox_attention_main(qm, ka_b[:n_p], va_b[:n_p], cols_a[:n_p], km_b, vm_b, cols_m, t_aux,
                                tq=min(t_main, ATTN_BLOCK))
    att_a = jnp.concatenate([att_meta, att_s], axis=0)

    hist_a = jnp.concatenate([jnp.zeros((n_p, SUBLANE, CONV_DIM), f32),
                              jnp.pad(conv_past.astype(f32), ((0, 0), (SUBLANE - (CONV_W - 1), 0), (0, 0)))], axis=0)
    to_t = lambda h: jnp.transpose(h.reshape(-1, D_SSM, N_B), (0, 2, 1))
    from_t = lambda h: jnp.transpose(h, (0, 2, 1)).reshape(-1, H_B, P_B, N_B)
    h0_a = jnp.concatenate([jnp.zeros((n_p, N_B, D_SSM), f32), to_t(h_past.astype(f32))], axis=0)
    ya, tail_a, h_a = _ssd_mixer(zxa, sma[:, :, :LANE], hist_a, h0_a, w, SEQ_PAD, t_aux)
    ym, tail_m, h_m = _ssd_mixer(zxm, smm[:, :, :LANE], tail_a[:n_p], h_a[:n_p], w, min(t_main, SSD_CHUNK), min(t_main, SSD_CHUNK))

    unseq = lambda a: a.reshape(-1, a.shape[-1])
    x1m_f, x1m_b = _outproj_ln([unseq(att_m), unseq(ym)], w["w_out"], xm_f, w["ln_g"], w["ln_b"])
    x1a_f, x1a_b = _outproj_ln([unseq(att_a), unseq(ya[:, :t_aux])], w["w_out"], xa_f, w["ln_g"], w["ln_b"])

    new = dict(
        k_p=kp_f, v_p=vp_f,
        f_p=jnp.concatenate([logf_a[:n_p, :t_aux, :H_A], logf_m[:, :, :H_A]], axis=1),
        conv_p=tail_m[:, SUBLANE - (CONV_W - 1):], h_p=from_t(h_m),
        k_s=ka_f[n_p:], v_s=va_f[n_p:], f_s=logf_a[n_p:, :t_aux, :H_A],
        conv_s=tail_a[n_p:, SUBLANE - (CONV_W - 1):], h_s=from_t(h_a[n_p:]))
    return (x1m_f, x1m_b, x1a_f, x1a_b), new


def _odd_layer(xm_b, xm_f, xa_b, xa_f, n_p, t_main, cache, w):
    c_past, n_past, m_past = cache
    n_s = c_past.shape[0]
    n_aux = n_p + n_s
    t_aux = xa_f.shape[0] // n_aux

    def project(xb):
        qkv, = _matmul(xb, w["w_qkv"], [bf16], col_scale=w["qkv_scale"])
        o, = _matmul(xb, w["w_o"], [f32])
        gt, = _matmul(xb, w["w_gates"], [f32])
        return qkv, o, gt

    qkvm, om, gm = project(xm_b)
    qkva, oa, ga = project(xa_b)
    seq = lambda a, n: a.reshape(n, a.shape[0] // n, a.shape[1])
    qkvm, om, gm = [seq(a, n_p) for a in (qkvm, om, gm)]
    qkva, oa, ga = [_pad_seq(seq(a, n_aux)) for a in (qkva, oa, ga)]

    c0 = jnp.concatenate([jnp.zeros((n_p, H_C, DK_C, DV_C), f32), jnp.swapaxes(c_past.astype(f32), 2, 3)], axis=0)
    n0 = jnp.concatenate([jnp.zeros((n_p, H_C, DK_C), f32), n_past.astype(f32)], axis=0)
    m0 = jnp.concatenate([jnp.zeros((n_p, 1, LANE), f32), _pad_lanes(m_past.astype(f32))[:, None, :]], axis=0)
    ha, c_a, n_a, m_a = _mlstm_mixer(qkva, oa, ga, c0, n0, m0, w, SEQ_PAD, t_aux)
    lc = min(t_main, MLSTM_CHUNK)
    hm, c_m, n_m, m_m = _mlstm_mixer(qkvm, om, gm, c_a[:n_p], n_a[:n_p], m_a[:n_p], w, lc, lc)

    unseq = lambda a: a.reshape(-1, a.shape[-1])
    x1m_f, x1m_b = _outproj_ln([unseq(hm)], w["w_out"], xm_f, w["ln_g"], w["ln_b"])
    x1a_f, x1a_b = _outproj_ln([unseq(ha[:, :t_aux])], w["w_out"], xa_f, w["ln_g"], w["ln_b"])
    new = dict(c_p=jnp.swapaxes(c_m, 2, 3), n_p=n_m, m_p=m_m[:, 0, :H_C],
               c_s=jnp.swapaxes(c_a[n_p:], 2, 3), n_s=n_a[n_p:], m_s=m_a[n_p:, 0, :H_C])
    return (x1m_f, x1m_b, x1a_f, x1a_b), new


def _even_weights(e, w_in_a, b_fgate_a, conv_w, conv_b, dt_bias, a_log, d_skip, ssd_norm_g, w_out_a):
    wi = w_in_a[e]
    o_f = 3 * D_A
    o_z = o_f + H_A
    o_dt = o_z + D_SSM + CONV_DIM
    w_small = jnp.concatenate([_pad_lanes(wi[:, o_dt:o_dt + H_B]), _pad_lanes(wi[:, o_f:o_f + H_A])], axis=1)
    expand = (np.arange(LANE)[:, None] == (np.arange(D_SSM) // P_B)[None, :])
    return dict(
        w_q=wi[:, :D_A].astype(bf16), w_k=wi[:, D_A:2 * D_A].astype(bf16),
        w_v=wi[:, 2 * D_A:3 * D_A].astype(bf16),
        w_zx=wi[:, o_z:o_dt].astype(bf16), w_small=w_small.astype(bf16),
        b_f=_pad_lanes(b_fgate_a[e][None, :]),
        conv_w=jnp.pad(conv_w[e], ((0, SUBLANE - CONV_W), (0, 0))), conv_b=conv_b[e][None, :],
        dt_bias=_pad_lanes(dt_bias[e][None, :]), a_log_row=_pad_lanes(a_log[e][None, :]),
        a_log_col=a_log[e][:, None], d_skip=jnp.repeat(d_skip[e], P_B)[None, :],
        norm_g=ssd_norm_g[e][None, :], expand=jnp.asarray(expand, bf16),
        w_out=w_out_a[e].astype(bf16))


def _odd_weights(o, w_in_c, b_igate_c, b_fgate_c, mlstm_norm_g, w_out_c):
    wi = w_in_c[o]
    o_o = 2 * D_QK_C + D_V_C
    o_g = o_o + D_V_C
    scale = jnp.concatenate([jnp.ones((D_QK_C,), f32), jnp.full((D_QK_C,), DK_C ** -0.5, f32),
                             jnp.ones((D_V_C,), f32)])[None, :]
    return dict(
        w_qkv=wi[:, :o_o].astype(bf16), qkv_scale=scale, w_o=wi[:, o_o:o_g].astype(bf16),
        w_gates=_pad_lanes(wi[:, o_g:]).astype(bf16),
        gate_bias=_pad_lanes(jnp.concatenate([b_igate_c[o], b_fgate_c[o]])[None, :]),
        norm_g=mlstm_norm_g[o][None, :], w_out=w_out_c[o].astype(bf16))


def kernel(x_prompt, x_sample, cache_fox_k, cache_fox_v, cache_fox_logf, state_ssd_conv, state_ssd, state_mlstm_c, state_mlstm_n, state_mlstm_m, meta_tokens, w_in_a, b_fgate_a, conv_w, conv_b, dt_bias, a_log, d_skip, ssd_norm_g, w_out_a, w_in_c, b_igate_c, b_fgate_c, mlstm_norm_g, w_out_c, ln_mix_g, ln_mix_b, ln_ffn_g, ln_ffn_b, w_ffn_gate, w_ffn_up, w_ffn_down):
    n_p, t_main, d = x_prompt.shape
    n_s, t_s, _ = x_sample.shape
    assert t_s == N_META, "aux rows hold equal-length meta and running-stream sequences"
    xm_f = x_prompt.reshape(n_p * t_main, d)
    xa_f = jnp.concatenate([jnp.broadcast_to(meta_tokens.astype(x_prompt.dtype)[None], (n_p, N_META, d)),
                            x_sample], axis=0).reshape((n_p + n_s) * t_s, d)
    xm_b, xa_b = xm_f.astype(bf16), xa_f.astype(bf16)

    ffn_w = (w_ffn_gate.astype(bf16), w_ffn_up.astype(bf16), w_ffn_down.astype(bf16))
    even_out, odd_out = [], []
    for layer in range(DEPTH):
        if layer % 2 == 0:
            e = layer // 2
            w = _even_weights(e, w_in_a, b_fgate_a, conv_w, conv_b, dt_bias, a_log, d_skip, ssd_norm_g, w_out_a)
            cache = (cache_fox_k[e], cache_fox_v[e], cache_fox_logf[e], state_ssd_conv[e], state_ssd[e])
            w["ln_g"], w["ln_b"] = ln_mix_g[layer], ln_mix_b[layer]
            (x1m_f, x1m_b, x1a_f, x1a_b), new = _even_layer(xm_b, xm_f, xa_b, xa_f, n_p, t_main, cache, w)
            even_out.append(new)
        else:
            o = layer // 2
            w = _odd_weights(o, w_in_c, b_igate_c, b_fgate_c, mlstm_norm_g, w_out_c)
            cache = (state_mlstm_c[o], state_mlstm_n[o], state_mlstm_m[o])
            w["ln_g"], w["ln_b"] = ln_mix_g[layer], ln_mix_b[layer]
            (x1m_f, x1m_b, x1a_f, x1a_b), new = _odd_layer(xm_b, xm_f, xa_b, xa_f, n_p, t_main, cache, w)
            odd_out.append(new)
        xm_f, xm_b = _ffn_ln(x1m_b, x1m_f, *ffn_w, layer, ln_ffn_g[layer], ln_ffn_b[layer])
        xa_f, xa_b = _ffn_ln(x1a_b, x1a_f, *ffn_w, layer, ln_ffn_g[layer], ln_ffn_b[layer])

    y_prompt = xm_f.reshape(n_p, t_main, d)
    y_sample = xa_f.reshape(n_p + n_s, t_s, d)[n_p:]
    st = lambda outs, key: jnp.stack([o[key] for o in outs])
    return (y_prompt, y_sample,
            st(even_out, "k_p"), st(even_out, "v_p"), st(even_out, "f_p"), st(even_out, "conv_p"), st(even_out, "h_p"),
            st(odd_out, "c_p"), st(odd_out, "n_p"), st(odd_out, "m_p"),
            st(even_out, "k_s"), st(even_out, "v_s"), st(even_out, "f_s"), st(even_out, "conv_s"), st(even_out, "h_s"),
            st(odd_out, "c_s"), st(odd_out, "n_s"), st(odd_out, "m_s"))
```
